```python
import math
import jax, jax.numpy as jnp
from jax import lax
import numpy as np

D_MODEL = 1024
BATCH = 8
SEQ = 2048
DEPTH = 4
DEC_BATCH = 32
DEC_SEQ = 8
PAST_LEN = 8192
PAGE_SIZE = 128

HEAD_DIM = 64

SGU_WIDTH = D_MODEL // 4
SGU_GROUPS = SGU_WIDTH // HEAD_DIM
SGU_CHUNK = 128
ATTN_WIDTH = 3 * D_MODEL // 8
ATTN_HEADS = ATTN_WIDTH // HEAD_DIM
RWKV_WIDTH = D_MODEL - SGU_WIDTH - ATTN_WIDTH
RWKV_HEADS = RWKV_WIDTH // HEAD_DIM

MOBA_BLOCK = 256
MOBA_TOPK = 3
MOBA_QCHUNK = 16

REL_BUCKETS = 32
REL_MAX_DIST = 1024

DECAY_LORA = 64
AAA_LORA = 64
GATE_LORA = 128
RWKV_IN = 3 * RWKV_WIDTH + DECAY_LORA + AAA_LORA + GATE_LORA
GN_EPS = 64e-5

IN_WIDTH = 2 * SGU_WIDTH + 3 * ATTN_WIDTH + RWKV_IN

N_EXPERTS = 16
N_EXPERT_GROUPS = 4
EXPERTS_PER_GROUP = N_EXPERTS // N_EXPERT_GROUPS
TOP_K_EXPERTS = 2
D_EXPERT = D_MODEL // 4

DN_ALPHA = (2 * DEPTH) ** 0.25
DN_BETA = (8 * DEPTH) ** -0.25
LN_EPS = 1e-5

kernel_name = 'hymba_moba_rwkv7_sgu_moe_step'


def layer_norm(x, g, b, eps=LN_EPS):
    xf = x.astype(jnp.float32)
    mu = jnp.mean(xf, axis=-1, keepdims=True)
    var = jnp.mean(jnp.square(xf - mu), axis=-1, keepdims=True)
    return ((xf - mu) * lax.rsqrt(var + eps)).astype(x.dtype) * g + b


def split_cols(x, widths):
    cuts = [int(s) for s in np.cumsum(widths)[:-1]]
    return jnp.split(x, cuts, axis=-1)


def t5_bucket(dist):
    max_exact = REL_BUCKETS // 2
    d = jnp.maximum(dist, 0)
    df = jnp.maximum(d, 1).astype(jnp.float32)
    large = max_exact + (jnp.log(df / max_exact)
                         / math.log(REL_MAX_DIST / max_exact)
                         * (REL_BUCKETS - max_exact)).astype(jnp.int32)
    large = jnp.minimum(large, REL_BUCKETS - 1)
    return jnp.where(d < max_exact, d, large)


def chunk_sgu(pu, pv, w_s, b_s, ln_g, ln_b):
    bsz, t_len, _ = pu.shape
    u = jax.nn.gelu(pu)
    v = jax.nn.gelu(pv).reshape(bsz, t_len, SGU_GROUPS, HEAD_DIM)
    v = layer_norm(v, ln_g.reshape(SGU_GROUPS, HEAD_DIM), ln_b.reshape(SGU_GROUPS, HEAD_DIM))
    n_chunks = -(-t_len // SGU_CHUNK)
    pad = n_chunks * SGU_CHUNK - t_len
    vc = jnp.pad(v, ((0, 0), (0, pad), (0, 0), (0, 0))).reshape(
        bsz, n_chunks, SGU_CHUNK, SGU_GROUPS, HEAD_DIM)
    causal = jnp.tril(jnp.ones((SGU_CHUNK, SGU_CHUNK), dtype=bool))
    w_masked = jnp.where(causal, w_s, jnp.zeros_like(w_s))
    s = jnp.einsum('gts,bcsge->bctge', w_masked, vc) + b_s.T[None, None, :, :, None]
    s = s.reshape(bsz, n_chunks * SGU_CHUNK, SGU_WIDTH)[:, :t_len]
    return u * s, v.reshape(bsz, t_len, SGU_WIDTH)


def moba_attention(q, k_all, v_all, q_start, rel_table):
    bsz, t_q, n_h, d_h = q.shape
    l_k = k_all.shape[1]
    nb = -(-l_k // MOBA_BLOCK)
    pad = nb * MOBA_BLOCK - l_k
    kb = jnp.pad(k_all, ((0, 0), (0, pad), (0, 0), (0, 0))).reshape(bsz, nb, MOBA_BLOCK, n_h, d_h)
    vb = jnp.pad(v_all, ((0, 0), (0, pad), (0, 0), (0, 0))).reshape(bsz, nb, MOBA_BLOCK, n_h, d_h)
    k_mean = jnp.mean(kb.astype(jnp.float32), axis=2)
    kb_t = kb.transpose(0, 3, 1, 2, 4)
    vb_t = vb.transpose(0, 3, 1, 2, 4)
    n_sel = min(MOBA_TOPK, nb)
    bias_hr = rel_table.T
    scale = HEAD_DIM ** -0.5
    gather = jax.vmap(jax.vmap(lambda blocks, i: blocks[i]))

    def attend(args):
        q_c, pos_c = args
        qc = pos_c.shape[0]
        own = pos_c // MOBA_BLOCK
        blk = jnp.einsum('bqhd,bjhd->bhqj', q_c.astype(jnp.float32), k_mean)
        past = jnp.arange(nb)[None, :] < own[:, None]
        blk = jnp.where(past[None, None], blk, -jnp.inf)
        _, top_i = lax.top_k(blk, n_sel)
        sel_ok = jnp.arange(n_sel)[None, :] < own[:, None]
        idx = jnp.concatenate(
            [top_i, jnp.broadcast_to(own[None, None, :, None], (bsz, n_h, qc, 1))], axis=-1)
        ok = jnp.concatenate(
            [jnp.broadcast_to(sel_ok[None, None], top_i.shape),
             jnp.ones((bsz, n_h, qc, 1), dtype=bool)], axis=-1)
        kg = gather(kb_t, idx)
        vg = gather(vb_t, idx)
        kpos = idx[..., None] * MOBA_BLOCK + jnp.arange(MOBA_BLOCK)
        dist = pos_c[None, None, :, None, None] - kpos
        valid = ok[..., None] & (dist >= 0)
        bias = bias_hr[jnp.arange(n_h)[None, :, None, None, None], t5_bucket(dist)]
        logits = jnp.einsum('bqhd,bhqnkd->bhqnk', q_c, kg).astype(jnp.float32) * scale + bias
        logits = jnp.where(valid, logits, -1e30)
        probs = jax.nn.softmax(logits.reshape(bsz, n_h, qc, -1), axis=-1).reshape(logits.shape)
        return jnp.einsum('bhqnk,bhqnkd->bqhd', probs.astype(vg.dtype), vg)

    qcs = min(MOBA_QCHUNK, t_q)
    n_c = -(-t_q // qcs)
    q_pad = jnp.pad(q, ((0, 0), (0, n_c * qcs - t_q), (0, 0), (0, 0)))
    q_chunks = q_pad.reshape(bsz, n_c, qcs, n_h, d_h).transpose(1, 0, 2, 3, 4)
    pos = jnp.minimum(q_start + jnp.arange(n_c * qcs), l_k - 1).reshape(n_c, qcs)
    out = lax.map(attend, (q_chunks, pos))
    return out.transpose(1, 0, 2, 3, 4).reshape(bsz, n_c * qcs, n_h * d_h)[:, :t_q]


def rwkv7_mix(p_in, shift0, s0, p):
    bsz, t_len, _ = p_in.shape
    prev = jnp.concatenate([shift0[:, None, :].astype(p_in.dtype), p_in[:, :-1]], axis=1)
    xs = p_in + (prev - p_in) * p['rwkv_mu']
    r, k, v, xw, xa, xg = split_cols(xs, [RWKV_WIDTH] * 3 + [DECAY_LORA, AAA_LORA, GATE_LORA])
    w_log = -jax.nn.softplus(-(p['rwkv_w0'] + jnp.tanh(xw) @ p['rwkv_w_w2'])) - 0.5
    decay = jnp.exp(-jnp.exp(w_log.astype(jnp.float32)))
    a = jax.nn.sigmoid(p['rwkv_a0'] + xa @ p['rwkv_w_a2'])
    g = jax.nn.sigmoid(xg) @ p['rwkv_w_g2']
    hd = lambda t: t.reshape(bsz, t_len, RWKV_HEADS, HEAD_DIM).astype(jnp.float32)
    kk = hd(k * p['rwkv_k_k'])
    kk = kk / jnp.maximum(jnp.linalg.norm(kk, axis=-1, keepdims=True), 1e-12)
    k = k * (1 + (a - 1) * p['rwkv_k_a'])
    r_h, k_h, v_h, a_h, w_h = hd(r), hd(k), hd(v), hd(a), hd(decay)

    def step(state, inp):
        r_t, w_t, k_t, v_t, kk_t, a_t = inp
        sa = jnp.einsum('bhvk,bhk->bhv', state, -kk_t)
        state = (state * w_t[:, :, None, :]
                 + sa[..., None] * (kk_t * a_t)[:, :, None, :]
                 + v_t[..., None] * k_t[:, :, None, :])
        return state, jnp.einsum('bhvk,bhk->bhv', state, r_t)

    seq = tuple(jnp.moveaxis(t, 1, 0) for t in (r_h, w_h, k_h, v_h, kk, a_h))
    s_final, o = lax.scan(step, s0.astype(jnp.float32), seq)
    o = jnp.moveaxis(o, 0, 1)
    mu = jnp.mean(o, axis=-1, keepdims=True)
    var = jnp.mean(jnp.square(o - mu), axis=-1, keepdims=True)
    o = (((o - mu) * lax.rsqrt(var + GN_EPS)).reshape(bsz, t_len, RWKV_WIDTH)
         * p['rwkv_lnx_g'] + p['rwkv_lnx_b'])
    bonus = jnp.sum(r_h * k_h * p['rwkv_r_k'], axis=-1, keepdims=True) * v_h
    o = (o + bonus.reshape(bsz, t_len, RWKV_WIDTH)) * g
    return o.astype(p_in.dtype), p_in[:, -1], s_final.astype(s0.dtype)


def moe_ffn(h, p):
    aff = jax.nn.sigmoid((h @ p['w_router']).astype(jnp.float32))
    sel = aff + p['router_bias'].astype(jnp.float32)
    grouped = sel.reshape(sel.shape[:-1] + (N_EXPERT_GROUPS, EXPERTS_PER_GROUP))
    grp_score = jnp.sum(lax.top_k(grouped, TOP_K_EXPERTS)[0], axis=-1)
    grp = jnp.argmax(grp_score, axis=-1)
    in_grp = (jnp.arange(N_EXPERTS) // EXPERTS_PER_GROUP) == grp[..., None]
    _, top_i = lax.top_k(jnp.where(in_grp, sel, -jnp.inf), TOP_K_EXPERTS)
    top_a = jnp.take_along_axis(aff, top_i, axis=-1)
    top_w = top_a / jnp.sum(top_a, axis=-1, keepdims=True)
    gates = jnp.sum(jax.nn.one_hot(top_i, N_EXPERTS, dtype=jnp.float32) * top_w[..., None], axis=-2)
    hg = jnp.einsum('btd,edf->btef', h, p['moe_w_gate'])
    hu = jnp.einsum('btd,edf->btef', h, p['moe_w_up'])
    act = jax.nn.silu(hg) * hu * gates[..., None].astype(h.dtype)
    return jnp.einsum('btef,efd->btd', act, p['moe_w_down'])


def hybrid_layer(x, c, k_past, v_past, s0, shift0, p):
    bsz, t_len, _ = x.shape
    mod = (jax.nn.silu(c) @ p['w_ada'] + p['b_ada'])[:, None, :]
    sh1, sc1, g1, sh2, sc2, g2 = jnp.split(mod, 6, axis=-1)

    h = x * (1 + sc1) + sh1
    pu, pv, q, k, v, prw = split_cols(
        h @ p['w_in'], [SGU_WIDTH, SGU_WIDTH, ATTN_WIDTH, ATTN_WIDTH, ATTN_WIDTH, RWKV_IN])
    y_a, sgu_v = chunk_sgu(pu, pv, p['sgu_w'], p['sgu_b'], p['sgu_ln_g'], p['sgu_ln_b'])
    heads = lambda t: t.reshape(bsz, t_len, ATTN_HEADS, HEAD_DIM)
    q, k, v = heads(q), heads(k), heads(v)
    if k_past is None:
        k_all, v_all, q_start = k, v, 0
    else:
        k_all = jnp.concatenate([k_past, k], axis=1)
        v_all = jnp.concatenate([v_past, v], axis=1)
        q_start = k_past.shape[1]
    y_b = moba_attention(q, k_all, v_all, q_start, p['rel_bias'])
    y_c, shift_new, s_new = rwkv7_mix(prw, shift0, s0, p)
    mix = jnp.concatenate([y_a, y_b, y_c], axis=-1) @ p['w_out']
    x = layer_norm(DN_ALPHA * x + (1 + g1) * mix, p['ln1_g'], p['ln1_b'])

    h2 = x * (1 + sc2) + sh2
    x = layer_norm(DN_ALPHA * x + (1 + g2) * moe_ffn(h2, p), p['ln2_g'], p['ln2_b'])
    return x, k, v, s_new, shift_new, sgu_v


def setup_inputs(seed: int = 0) -> dict:
    key = jax.random.key(seed)
    ks = iter(jax.random.split(key, 64))
    nrm = lambda shape, s: jax.random.normal(next(ks), shape, jnp.float32) * s
    n_pages = PAST_LEN // PAGE_SIZE
    n_used = DEC_BATCH * n_pages
    n_pool = n_used + max(1, n_used // 4)
    page_table = jax.random.permutation(next(ks), n_pool)[:n_used].reshape(
        DEC_BATCH, n_pages).astype(jnp.int32)
    D = D_MODEL
    return {
        'x_prompt': nrm((BATCH, SEQ, D), 1.0),
        'x_sample': nrm((DEC_BATCH, DEC_SEQ, D), 1.0),
        'cache_k': nrm((DEPTH, n_pool, PAGE_SIZE, ATTN_HEADS, HEAD_DIM), 1.0),
        'cache_v': nrm((DEPTH, n_pool, PAGE_SIZE, ATTN_HEADS, HEAD_DIM), 1.0),
        'state_rwkv': nrm((DEPTH, DEC_BATCH, RWKV_HEADS, HEAD_DIM, HEAD_DIM), 0.1),
        'state_shift': nrm((DEPTH, DEC_BATCH, RWKV_IN), 1.0),
        'page_table': page_table,
        'c_prompt': nrm((BATCH, D), 1.0),
        'c_sample': nrm((DEC_BATCH, D), 1.0),
        'w_ada': nrm((DEPTH, D, 6 * D), 0.5 * D ** -0.5),
        'b_ada': nrm((DEPTH, 6 * D), 0.02),
        'w_in': nrm((DEPTH, D, IN_WIDTH), D ** -0.5),
        'w_out': nrm((DEPTH, D, D), D ** -0.5 * DN_BETA),
        'sgu_w': nrm((DEPTH, SGU_GROUPS, SGU_CHUNK, SGU_CHUNK), SGU_CHUNK ** -0.5),
        'sgu_b': 1.0 + nrm((DEPTH, SGU_GROUPS, SGU_CHUNK), 0.1),
        'sgu_ln_g': 1.0 + nrm((DEPTH, SGU_WIDTH), 0.1),
        'sgu_ln_b': nrm((DEPTH, SGU_WIDTH), 0.02),
        'rel_bias': nrm((REL_BUCKETS, ATTN_HEADS), 0.5),
        'rwkv_mu': jax.random.uniform(next(ks), (DEPTH, RWKV_IN), jnp.float32),
        'rwkv_w0': jax.random.uniform(next(ks), (DEPTH, RWKV_WIDTH), jnp.float32, -6.0, 0.0),
        'rwkv_w_w2': nrm((DEPTH, DECAY_LORA, RWKV_WIDTH), 0.1),
        'rwkv_a0': nrm((DEPTH, RWKV_WIDTH), 0.1),
        'rwkv_w_a2': nrm((DEPTH, AAA_LORA, RWKV_WIDTH), AAA_LORA ** -0.5),
        'rwkv_w_g2': nrm((DEPTH, GATE_LORA, RWKV_WIDTH), GATE_LORA ** -0.5),
        'rwkv_k_k': 0.85 + nrm((DEPTH, RWKV_WIDTH), 0.05),
        'rwkv_k_a': 1.0 + nrm((DEPTH, RWKV_WIDTH), 0.05),
        'rwkv_r_k': nrm((DEPTH, RWKV_HEADS, HEAD_DIM), 0.1),
        'rwkv_lnx_g': 1.0 + nrm((DEPTH, RWKV_WIDTH), 0.1),
        'rwkv_lnx_b': nrm((DEPTH, RWKV_WIDTH), 0.02),
        'ln1_g': 1.0 + nrm((DEPTH, D), 0.1),
        'ln1_b': nrm((DEPTH, D), 0.02),
        'ln2_g': 1.0 + nrm((DEPTH, D), 0.1),
        'ln2_b': nrm((DEPTH, D), 0.02),
        'w_router': nrm((D, N_EXPERTS), D ** -0.5),
        'router_bias': nrm((N_EXPERTS,), 0.01),
        'moe_w_gate': nrm((DEPTH, N_EXPERTS, D, D_EXPERT), D ** -0.5),
        'moe_w_up': nrm((DEPTH, N_EXPERTS, D, D_EXPERT), D ** -0.5),
        'moe_w_down': nrm((DEPTH, N_EXPERTS, D_EXPERT, D), D_EXPERT ** -0.5 * DN_BETA),
    }


def reference(x_prompt, x_sample, cache_k, cache_v, state_rwkv, state_shift, page_table,
              c_prompt, c_sample, w_ada, b_ada, w_in, w_out, sgu_w, sgu_b, sgu_ln_g, sgu_ln_b,
              rel_bias, rwkv_mu, rwkv_w0, rwkv_w_w2, rwkv_a0, rwkv_w_a2, rwkv_w_g2, rwkv_k_k,
              rwkv_k_a, rwkv_r_k, rwkv_lnx_g, rwkv_lnx_b, ln1_g, ln1_b, ln2_g, ln2_b,
              w_router, router_bias, moe_w_gate, moe_w_up, moe_w_down):
    xp, xs = x_prompt, x_sample
    bp = xp.shape[0]
    dec_b, n_pages = page_table.shape
    past_len = n_pages * cache_k.shape[2]
    kp_l, vp_l, ks_l, vs_l, sp_l, ss_l, shp_l, shs_l, sgu_l = ([] for _ in range(9))
    for l in range(DEPTH):
        p = {
            'rel_bias': rel_bias, 'w_router': w_router, 'router_bias': router_bias,
            'w_ada': w_ada[l], 'b_ada': b_ada[l], 'w_in': w_in[l], 'w_out': w_out[l],
            'sgu_w': sgu_w[l], 'sgu_b': sgu_b[l], 'sgu_ln_g': sgu_ln_g[l], 'sgu_ln_b': sgu_ln_b[l],
            'rwkv_mu': rwkv_mu[l], 'rwkv_w0': rwkv_w0[l], 'rwkv_w_w2': rwkv_w_w2[l],
            'rwkv_a0': rwkv_a0[l], 'rwkv_w_a2': rwkv_w_a2[l], 'rwkv_w_g2': rwkv_w_g2[l],
            'rwkv_k_k': rwkv_k_k[l], 'rwkv_k_a': rwkv_k_a[l], 'rwkv_r_k': rwkv_r_k[l],
            'rwkv_lnx_g': rwkv_lnx_g[l], 'rwkv_lnx_b': rwkv_lnx_b[l],
            'ln1_g': ln1_g[l], 'ln1_b': ln1_b[l], 'ln2_g': ln2_g[l], 'ln2_b': ln2_b[l],
            'moe_w_gate': moe_w_gate[l], 'moe_w_up': moe_w_up[l], 'moe_w_down': moe_w_down[l],
        }

        s0_p = jnp.zeros((bp, RWKV_HEADS, HEAD_DIM, HEAD_DIM), xp.dtype)
        sh0_p = jnp.zeros((bp, RWKV_IN), xp.dtype)
        xp, kp, vp, sp, shp, _ = hybrid_layer(xp, c_prompt, None, None, s0_p, sh0_p, p)

        k_past = cache_k[l][page_table].reshape(dec_b, past_len, ATTN_HEADS, HEAD_DIM)
        v_past = cache_v[l][page_table].reshape(dec_b, past_len, ATTN_HEADS, HEAD_DIM)
        xs, ks_, vs_, ss, shs, sgu_v = hybrid_layer(
            xs, c_sample, k_past, v_past, state_rwkv[l], state_shift[l], p)

        kp_l.append(kp); vp_l.append(vp); ks_l.append(ks_); vs_l.append(vs_)
        sp_l.append(sp); ss_l.append(ss); shp_l.append(shp); shs_l.append(shs); sgu_l.append(sgu_v)
    return (xp, xs, jnp.stack(kp_l), jnp.stack(vp_l), jnp.stack(ks_l), jnp.stack(vs_l),
            jnp.stack(sp_l), jnp.stack(ss_l), jnp.stack(shp_l), jnp.stack(shs_l), jnp.stack(sgu_l))
```

```python
import functools
import math

import jax
import jax.numpy as jnp
import numpy as np
from jax import lax
from jax.experimental import pallas as pl
from jax.experimental.pallas import tpu as pltpu

HEAD_DIM = 64
LANES = 128
SGU_CHUNK = 128
MOBA_BLOCK = 256
MOBA_TOPK = 3
REL_BUCKETS = 32
REL_MAX_DIST = 1024
DECAY_LORA = 64
AAA_LORA = 64
GATE_LORA = 128
GN_EPS = 64e-5
LN_EPS = 1e-5
N_EXPERT_GROUPS = 4
TOP_K_EXPERTS = 2
VMEM_LIMIT = 56 * 1024 * 1024
NEG_BIG = -1e30

f32 = jnp.float32
bf16 = jnp.bfloat16


def _cparams(sem):
    return pltpu.CompilerParams(dimension_semantics=sem, vmem_limit_bytes=VMEM_LIMIT)


def _dot(a, b):
    return jnp.dot(a, b, preferred_element_type=f32)


def _dot_nt(a, b):
    return lax.dot_general(a, b, (((1,), (1,)), ((), ())), preferred_element_type=f32)


def _dot_f32(a, b):
    return _dot(a.astype(bf16), b.astype(bf16))


def _dot_nt_f32(a, b):
    return _dot_nt(a.astype(bf16), b.astype(bf16))


def _gelu(x):
    c = math.sqrt(2.0 / math.pi)
    return 0.5 * x * (1.0 + jnp.tanh(c * (x + 0.044715 * (x * x * x))))


def _sigmoid(x):
    return 1.0 / (1.0 + jnp.exp(-x))


def _silu(x):
    return x * _sigmoid(x)


def _t5_bias(dist, rel_ref, head):
    max_exact = REL_BUCKETS // 2
    d = jnp.maximum(dist, 0)
    df = jnp.maximum(d, 1).astype(f32)
    large = max_exact + (jnp.log(df / max_exact) / math.log(REL_MAX_DIST / max_exact)
                         * (REL_BUCKETS - max_exact)).astype(jnp.int32)
    large = jnp.minimum(large, REL_BUCKETS - 1)
    bucket = jnp.where(d < max_exact, d, large)
    out = jnp.zeros(dist.shape, f32)
    for r in range(REL_BUCKETS):
        out = jnp.where(bucket == r, rel_ref[r, head], out)
    return out


def _bias_tiles_kernel(rel_ref, o_ref):
    h = pl.program_id(0)
    delta = pl.program_id(1)
    i = lax.broadcasted_iota(jnp.int32, (MOBA_BLOCK, MOBA_BLOCK), 0)
    j = lax.broadcasted_iota(jnp.int32, (MOBA_BLOCK, MOBA_BLOCK), 1)
    dist = delta * MOBA_BLOCK + i - j
    o_ref[0, 0] = _t5_bias(dist, rel_ref, h)


def bias_tiles(rel_bias, n_heads, n_blocks):
    return pl.pallas_call(
        _bias_tiles_kernel,
        grid=(n_heads, n_blocks),
        in_specs=[pl.BlockSpec(memory_space=pltpu.SMEM)],
        out_specs=pl.BlockSpec((1, 1, MOBA_BLOCK, MOBA_BLOCK), lambda h, d: (h, d, 0, 0)),
        out_shape=jax.ShapeDtypeStruct((n_heads, n_blocks, MOBA_BLOCK, MOBA_BLOCK), f32),
        name="bias_tiles",
        compiler_params=_cparams(("arbitrary", "arbitrary")),
    )(rel_bias)


def _bias_rows_kernel(rel_ref, o_ref, *, q_start, t_q):
    h = pl.program_id(0)
    n_keys = o_ref.shape[2]
    i = lax.broadcasted_iota(jnp.int32, (t_q, n_keys), 0)
    j = lax.broadcasted_iota(jnp.int32, (t_q, n_keys), 1)
    o_ref[0] = _t5_bias(q_start + i - j, rel_ref, h)


def bias_rows(rel_bias, n_heads, q_start, t_q, n_keys):
    return pl.pallas_call(
        functools.partial(_bias_rows_kernel, q_start=q_start, t_q=t_q),
        grid=(n_heads,),
        in_specs=[pl.BlockSpec(memory_space=pltpu.SMEM)],
        out_specs=pl.BlockSpec((1, t_q, n_keys), lambda h: (h, 0, 0)),
        out_shape=jax.ShapeDtypeStruct((n_heads, t_q, n_keys), f32),
        name="bias_rows",
        compiler_params=_cparams(("arbitrary",)),
    )(rel_bias)


def _ada_kernel(c_ref, w_ref, b_ref, o_ref):
    c = _silu(c_ref[...]).astype(bf16)
    o_ref[0] = _dot(c, w_ref[0].astype(bf16)) + b_ref[0]


def ada_modulation(c_all, w_ada, b_ada, tn=1536):
    depth, d, n = w_ada.shape
    rows = c_all.shape[0]
    return pl.pallas_call(
        _ada_kernel,
        grid=(depth, n // tn),
        in_specs=[pl.BlockSpec((rows, d), lambda l, j: (0, 0)),
                  pl.BlockSpec((1, d, tn), lambda l, j: (l, 0, j)),
                  pl.BlockSpec((1, 1, tn), lambda l, j: (l, 0, j))],
        out_specs=pl.BlockSpec((1, rows, tn), lambda l, j: (l, 0, j)),
        out_shape=jax.ShapeDtypeStruct((depth, rows, n), f32),
        name="ada_modulation",
        compiler_params=_cparams(("arbitrary", "arbitrary")),
    )(c_all, w_ada, b_ada.reshape(depth, 1, n))


def _inproj_kernel(x_ref, mod_ref, w_ref, *o_refs, d, widths):
    bt, tt, _ = x_ref.shape
    sh = mod_ref[:, :, 0:d]
    sc = mod_ref[:, :, d:2 * d]
    h = (x_ref[...] * (1.0 + sc) + sh).reshape(bt * tt, d).astype(bf16)
    off = 0
    for o_ref, wd in zip(o_refs, widths):
        o_ref[...] = _dot(h, w_ref[:, off:off + wd]).reshape(bt, tt, wd)
        off += wd


def in_projection(x, mod, w_in_bf16, widths, bt, tt):
    b, t, d = x.shape
    n = w_in_bf16.shape[1]
    assert sum(widths) == n and b % bt == 0 and t % tt == 0
    return pl.pallas_call(
        functools.partial(_inproj_kernel, d=d, widths=widths),
        grid=(b // bt, t // tt),
        in_specs=[pl.BlockSpec((bt, tt, d), lambda i, j: (i, j, 0)),
                  pl.BlockSpec((bt, 1, mod.shape[2]), lambda i, j: (i, 0, 0)),
                  pl.BlockSpec((d, n), lambda i, j: (0, 0))],
        out_specs=[pl.BlockSpec((bt, tt, wd), lambda i, j: (i, j, 0)) for wd in widths],
        out_shape=[jax.ShapeDtypeStruct((b, t, wd), f32) for wd in widths],
        name="in_projection",
        compiler_params=_cparams(("arbitrary", "arbitrary")),
    )(x, mod, w_in_bf16)


def _seg_mean_matrix(width, seg):
    r = lax.broadcasted_iota(jnp.int32, (width, width), 0) // seg
    c = lax.broadcasted_iota(jnp.int32, (width, width), 1) // seg
    return jnp.where(r == c, 1.0 / seg, 0.0).astype(bf16)


def _seg_sum_matrix(width, seg):
    r = lax.broadcasted_iota(jnp.int32, (width, width), 0) // seg
    c = lax.broadcasted_iota(jnp.int32, (width, width), 1) // seg
    return jnp.where(r == c, 1.0, 0.0).astype(bf16)


def _seg_reduce(x, m):
    hi = x.astype(bf16)
    lo = (x - hi.astype(f32)).astype(bf16)
    return _dot(hi, m) + _dot(lo, m)


def _sgu_kernel(x_ref, w_ref, bias_ref, g_ref, b_ref, *o_refs, n_chunks, emit_v):
    wd = x_ref.shape[2] // 2
    n_groups = wd // HEAD_DIM
    x = x_ref[0]
    u = _gelu(x[:, :wd])
    v = _gelu(x[:, wd:])
    m = _seg_mean_matrix(wd, HEAD_DIM)
    mu = _seg_reduce(v, m)
    c = v - mu
    var = _seg_reduce(c * c, m)
    vn = c * lax.rsqrt(var + LN_EPS) * g_ref[...] + b_ref[...]
    if emit_v:
        o_refs[1][0] = vn
    row = lax.broadcasted_iota(jnp.int32, (SGU_CHUNK, SGU_CHUNK), 0)
    col = lax.broadcasted_iota(jnp.int32, (SGU_CHUNK, SGU_CHUNK), 1)
    lane_grp = lax.broadcasted_iota(jnp.int32, (SGU_CHUNK, wd), 1) // HEAD_DIM
    w_tril = [jnp.where(col <= row, w_ref[g], 0.0).astype(bf16) for g in range(n_groups)]
    for ci in range(n_chunks):
        rows = slice(ci * SGU_CHUNK, (ci + 1) * SGU_CHUNK)
        vc = vn[rows]
        s = bias_ref[...]
        for g in range(n_groups):
            s = s + _dot(w_tril[g], jnp.where(lane_grp == g, vc, 0.0).astype(bf16))
        o_refs[0][0, rows, :] = u[rows] * s


def sgu(puv, w_s, bias_tile, ln_g, ln_b, rows_per_step, emit_v):
    g_, r_, w2 = puv.shape
    wd = w2 // 2
    assert r_ % rows_per_step == 0 and rows_per_step % SGU_CHUNK == 0
    n_out = 2 if emit_v else 1
    outs = pl.pallas_call(
        functools.partial(_sgu_kernel, n_chunks=rows_per_step // SGU_CHUNK, emit_v=emit_v),
        grid=(g_, r_ // rows_per_step),
        in_specs=[pl.BlockSpec((1, rows_per_step, w2), lambda i, j: (i, j, 0)),
                  pl.BlockSpec(w_s.shape, lambda i, j: (0, 0, 0)),
                  pl.BlockSpec(bias_tile.shape, lambda i, j: (0, 0)),
                  pl.BlockSpec((1, wd), lambda i, j: (0, 0)),
                  pl.BlockSpec((1, wd), lambda i, j: (0, 0))],
        out_specs=[pl.BlockSpec((1, rows_per_step, wd), lambda i, j: (i, j, 0))] * n_out,
        out_shape=[jax.ShapeDtypeStruct((g_, r_, wd), f32)] * n_out,
        name="sgu",
        compiler_params=_cparams(("arbitrary", "arbitrary")),
    )(puv, w_s, bias_tile, ln_g.reshape(1, wd), ln_b.reshape(1, wd))
    return outs


def _topk_block_mask(scores, n_valid, n_cols):
    col = lax.broadcasted_iota(jnp.int32, scores.shape, 1)
    valid = col < n_valid
    sel = jnp.zeros(scores.shape, f32)
    for j in range(n_cols):
        sj = scores[:, j:j + 1]
        beats = jnp.logical_and(valid, jnp.logical_or(scores > sj, jnp.logical_and(scores == sj, col < j)))
        cnt = jnp.sum(beats.astype(f32), axis=1, keepdims=True)
        sel = jnp.where(jnp.logical_and(col == j, jnp.logical_and(cnt < MOBA_TOPK, valid)), 1.0, sel)
    return sel


def _moba_full_kernel(q_ref, k_ref, v_ref, bias_ref, o_ref, k16_ref, v16_ref):
    t_len = q_ref.shape[1]
    nb = t_len // MOBA_BLOCK
    blk = MOBA_BLOCK
    scale = HEAD_DIM ** -0.5
    k2 = k_ref[0]
    k16_ref[...] = k2.astype(bf16)
    v16_ref[...] = v_ref[0].astype(bf16)
    kmean = jnp.mean(k2.reshape(nb, blk, LANES), axis=1)
    lane = lax.broadcasted_iota(jnp.int32, (1, LANES), 1)
    hmask = [(lane // HEAD_DIM == hh).astype(f32) for hh in range(2)]
    row = lax.broadcasted_iota(jnp.int32, (blk, blk), 0)
    colk = lax.broadcasted_iota(jnp.int32, (blk, blk), 1)
    causal = colk <= row
    colb = lax.broadcasted_iota(jnp.int32, (blk, nb), 1)

    def q_block(qb, carry):
        qs = pl.multiple_of(qb * blk, blk)
        q2 = q_ref[0, pl.ds(qs, blk), :]
        k_own = k16_ref[pl.ds(qs, blk), :]
        v_own = v16_ref[pl.ds(qs, blk), :]
        qh, selm, state = [], [], []
        for hh in range(2):
            qm = q2 * hmask[hh]
            selm.append(_topk_block_mask(_dot_nt_f32(qm, kmean), qb, nb))
            q16 = (qm * scale).astype(bf16)
            qh.append(q16)
            logits = jnp.where(causal, _dot_nt(q16, k_own) + bias_ref[hh, 0], NEG_BIG)
            m = jnp.max(logits, axis=1, keepdims=True)
            p = jnp.exp(logits - m)
            state += [m, jnp.sum(p, axis=1, keepdims=True), _dot(p.astype(bf16), v_own)]

        def k_block(kb, st):
            ks = pl.multiple_of(kb * blk, blk)
            k_blk = k16_ref[pl.ds(ks, blk), :]
            v_blk = v16_ref[pl.ds(ks, blk), :]
            new = []
            for hh in range(2):
                m, l, acc = st[3 * hh:3 * hh + 3]
                chosen = jnp.sum(jnp.where(colb == kb, selm[hh], 0.0), axis=1, keepdims=True) > 0.5
                logits = jnp.where(chosen, _dot_nt(qh[hh], k_blk) + bias_ref[hh, qb - kb], NEG_BIG)
                m_new = jnp.maximum(m, jnp.max(logits, axis=1, keepdims=True))
                alpha = jnp.exp(m - m_new)
                p = jnp.exp(logits - m_new)
                new += [m_new, alpha * l + jnp.sum(p, axis=1, keepdims=True),
                        alpha * acc + _dot(p.astype(bf16), v_blk)]
            return tuple(new)

        st = lax.fori_loop(0, qb, k_block, tuple(state))
        out = jnp.where(lane < HEAD_DIM, st[2] / st[1], st[5] / st[4])
        o_ref[0, pl.ds(qs, blk), :] = out
        return carry

    lax.fori_loop(0, nb, q_block, 0)


def moba_full(q, k, v, bias):
    b, t, hw = q.shape
    n_pairs = hw // LANES
    nb = t // MOBA_BLOCK
    qkv_spec = pl.BlockSpec((1, t, LANES), lambda p, i: (i, 0, p))
    return pl.pallas_call(
        _moba_full_kernel,
        grid=(n_pairs, b),
        in_specs=[qkv_spec, qkv_spec, qkv_spec,
                  pl.BlockSpec((2, nb, MOBA_BLOCK, MOBA_BLOCK), lambda p, i: (p, 0, 0, 0))],
        out_specs=pl.BlockSpec((1, t, LANES), lambda p, i: (i, 0, p)),
        out_shape=jax.ShapeDtypeStruct((b, t, hw), f32),
        scratch_shapes=[pltpu.VMEM((t, LANES), bf16), pltpu.VMEM((t, LANES), bf16)],
        name="moba_full",
        compiler_params=_cparams(("arbitrary", "arbitrary")),
    )(q, k, v, bias)


PAGES_PER_STEP = 8


def _pair_queries(q2):
    lane = lax.broadcasted_iota(jnp.int32, (1, LANES), 1)
    return jnp.concatenate([q2 * (lane // HEAD_DIM == hh).astype(f32) for hh in range(2)], axis=0)


def _moba_paged_probs_kernel(pt_ref, q_ref, kn_ref, bias_ref, *refs, n_pairs, page):
    kp_refs = refs[:PAGES_PER_STEP]
    p_ref, pown_ref, logit_ref, bsum_ref = refs[PAGES_PER_STEP:]
    c = pl.program_id(1)
    n_steps = pl.num_programs(1)
    t_q = q_ref.shape[1]
    n_pages = logit_ref.shape[1]
    pages_per_block = MOBA_BLOCK // page
    n_blocks = n_pages // pages_per_block
    scale = HEAD_DIM ** -0.5

    @pl.when(c == 0)
    def _():
        bsum_ref[...] = jnp.zeros_like(bsum_ref)

    qf = [_pair_queries(q_ref[0, :, p * LANES:(p + 1) * LANES]) for p in range(n_pairs)]
    q16 = [(x * scale).astype(bf16) for x in qf]
    for j in range(PAGES_PER_STEP):
        pg = c * PAGES_PER_STEP + j
        kpage = kp_refs[j][0, 0]
        blk_row = pg // pages_per_block
        bsum_ref[pl.ds(blk_row, 1), :] += jnp.sum(kpage, axis=0, keepdims=True)
        for p in range(n_pairs):
            logit_ref[p, pg] = _dot_nt(q16[p], kpage[:, p * LANES:(p + 1) * LANES].astype(bf16))

    @pl.when(c == n_steps - 1)
    def _():
        kmean = bsum_ref[...] * (1.0 / MOBA_BLOCK)
        rq = lax.broadcasted_iota(jnp.int32, (2 * t_q, LANES), 0) % t_q
        ck = lax.broadcasted_iota(jnp.int32, (2 * t_q, LANES), 1)
        own_ok = ck <= rq
        zpad = jnp.zeros((LANES - t_q, LANES), f32)
        for p in range(n_pairs):
            lanes = slice(p * LANES, (p + 1) * LANES)
            selm = _topk_block_mask(_dot_nt_f32(qf[p], kmean[:, lanes]), n_blocks, n_blocks)
            cols = []
            for jb in range(n_blocks):
                cols += [jnp.broadcast_to(selm[:, jb:jb + 1], (2 * t_q, LANES))] * pages_per_block
            chosen = jnp.stack(cols, axis=0) > 0.5
            lg = jnp.where(chosen, logit_ref[p] + bias_ref[p, 0:n_pages], NEG_BIG)
            kn = jnp.concatenate([kn_ref[0, :, lanes], zpad], axis=0).astype(bf16)
            own = jnp.where(own_ok, _dot_nt(q16[p], kn) + bias_ref[p, n_pages], NEG_BIG)
            m = jnp.maximum(jnp.max(jnp.max(lg, axis=0), axis=1, keepdims=True),
                            jnp.max(own, axis=1, keepdims=True))
            e = jnp.exp(lg - m)
            e_own = jnp.exp(own - m)
            denom = (jnp.sum(jnp.sum(e, axis=0), axis=1, keepdims=True)
                     + jnp.sum(e_own, axis=1, keepdims=True))
            inv = 1.0 / denom
            p_ref[0, p] = e * inv
            pown_ref[0, p] = e_own * inv


def _moba_paged_mix_kernel(pt_ref, p_ref, pown_ref, vn_ref, *refs, n_pairs):
    vp_refs = refs[:PAGES_PER_STEP]
    o_ref, acc_ref = refs[PAGES_PER_STEP:]
    c = pl.program_id(1)
    n_steps = pl.num_programs(1)
    t_q = vn_ref.shape[1]

    @pl.when(c == 0)
    def _():
        acc_ref[...] = jnp.zeros_like(acc_ref)

    for j in range(PAGES_PER_STEP):
        vpage = vp_refs[j][0, 0]
        for p in range(n_pairs):
            acc_ref[p] += _dot(p_ref[0, p, j].astype(bf16), vpage[:, p * LANES:(p + 1) * LANES].astype(bf16))

    @pl.when(c == n_steps - 1)
    def _():
        lane = lax.broadcasted_iota(jnp.int32, (1, LANES), 1)
        zpad = jnp.zeros((LANES - t_q, LANES), f32)
        for p in range(n_pairs):
            lanes = slice(p * LANES, (p + 1) * LANES)
            vn = jnp.concatenate([vn_ref[0, :, lanes], zpad], axis=0).astype(bf16)
            acc = acc_ref[p] + _dot(pown_ref[0, p].astype(bf16), vn)
            o_ref[0, :, lanes] = jnp.where(lane < HEAD_DIM, acc[0:t_q], acc[t_q:2 * t_q])


def moba_paged(q, k_new, v_new, cache_k, cache_v, layer, page_table, bias_pages):
    b, t_q, hw = q.shape
    n_pairs = hw // LANES
    n_pages = page_table.shape[1]
    page = cache_k.shape[2]
    assert page == LANES and n_pages % PAGES_PER_STEP == 0 and MOBA_BLOCK % page == 0
    n_steps = n_pages // PAGES_PER_STEP
    rows = 2 * t_q

    def page_spec(j):
        return pl.BlockSpec((1, 1, page, hw),
                            lambda i, c, pt: (layer, pt[i, c * PAGES_PER_STEP + j], 0, 0))

    new_spec = pl.BlockSpec((1, t_q, hw), lambda i, c, pt: (i, 0, 0))
    probs, p_own = pl.pallas_call(
        functools.partial(_moba_paged_probs_kernel, n_pairs=n_pairs, page=page),
        grid_spec=pltpu.PrefetchScalarGridSpec(
            num_scalar_prefetch=1,
            grid=(b, n_steps),
            in_specs=[new_spec, new_spec,
                      pl.BlockSpec(bias_pages.shape, lambda i, c, pt: (0, 0, 0, 0))]
                     + [page_spec(j) for j in range(PAGES_PER_STEP)],
            out_specs=[pl.BlockSpec((1, n_pairs, n_pages, rows, LANES), lambda i, c, pt: (i, 0, 0, 0, 0)),
                       pl.BlockSpec((1, n_pairs, rows, LANES), lambda i, c, pt: (i, 0, 0, 0))],
            scratch_shapes=[pltpu.VMEM((n_pairs, n_pages, rows, LANES), f32),
                            pltpu.VMEM((n_pages * page // MOBA_BLOCK, hw), f32)]),
        out_shape=[jax.ShapeDtypeStruct((b, n_pairs, n_pages, rows, LANES), f32),
                   jax.ShapeDtypeStruct((b, n_pairs, rows, LANES), f32)],
        name="moba_paged_probs",
        compiler_params=_cparams(("arbitrary", "arbitrary")),
    )(page_table, q, k_new, bias_pages, *([cache_k] * PAGES_PER_STEP))
    return pl.pallas_call(
        functools.partial(_moba_paged_mix_kernel, n_pairs=n_pairs),
        grid_spec=pltpu.PrefetchScalarGridSpec(
            num_scalar_prefetch=1,
            grid=(b, n_steps),
            in_specs=[pl.BlockSpec((1, n_pairs, PAGES_PER_STEP, rows, LANES),
                                   lambda i, c, pt: (i, 0, c, 0, 0)),
                      pl.BlockSpec((1, n_pairs, rows, LANES), lambda i, c, pt: (i, 0, 0, 0)),
                      new_spec]
                     + [page_spec(j) for j in range(PAGES_PER_STEP)],
            out_specs=pl.BlockSpec((1, t_q, hw), lambda i, c, pt: (i, 0, 0)),
            scratch_shapes=[pltpu.VMEM((n_pairs, rows, LANES), f32)]),
        out_shape=jax.ShapeDtypeStruct((b, t_q, hw), f32),
        name="moba_paged_mix",
        compiler_params=_cparams(("arbitrary", "arbitrary")),
    )(page_table, probs, p_own, v_new, *([cache_v] * PAGES_PER_STEP))


def paged_bias_layout(bias_rows_arr, t_q):
    h, _, n_keys = bias_rows_arr.shape
    x = bias_rows_arr.reshape(h // 2, 2, t_q, n_keys // LANES, LANES)
    return x.transpose(0, 3, 1, 2, 4).reshape(h // 2, n_keys // LANES, 2 * t_q, LANES)


def _rwkv_prep_kernel(p_ref, prev_ref, shift_ref, mu_ref, w0_ref, a0_ref, kk_ref, ka_ref, rk_ref,
                      ww_ref, wa_ref, wg_ref, *o_refs, hw):
    bt, tt, n = p_ref.shape
    j = pl.program_id(1)
    p = p_ref[...].reshape(bt * tt, n)
    first = jnp.where(j == 0, shift_ref[...], prev_ref[:, 7:8, :])
    first = jnp.broadcast_to(first, (bt, tt, n)).reshape(bt * tt, n)
    row = lax.broadcasted_iota(jnp.int32, (bt * tt, 1), 0)
    prev = jnp.where(row % tt == 0, first, pltpu.roll(p, 1, 0))
    xs = p + (prev - p) * mu_ref[...]
    r = xs[:, 0:hw]
    k = xs[:, hw:2 * hw]
    v = xs[:, 2 * hw:3 * hw]
    xwa = xs[:, 3 * hw:3 * hw + LANES]
    xg = xs[:, 3 * hw + LANES:3 * hw + 2 * LANES]
    z = -(w0_ref[...] + _dot_f32(jnp.tanh(xwa), ww_ref[...]))
    softplus = jnp.maximum(z, 0.0) + jnp.log(1.0 + jnp.exp(-jnp.abs(z)))
    log_decay = -jnp.exp(-softplus - 0.5)
    a = _sigmoid(a0_ref[...] + _dot_f32(xwa, wa_ref[...]))
    g = _dot_f32(_sigmoid(xg), wg_ref[...])
    mseg = _seg_sum_matrix(hw, HEAD_DIM)
    kk = k * kk_ref[...]
    kk = kk / jnp.maximum(jnp.sqrt(_seg_reduce(kk * kk, mseg)), 1e-12)
    k2 = k * (1.0 + (a - 1.0) * ka_ref[...])
    bonus = _seg_reduce(r * k2 * rk_ref[...], mseg) * v
    outs = (r, log_decay, k2, v, -kk, kk * a, g, bonus)
    for o_ref, val in zip(o_refs, outs):
        o_ref[...] = val.reshape(bt, tt, hw)


def rwkv_prep(prw, shift0, prm, bt, tt):
    b, t, n = prw.shape
    hw = prm['w0'].shape[-1]
    assert n == 3 * hw + 2 * LANES and tt % 8 == 0
    row = lambda x: x.reshape(1, -1)
    vec = lambda: pl.BlockSpec((1, hw), lambda i, j: (0, 0))
    mat = lambda: pl.BlockSpec((LANES, hw), lambda i, j: (0, 0))
    prev_blk = tt // 8
    return pl.pallas_call(
        functools.partial(_rwkv_prep_kernel, hw=hw),
        grid=(b // bt, t // tt),
        in_specs=[pl.BlockSpec((bt, tt, n), lambda i, j: (i, j, 0)),
                  pl.BlockSpec((bt, 8, n), lambda i, j: (i, jnp.maximum(j * prev_blk - 1, 0), 0)),
                  pl.BlockSpec((bt, 1, n), lambda i, j: (i, 0, 0)),
                  pl.BlockSpec((1, n), lambda i, j: (0, 0)),
                  vec(), vec(), vec(), vec(), vec(), mat(), mat(), mat()],
        out_specs=[pl.BlockSpec((bt, tt, hw), lambda i, j: (i, j, 0))] * 8,
        out_shape=[jax.ShapeDtypeStruct((b, t, hw), f32)] * 8,
        name="rwkv_prep",
        compiler_params=_cparams(("arbitrary", "arbitrary")),
    )(prw, prw, shift0.reshape(b, 1, n), row(prm['mu']), row(prm['w0']), row(prm['a0']),
      row(prm['k_k']), row(prm['k_a']), row(prm['r_k']), prm['w_w2p'], prm['w_a2p'], prm['w_g2'])


def rwkv_prep_params(mu, w0, w_w2, a0, w_a2, w_g2, k_k, k_a, r_k):
    hw = w0.shape[-1]
    zeros = jnp.zeros((LANES - DECAY_LORA, hw), w_w2.dtype)
    return dict(mu=mu, w0=w0, a0=a0, k_k=k_k, k_a=k_a, r_k=r_k.reshape(-1),
                w_w2p=jnp.concatenate([w_w2, zeros], axis=0),
                w_a2p=jnp.concatenate([zeros, w_a2], axis=0), w_g2=w_g2)


def _split3(x):
    hi = x.astype(bf16)
    return hi, (x - hi.astype(f32)).astype(bf16)


def _dot3(a, b):
    ah, al = _split3(a)
    bh, bl = _split3(b)
    return _dot(ah, bh) + (_dot(ah, bl) + _dot(al, bh))


def _dot3_nt(a, b):
    ah, al = _split3(a)
    bh, bl = _split3(b)
    return _dot_nt(ah, bh) + (_dot_nt(ah, bl) + _dot_nt(al, bh))


def _dot3_tn(a, b):
    return _dot3(a.T, b)


def _rwkv_chunk_kernel(r_ref, lw_ref, k_ref, v_ref, x_ref, b_ref, g_ref, bonus_ref, s0_ref,
                       lng_ref, lnb_ref, y_ref, sout_ref, state_ref, *, n_pairs):
    c_len = r_ref.shape[1]
    jc = pl.program_id(1)
    n_chunks = pl.num_programs(1)
    two_c = 2 * c_len

    @pl.when(jc == 0)
    def _():
        state_ref[...] = s0_ref[0]

    rr = lax.broadcasted_iota(jnp.int32, (two_c, two_c), 0)
    cc = lax.broadcasted_iota(jnp.int32, (two_c, two_c), 1)
    same_head = (rr // c_len) == (cc // c_len)
    strict = jnp.logical_and(same_head, (cc % c_len) < (rr % c_len))
    incl = jnp.logical_and(same_head, (cc % c_len) <= (rr % c_len))
    eye = (rr == cc).astype(f32)
    row_head = lax.broadcasted_iota(jnp.int32, (two_c, LANES), 0) // c_len
    lane_head = lax.broadcasted_iota(jnp.int32, (two_c, LANES), 1) // HEAD_DIM
    bd = row_head == lane_head
    sv = lax.broadcasted_iota(jnp.int32, (LANES, LANES), 0) // HEAD_DIM
    sk = lax.broadcasted_iota(jnp.int32, (LANES, LANES), 1) // HEAD_DIM
    bd_state = sv == sk
    trow = lax.broadcasted_iota(jnp.int32, (c_len, LANES), 0)
    mseg = _seg_mean_matrix(LANES, HEAD_DIM)
    n_doubling = max(1, (c_len - 1).bit_length())

    def stack(a):
        return jnp.where(bd, jnp.concatenate([a, a], axis=0), 0.0)

    for p in range(n_pairs):
        lanes = slice(p * LANES, (p + 1) * LANES)
        lw = lw_ref[0, :, lanes]
        cum = lw
        sh = 1
        while sh < c_len:
            cum = cum + jnp.where(trow >= sh, pltpu.roll(cum, sh, 0), 0.0)
            sh *= 2
        gam = jnp.exp(cum)
        inv_gam = jnp.exp(-cum)
        a_t = stack(x_ref[0, :, lanes] * jnp.exp(cum - lw))
        r_t = stack(r_ref[0, :, lanes] * gam)
        b_t = stack(b_ref[0, :, lanes] * inv_gam)
        k_t = stack(k_ref[0, :, lanes] * inv_gam)
        v_s = stack(v_ref[0, :, lanes])
        gam_end = gam[c_len - 1:c_len, :]
        s0 = state_ref[p]

        l_ab = jnp.where(strict, _dot3_nt(a_t, b_t), 0.0)
        l_ak = jnp.where(strict, _dot3_nt(a_t, k_t), 0.0)
        m_rb = jnp.where(incl, _dot3_nt(r_t, b_t), 0.0)
        m_rk = jnp.where(incl, _dot3_nt(r_t, k_t), 0.0)
        t_inv = eye + l_ab
        pw = l_ab
        for _ in range(n_doubling - 1):
            pw = _dot3(pw, pw)
            t_inv = t_inv + _dot3(t_inv, pw)
        base_a = _dot3_nt(a_t, s0)
        base_r = _dot3_nt(r_t, s0)
        d_s = _dot3(t_inv, base_a + _dot3(l_ak, v_s))
        o_s = base_r + _dot3(m_rb, d_s) + _dot3(m_rk, v_s)
        o = o_s[0:c_len] + o_s[c_len:two_c]
        s_new = (s0 + _dot3_tn(d_s, b_t) + _dot3_tn(v_s, k_t)) * gam_end
        state_ref[p] = jnp.where(bd_state, s_new, 0.0)

        mu = _seg_reduce(o, mseg)
        cen = o - mu
        var = _seg_reduce(cen * cen, mseg)
        y = cen * lax.rsqrt(var + GN_EPS) * lng_ref[:, lanes] + lnb_ref[:, lanes]
        y_ref[0, :, lanes] = (y + bonus_ref[0, :, lanes]) * g_ref[0, :, lanes]

    @pl.when(jc == n_chunks - 1)
    def _():
        sout_ref[0] = state_ref[...]


def rwkv_chunked(seq, s0_bd, lnx_g, lnx_b, c_len):
    r, lw, k2, v, x, kka, g, bonus = seq
    b, t, hw = r.shape
    n_pairs = hw // LANES
    assert t % c_len == 0
    seq_spec = pl.BlockSpec((1, c_len, hw), lambda i, j: (i, j, 0))
    st_spec = pl.BlockSpec((1, n_pairs, LANES, LANES), lambda i, j: (i, 0, 0, 0))
    vec = pl.BlockSpec((1, hw), lambda i, j: (0, 0))
    return pl.pallas_call(
        functools.partial(_rwkv_chunk_kernel, n_pairs=n_pairs),
        grid=(b, t // c_len),
        in_specs=[seq_spec] * 8 + [st_spec, vec, vec],
        out_specs=[seq_spec, st_spec],
        out_shape=[jax.ShapeDtypeStruct((b, t, hw), f32),
                   jax.ShapeDtypeStruct((b, n_pairs, LANES, LANES), f32)],
        scratch_shapes=[pltpu.VMEM((n_pairs, LANES, LANES), f32)],
        name="rwkv_chunked",
        compiler_params=_cparams(("arbitrary", "arbitrary")),
    )(r, lw, k2, v, x, kka, g, bonus, s0_bd, lnx_g.reshape(1, hw), lnx_b.reshape(1, hw))


def state_to_blockdiag(s):
    b, h, n, _ = s.shape
    sp = s.reshape(b, h // 2, 2, n, n)
    z = jnp.zeros_like(sp[:, :, 0])
    top = jnp.concatenate([sp[:, :, 0], z], axis=-1)
    bot = jnp.concatenate([z, sp[:, :, 1]], axis=-1)
    return jnp.concatenate([top, bot], axis=-2)


def state_from_blockdiag(sbd):
    b, hp, _, _ = sbd.shape
    n = HEAD_DIM
    return jnp.stack([sbd[:, :, :n, :n], sbd[:, :, n:, n:]], axis=2).reshape(b, hp * 2, n, n)


def _layer_norm(z, g, b):
    mu = jnp.mean(z, axis=-1, keepdims=True)
    c = z - mu
    var = jnp.mean(c * c, axis=-1, keepdims=True)
    return c * lax.rsqrt(var + LN_EPS) * g + b


def _route(sel, aff, n_groups, per_group, top_k):
    rows = sel.shape[1]
    srow = [sel[e:e + 1, :] for e in range(n_groups * per_group)]
    arow = [aff[e:e + 1, :] for e in range(n_groups * per_group)]
    scores = []
    for g in range(n_groups):
        s = srow[g * per_group:(g + 1) * per_group]
        best = None
        for i in range(per_group):
            for j in range(i + 1, per_group):
                pair = s[i] + s[j]
                best = pair if best is None else jnp.maximum(best, pair)
        scores.append(best)
    top = scores[0]
    for g in range(1, n_groups):
        top = jnp.maximum(top, scores[g])
    taken = jnp.zeros((1, rows), jnp.bool_)
    eidx = lax.broadcasted_iota(jnp.int32, (n_groups * per_group, rows), 0)
    gates = jnp.zeros((n_groups * per_group, rows), f32)
    for g in range(n_groups):
        is_grp = jnp.logical_and(scores[g] == top, jnp.logical_not(taken))
        taken = jnp.logical_or(taken, is_grp)
        s = srow[g * per_group:(g + 1) * per_group]
        a = arow[g * per_group:(g + 1) * per_group]
        chosen = []
        for i in range(per_group):
            cnt = jnp.zeros((1, rows), f32)
            for j in range(per_group):
                if j != i:
                    ahead = (s[j] > s[i]) if j > i else (s[j] >= s[i])
                    cnt = cnt + ahead.astype(f32)
            chosen.append(jnp.logical_and(cnt < top_k, is_grp))
        denom = jnp.zeros((1, rows), f32)
        for i in range(per_group):
            denom = denom + jnp.where(chosen[i], a[i], 0.0)
        for i in range(per_group):
            gate = jnp.where(chosen[i], a[i] / denom, 0.0)
            gates = jnp.where(eidx == g * per_group + i, gate, gates)
    return gates


def _outproj_kernel(x_ref, ya_ref, yb_ref, yc_ref, mod_ref, w_ref, g_ref, b_ref, wr_ref, rb_ref,
                    x1_ref, h2_ref, gates_ref, *, d, alpha):
    bt, tt, _ = x_ref.shape
    rows = bt * tt
    off = 0
    mix = jnp.zeros((rows, d), f32)
    for y_ref in (ya_ref, yb_ref, yc_ref):
        wd = y_ref.shape[2]
        mix = mix + _dot(y_ref[...].reshape(rows, wd).astype(bf16), w_ref[off:off + wd, :])
        off += wd
    g1 = mod_ref[:, :, 2 * d:3 * d]
    z = alpha * x_ref[...] + (1.0 + g1) * mix.reshape(bt, tt, d)
    x1 = _layer_norm(z, g_ref[...], b_ref[...])
    x1_ref[...] = x1
    sh2 = mod_ref[:, :, 3 * d:4 * d]
    sc2 = mod_ref[:, :, 4 * d:5 * d]
    h2 = (x1 * (1.0 + sc2) + sh2).reshape(rows, d)
    h2_ref[...] = h2.astype(bf16).reshape(bt, tt, d)
    aff = _sigmoid(_dot_nt_f32(wr_ref[...], h2))
    gates_ref[...] = _route(aff + rb_ref[...], aff, N_EXPERT_GROUPS,
                            wr_ref.shape[0] // N_EXPERT_GROUPS, TOP_K_EXPERTS)


def out_projection(x, ya, yb, yc, mod, w_out_bf16, ln_g, ln_b, w_router_t, router_bias, alpha, bt, tt):
    b, t, d = x.shape
    n_exp = w_router_t.shape[0]
    rows = bt * tt
    blk = lambda wd: pl.BlockSpec((bt, tt, wd), lambda i, j: (i, j, 0))
    full = lambda a: pl.BlockSpec(a.shape, lambda i, j: (0,) * a.ndim)
    n_tblk = t // tt
    ln_g = ln_g.reshape(1, d)
    ln_b = ln_b.reshape(1, d)
    rb = router_bias.reshape(n_exp, 1)
    return pl.pallas_call(
        functools.partial(_outproj_kernel, d=d, alpha=alpha),
        grid=(b // bt, n_tblk),
        in_specs=[blk(d), blk(ya.shape[2]), blk(yb.shape[2]), blk(yc.shape[2]),
                  pl.BlockSpec((bt, 1, mod.shape[2]), lambda i, j: (i, 0, 0)),
                  full(w_out_bf16), full(ln_g), full(ln_b), full(w_router_t), full(rb)],
        out_specs=[blk(d), blk(d), pl.BlockSpec((n_exp, rows), lambda i, j: (0, i * n_tblk + j))],
        out_shape=[jax.ShapeDtypeStruct((b, t, d), f32), jax.ShapeDtypeStruct((b, t, d), bf16),
                   jax.ShapeDtypeStruct((n_exp, b * t), f32)],
        name="out_projection",
        compiler_params=_cparams(("arbitrary", "arbitrary")),
    )(x, ya, yb, yc, mod, w_out_bf16, ln_g, ln_b, w_router_t, rb)


def _moe_kernel(h_ref, gates_ref, x1_ref, mod_ref, wg_ref, wu_ref, wd_ref, g_ref, b_ref,
                o_ref, acc_ref, *, d, alpha):
    bt, tt, _ = h_ref.shape
    rows = bt * tt
    e = pl.program_id(2)
    n_exp = pl.num_programs(2)

    @pl.when(e == 0)
    def _():
        acc_ref[...] = jnp.zeros_like(acc_ref)

    h = h_ref[...].reshape(rows, d)
    gates = gates_ref[...].reshape(rows, gates_ref.shape[2])
    lane = lax.broadcasted_iota(jnp.int32, gates.shape, 1)
    gate = jnp.sum(jnp.where(lane == e, gates, 0.0), axis=1, keepdims=True)
    hg = _dot(h, wg_ref[0])
    hu = _dot(h, wu_ref[0])
    act = _silu(hg) * hu * gate
    acc_ref[...] += _dot(act.astype(bf16), wd_ref[0])

    @pl.when(e == n_exp - 1)
    def _():
        g2 = mod_ref[:, :, 5 * d:6 * d]
        z = alpha * x1_ref[...] + (1.0 + g2) * acc_ref[...].reshape(bt, tt, d)
        o_ref[...] = _layer_norm(z, g_ref[...], b_ref[...])


def moe_ffn(h2, gates, x1, mod, wg_bf16, wu_bf16, wd_bf16, ln_g, ln_b, alpha, bt, tt):
    b, t, d = x1.shape
    n_exp = wg_bf16.shape[0]
    f = wg_bf16.shape[2]
    blk = lambda wd: pl.BlockSpec((bt, tt, wd), lambda i, j, e: (i, j, 0))
    return pl.pallas_call(
        functools.partial(_moe_kernel, d=d, alpha=alpha),
        grid=(b // bt, t // tt, n_exp),
        in_specs=[blk(d), blk(n_exp), blk(d),
                  pl.BlockSpec((bt, 1, mod.shape[2]), lambda i, j, e: (i, 0, 0)),
                  pl.BlockSpec((1, d, f), lambda i, j, e: (e, 0, 0)),
                  pl.BlockSpec((1, d, f), lambda i, j, e: (e, 0, 0)),
                  pl.BlockSpec((1, f, d), lambda i, j, e: (e, 0, 0)),
                  pl.BlockSpec((1, d), lambda i, j, e: (0, 0)),
                  pl.BlockSpec((1, d), lambda i, j, e: (0, 0))],
        out_specs=blk(d),
        out_shape=jax.ShapeDtypeStruct((b, t, d), f32),
        scratch_shapes=[pltpu.VMEM((bt * tt, d), f32)],
        name="moe_ffn",
        compiler_params=_cparams(("arbitrary", "arbitrary", "arbitrary")),
    )(h2, gates, x1, mod, wg_bf16, wu_bf16, wd_bf16, ln_g.reshape(1, d), ln_b.reshape(1, d))


def sgu_full_params(w_s, b_s):
    return w_s, jnp.repeat(b_s.T, HEAD_DIM, axis=1)


def sgu_short_params(w_s, b_s, t_len):
    reps = SGU_CHUNK // t_len
    eye = jnp.eye(reps, dtype=w_s.dtype)
    w_small = w_s[:, :t_len, :t_len]
    w_big = jnp.einsum('ab,gts->gatbs', eye, w_small).reshape(w_s.shape[0], SGU_CHUNK, SGU_CHUNK)
    bias = jnp.tile(jnp.repeat(b_s[:, :t_len].T, HEAD_DIM, axis=1), (reps, 1))
    return w_big, bias


PROMPT_ROWS = 512
RWKV_CHUNK = 64


def _hybrid_layer(x, mod, tile, wts, attend, sgu_prm, shift0, s0_bd, rwkv_chunk, emit_sgu_v, alpha):
    bt, tt = tile
    b, t, d = x.shape
    hw_sgu = sgu_prm[0].shape[0] * HEAD_DIM
    hw_attn = wts['attn_width']
    widths = (2 * hw_sgu, hw_attn, hw_attn, hw_attn, wts['w_in'].shape[1] - 2 * hw_sgu - 3 * hw_attn)
    puv, q, k, v, prw = in_projection(x, mod, wts['w_in'], widths, bt, tt)
    n_rows = b * t
    sgu_rows = min(PROMPT_ROWS, n_rows) if t < SGU_CHUNK else tt
    sgu_in = puv.reshape(1, n_rows, 2 * hw_sgu) if t < SGU_CHUNK else puv
    sgu_out = sgu(sgu_in, sgu_prm[0], sgu_prm[1], wts['sgu_ln_g'], wts['sgu_ln_b'], sgu_rows, emit_sgu_v)
    ya = sgu_out[0].reshape(b, t, hw_sgu)
    sgu_v = sgu_out[1].reshape(b, t, hw_sgu) if emit_sgu_v else None
    yb = attend(q, k, v)
    seq = rwkv_prep(prw, shift0, wts['rwkv'], bt, tt)
    yc, s_bd = rwkv_chunked(seq, s0_bd, wts['lnx_g'], wts['lnx_b'], rwkv_chunk)
    x1, h2, gates_t = out_projection(x, ya, yb, yc, mod, wts['w_out'], wts['ln1_g'], wts['ln1_b'],
                                     wts['w_router_t'], wts['router_bias'], alpha, bt, tt)
    gates = gates_t.T.reshape(b, t, gates_t.shape[0])
    x2 = moe_ffn(h2, gates, x1, mod, wts['moe_wg'], wts['moe_wu'], wts['moe_wd'],
                 wts['ln2_g'], wts['ln2_b'], alpha, bt, tt)
    return x2, k, v, state_from_blockdiag(s_bd), prw[:, -1], sgu_v


def kernel(x_prompt, x_sample, cache_k, cache_v, state_rwkv, state_shift, page_table, c_prompt, c_sample, w_ada, b_ada, w_in, w_out, sgu_w, sgu_b, sgu_ln_g, sgu_ln_b, rel_bias, rwkv_mu, rwkv_w0, rwkv_w_w2, rwkv_a0, rwkv_w_a2, rwkv_w_g2, rwkv_k_k, rwkv_k_a, rwkv_r_k, rwkv_lnx_g, rwkv_lnx_b, ln1_g, ln1_b, ln2_g, ln2_b, w_router, router_bias, moe_w_gate, moe_w_up, moe_w_down):
    depth = w_ada.shape[0]
    bp, t_p, d = x_prompt.shape
    bs, t_s, _ = x_sample.shape
    n_heads = cache_k.shape[3]
    hw_attn = n_heads * HEAD_DIM
    n_pool, page = cache_k.shape[1], cache_k.shape[2]
    past_len = page_table.shape[1] * page
    rwkv_in = state_shift.shape[2]
    alpha = (2 * depth) ** 0.25

    mods = ada_modulation(jnp.concatenate([c_prompt, c_sample], axis=0), w_ada, b_ada)
    bias_p = bias_tiles(rel_bias, n_heads, t_p // MOBA_BLOCK)
    bias_s = paged_bias_layout(bias_rows(rel_bias, n_heads, past_len, t_s, past_len + LANES), t_s)
    ck = cache_k.reshape(depth, n_pool, page, hw_attn)
    cv = cache_v.reshape(depth, n_pool, page, hw_attn)
    w_in16, w_out16 = w_in.astype(bf16), w_out.astype(bf16)
    wg16, wu16, wd16 = moe_w_gate.astype(bf16), moe_w_up.astype(bf16), moe_w_down.astype(bf16)
    w_router_t = w_router.T
    zero_shift = jnp.zeros((bp, rwkv_in), f32)
    zero_state = jnp.zeros((bp, n_heads // 2, LANES, LANES), f32)

    xp, xs = x_prompt, x_sample
    outs = [[] for _ in range(9)]
    for l in range(depth):
        wts = dict(
            attn_width=hw_attn, w_in=w_in16[l], w_out=w_out16[l], sgu_ln_g=sgu_ln_g[l], sgu_ln_b=sgu_ln_b[l],
            rwkv=rwkv_prep_params(rwkv_mu[l], rwkv_w0[l], rwkv_w_w2[l], rwkv_a0[l], rwkv_w_a2[l],
                                  rwkv_w_g2[l], rwkv_k_k[l], rwkv_k_a[l], rwkv_r_k[l]),
            lnx_g=rwkv_lnx_g[l], lnx_b=rwkv_lnx_b[l], ln1_g=ln1_g[l], ln1_b=ln1_b[l],
            ln2_g=ln2_g[l], ln2_b=ln2_b[l], w_router_t=w_router_t, router_bias=router_bias,
            moe_wg=wg16[l], moe_wu=wu16[l], moe_wd=wd16[l])
        mod_p = mods[l, :bp][:, None, :]
        mod_s = mods[l, bp:][:, None, :]
        xp, kp, vp, sp, shp, _ = _hybrid_layer(
            xp, mod_p, (1, PROMPT_ROWS), wts, lambda q, k, v: moba_full(q, k, v, bias_p),
            sgu_full_params(sgu_w[l], sgu_b[l]), zero_shift, zero_state, RWKV_CHUNK, False, alpha)
        xs, ks_, vs_, ss, shs, sgu_v = _hybrid_layer(
            xs, mod_s, (bs, t_s), wts,
            lambda q, k, v: moba_paged(q, k, v, ck, cv, l, page_table, bias_s),
            sgu_short_params(sgu_w[l], sgu_b[l], t_s), state_shift[l],
            state_to_blockdiag(state_rwkv[l]), t_s, True, alpha)
        heads = lambda a: a.reshape(a.shape[0], a.shape[1], n_heads, HEAD_DIM)
        for lst, val in zip(outs, (heads(kp), heads(vp), heads(ks_), heads(vs_), sp, ss, shp, shs, sgu_v)):
            lst.append(val)
    return (xp, xs) + tuple(jnp.stack(lst) for lst in outs)
```

```python
import functools
import math

import jax
import jax.numpy as jnp
import numpy as np
from jax import lax
from jax.experimental import pallas as pl
from jax.experimental.pallas import tpu as pltpu

HEAD_DIM = 64
LANES = 128
SGU_CHUNK = 128
MOBA_BLOCK = 256
MOBA_TOPK = 3
REL_BUCKETS = 32
REL_MAX_DIST = 1024
DECAY_LORA = 64
AAA_LORA = 64
GATE_LORA = 128
GN_EPS = 64e-5
LN_EPS = 1e-5
N_EXPERT_GROUPS = 4
TOP_K_EXPERTS = 2
VMEM_LIMIT = 56 * 1024 * 1024
NEG_BIG = -1e30

f32 = jnp.float32
bf16 = jnp.bfloat16


def _cparams(sem):
    return pltpu.CompilerParams(dimension_semantics=sem, vmem_limit_bytes=VMEM_LIMIT)


def _dot(a, b):
    return jnp.dot(a, b, preferred_element_type=f32)


def _dot_nt(a, b):
    return lax.dot_general(a, b, (((1,), (1,)), ((), ())), preferred_element_type=f32)


def _dot_f32(a, b):
    return _dot(a.astype(bf16), b.astype(bf16))


def _dot_nt_f32(a, b):
    return _dot_nt(a.astype(bf16), b.astype(bf16))


def _gelu(x):
    c = math.sqrt(2.0 / math.pi)
    return 0.5 * x * (1.0 + jnp.tanh(c * (x + 0.044715 * (x * x * x))))


def _sigmoid(x):
    return 1.0 / (1.0 + jnp.exp(-x))


def _silu(x):
    return x * _sigmoid(x)


def _t5_bias(dist, rel_ref, head):
    max_exact = REL_BUCKETS // 2
    d = jnp.maximum(dist, 0)
    df = jnp.maximum(d, 1).astype(f32)
    large = max_exact + (jnp.log(df / max_exact) / math.log(REL_MAX_DIST / max_exact)
                         * (REL_BUCKETS - max_exact)).astype(jnp.int32)
    large = jnp.minimum(large, REL_BUCKETS - 1)
    bucket = jnp.where(d < max_exact, d, large)
    out = jnp.zeros(dist.shape, f32)
    for r in range(REL_BUCKETS):
        out = jnp.where(bucket == r, rel_ref[r, head], out)
    return out


def _bias_tiles_kernel(rel_ref, o_ref):
    h = pl.program_id(0)
    delta = pl.num_programs(1) - 1 - pl.program_id(1)
    i = lax.broadcasted_iota(jnp.int32, (MOBA_BLOCK, MOBA_BLOCK), 0)
    j = lax.broadcasted_iota(jnp.int32, (MOBA_BLOCK, MOBA_BLOCK), 1)
    dist = delta * MOBA_BLOCK + j - i
    o_ref[0, 0] = _t5_bias(dist, rel_ref, h)


def bias_tiles(rel_bias, n_heads, n_blocks):
    return pl.pallas_call(
        _bias_tiles_kernel,
        grid=(n_heads, n_blocks),
        in_specs=[pl.BlockSpec(memory_space=pltpu.SMEM)],
        out_specs=pl.BlockSpec((1, 1, MOBA_BLOCK, MOBA_BLOCK), lambda h, d: (h, d, 0, 0)),
        out_shape=jax.ShapeDtypeStruct((n_heads, n_blocks, MOBA_BLOCK, MOBA_BLOCK), f32),
        name="bias_tiles",
        compiler_params=_cparams(("arbitrary", "arbitrary")),
    )(rel_bias)


def _bias_rows_kernel(rel_ref, o_ref, *, q_start, t_q):
    h = pl.program_id(0)
    n_keys = o_ref.shape[2]
    i = lax.broadcasted_iota(jnp.int32, (t_q, n_keys), 0)
    j = lax.broadcasted_iota(jnp.int32, (t_q, n_keys), 1)
    o_ref[0] = _t5_bias(q_start + i - j, rel_ref, h)


def bias_rows(rel_bias, n_heads, q_start, t_q, n_keys):
    return pl.pallas_call(
        functools.partial(_bias_rows_kernel, q_start=q_start, t_q=t_q),
        grid=(n_heads,),
        in_specs=[pl.BlockSpec(memory_space=pltpu.SMEM)],
        out_specs=pl.BlockSpec((1, t_q, n_keys), lambda h: (h, 0, 0)),
        out_shape=jax.ShapeDtypeStruct((n_heads, t_q, n_keys), f32),
        name="bias_rows",
        compiler_params=_cparams(("arbitrary",)),
    )(rel_bias)


def _ada_kernel(c_ref, w_ref, b_ref, o_ref):
    c = _silu(c_ref[...]).astype(bf16)
    o_ref[0] = _dot(c, w_ref[0].astype(bf16)) + b_ref[0]


def ada_modulation(c_all, w_ada, b_ada, tn=1536):
    depth, d, n = w_ada.shape
    rows = c_all.shape[0]
    return pl.pallas_call(
        _ada_kernel,
        grid=(depth, n // tn),
        in_specs=[pl.BlockSpec((rows, d), lambda l, j: (0, 0)),
                  pl.BlockSpec((1, d, tn), lambda l, j: (l, 0, j)),
                  pl.BlockSpec((1, 1, tn), lambda l, j: (l, 0, j))],
        out_specs=pl.BlockSpec((1, rows, tn), lambda l, j: (l, 0, j)),
        out_shape=jax.ShapeDtypeStruct((depth, rows, n), f32),
        name="ada_modulation",
        compiler_params=_cparams(("arbitrary", "arbitrary")),
    )(c_all, w_ada, b_ada.reshape(depth, 1, n))


def _inproj_kernel(x_ref, mod_ref, w_ref, *o_refs, d, widths):
    bt, tt, _ = x_ref.shape
    sh = mod_ref[:, :, 0:d]
    sc = mod_ref[:, :, d:2 * d]
    h = (x_ref[...] * (1.0 + sc) + sh).reshape(bt * tt, d).astype(bf16)
    off = 0
    for o_ref, wd in zip(o_refs, widths):
        o_ref[...] = _dot(h, w_ref[:, off:off + wd]).reshape(bt, tt, wd)
        off += wd


def in_projection(x, mod, w_in_bf16, widths, bt, tt):
    b, t, d = x.shape
    n = w_in_bf16.shape[1]
    assert sum(widths) == n and b % bt == 0 and t % tt == 0
    return pl.pallas_call(
        functools.partial(_inproj_kernel, d=d, widths=widths),
        grid=(b // bt, t // tt),
        in_specs=[pl.BlockSpec((bt, tt, d), lambda i, j: (i, j, 0)),
                  pl.BlockSpec((bt, 1, mod.shape[2]), lambda i, j: (i, 0, 0)),
                  pl.BlockSpec((d, n), lambda i, j: (0, 0))],
        out_specs=[pl.BlockSpec((bt, tt, wd), lambda i, j: (i, j, 0)) for wd in widths],
        out_shape=[jax.ShapeDtypeStruct((b, t, wd), f32) for wd in widths],
        name="in_projection",
        compiler_params=_cparams(("arbitrary", "arbitrary")),
    )(x, mod, w_in_bf16)


def _seg_mean_matrix(width, seg):
    r = lax.broadcasted_iota(jnp.int32, (width, width), 0) // seg
    c = lax.broadcasted_iota(jnp.int32, (width, width), 1) // seg
    return jnp.where(r == c, 1.0 / seg, 0.0).astype(bf16)


def _seg_sum_matrix(width, seg):
    r = lax.broadcasted_iota(jnp.int32, (width, width), 0) // seg
    c = lax.broadcasted_iota(jnp.int32, (width, width), 1) // seg
    return jnp.where(r == c, 1.0, 0.0).astype(bf16)


def _seg_reduce(x, m):
    hi = x.astype(bf16)
    lo = (x - hi.astype(f32)).astype(bf16)
    return _dot(hi, m) + _dot(lo, m)


def _sgu_kernel(x_ref, w_ref, bias_ref, g_ref, b_ref, *o_refs, n_chunks, emit_v):
    wd = x_ref.shape[2] // 2
    n_groups = wd // HEAD_DIM
    x = x_ref[0]
    u = _gelu(x[:, :wd])
    v = _gelu(x[:, wd:])
    m = _seg_mean_matrix(wd, HEAD_DIM)
    mu = _seg_reduce(v, m)
    c = v - mu
    var = _seg_reduce(c * c, m)
    vn = c * lax.rsqrt(var + LN_EPS) * g_ref[...] + b_ref[...]
    if emit_v:
        o_refs[1][0] = vn
    row = lax.broadcasted_iota(jnp.int32, (SGU_CHUNK, SGU_CHUNK), 0)
    col = lax.broadcasted_iota(jnp.int32, (SGU_CHUNK, SGU_CHUNK), 1)
    lane_grp = lax.broadcasted_iota(jnp.int32, (SGU_CHUNK, wd), 1) // HEAD_DIM
    w_tril = [jnp.where(col <= row, w_ref[g], 0.0).astype(bf16) for g in range(n_groups)]
    for ci in range(n_chunks):
        rows = slice(ci * SGU_CHUNK, (ci + 1) * SGU_CHUNK)
        vc = vn[rows]
        s = bias_ref[...]
        for g in range(n_groups):
            s = s + _dot(w_tril[g], jnp.where(lane_grp == g, vc, 0.0).astype(bf16))
        o_refs[0][0, rows, :] = u[rows] * s


def sgu(puv, w_s, bias_tile, ln_g, ln_b, rows_per_step, emit_v):
    g_, r_, w2 = puv.shape
    wd = w2 // 2
    assert r_ % rows_per_step == 0 and rows_per_step % SGU_CHUNK == 0
    n_out = 2 if emit_v else 1
    outs = pl.pallas_call(
        functools.partial(_sgu_kernel, n_chunks=rows_per_step // SGU_CHUNK, emit_v=emit_v),
        grid=(g_, r_ // rows_per_step),
        in_specs=[pl.BlockSpec((1, rows_per_step, w2), lambda i, j: (i, j, 0)),
                  pl.BlockSpec(w_s.shape, lambda i, j: (0, 0, 0)),
                  pl.BlockSpec(bias_tile.shape, lambda i, j: (0, 0)),
                  pl.BlockSpec((1, wd), lambda i, j: (0, 0)),
                  pl.BlockSpec((1, wd), lambda i, j: (0, 0))],
        out_specs=[pl.BlockSpec((1, rows_per_step, wd), lambda i, j: (i, j, 0))] * n_out,
        out_shape=[jax.ShapeDtypeStruct((g_, r_, wd), f32)] * n_out,
        name="sgu",
        compiler_params=_cparams(("arbitrary", "arbitrary")),
    )(puv, w_s, bias_tile, ln_g.reshape(1, wd), ln_b.reshape(1, wd))
    return outs


def _topk_rows(scores, n_valid):
    row = lax.broadcasted_iota(jnp.int32, scores.shape, 0)
    valid = row < n_valid
    sel = jnp.zeros(scores.shape, f32)
    for j in range(n_valid):
        sj = scores[j:j + 1, :]
        beats = jnp.logical_and(valid, jnp.logical_or(scores > sj, jnp.logical_and(scores == sj, row < j)))
        cnt = jnp.sum(beats.astype(f32), axis=0, keepdims=True)
        sel = jnp.where(jnp.logical_and(row == j, cnt < MOBA_TOPK), 1.0, sel)
    return sel


def _moba_full_kernel(q_ref, k_ref, v_ref, bias_ref, o_ref, k16_ref, vt16_ref):
    t_len = q_ref.shape[1]
    nb = t_len // MOBA_BLOCK
    blk = MOBA_BLOCK
    scale = HEAD_DIM ** -0.5
    k2 = k_ref[0]
    k16_ref[...] = k2.astype(bf16)
    vt16_ref[...] = v_ref[0].T.astype(bf16)
    kmean16 = jnp.mean(k2.reshape(nb, blk, LANES), axis=1).astype(bf16)
    lane = lax.broadcasted_iota(jnp.int32, (1, LANES), 1)
    hmask = [(lane // HEAD_DIM == hh).astype(f32) for hh in range(2)]
    krow = lax.broadcasted_iota(jnp.int32, (blk, blk), 0)
    qcol = lax.broadcasted_iota(jnp.int32, (blk, blk), 1)
    causal = krow <= qcol
    drow = lax.broadcasted_iota(jnp.int32, (LANES, blk), 0)

    for qb in range(nb):
        n_keys = (qb + 1) * blk
        q2 = q_ref[0, qb * blk:(qb + 1) * blk, :]
        k_all = k16_ref[0:n_keys, :]
        vt_all = vt16_ref[:, 0:n_keys]
        outs = []
        for hh in range(2):
            qm = q2 * hmask[hh]
            q16 = (qm * scale).astype(bf16)
            ok = causal
            if qb > 0:
                sel_t = _topk_rows(_dot_nt(kmean16, qm.astype(bf16)), qb)
                past_ok = jnp.broadcast_to(sel_t[0:qb][:, None, :] > 0.5, (qb, blk, blk))
                ok = jnp.concatenate([past_ok.reshape(qb * blk, blk), causal], axis=0)
            bias = bias_ref[hh, (nb - 1 - qb) * blk:nb * blk, :]
            logits = jnp.where(ok, _dot_nt(k_all, q16) + bias, NEG_BIG)
            m = jnp.max(logits, axis=0, keepdims=True)
            p = jnp.exp(logits - m)
            denom = jnp.sum(p, axis=0, keepdims=True)
            outs.append(_dot(vt_all, p.astype(bf16)) / denom)
        o_ref[0, qb * blk:(qb + 1) * blk, :] = jnp.where(drow < HEAD_DIM, outs[0], outs[1]).T


def moba_full(q, k, v, bias_t):
    b, t, hw = q.shape
    n_pairs = hw // LANES
    qkv_spec = pl.BlockSpec((1, t, LANES), lambda p, i: (i, 0, p))
    return pl.pallas_call(
        _moba_full_kernel,
        grid=(n_pairs, b),
        in_specs=[qkv_spec, qkv_spec, qkv_spec,
                  pl.BlockSpec((2, t, MOBA_BLOCK), lambda p, i: (p, 0, 0))],
        out_specs=pl.BlockSpec((1, t, LANES), lambda p, i: (i, 0, p)),
        out_shape=jax.ShapeDtypeStruct((b, t, hw), f32),
        scratch_shapes=[pltpu.VMEM((t, LANES), bf16), pltpu.VMEM((LANES, t), bf16)],
        name="moba_full",
        compiler_params=_cparams(("arbitrary", "arbitrary")),
    )(q, k, v, bias_t)


PAGES_PER_STEP = 8


def _pair_queries(q2):
    lane = lax.broadcasted_iota(jnp.int32, (1, LANES), 1)
    return jnp.concatenate([q2 * (lane // HEAD_DIM == hh).astype(f32) for hh in range(2)], axis=0)


def _page_rows(page_ref):
    page, n_heads, hd = page_ref.shape[2:]
    return page_ref[0, 0].reshape(page, n_heads * hd)


def _moba_paged_probs_kernel(pt_ref, q_ref, kn_ref, bias_ref, bown_ref, *refs, n_pairs):
    kp_refs = refs[:PAGES_PER_STEP]
    p_ref, pown_ref, logit_ref, bsum_ref = refs[PAGES_PER_STEP:]
    c = pl.program_id(1)
    n_steps = pl.num_programs(1)
    t_q = q_ref.shape[1]
    page = kp_refs[0].shape[2]
    pages_per_block = MOBA_BLOCK // page
    n_blocks = bsum_ref.shape[0]
    step_keys = PAGES_PER_STEP * page
    scale = HEAD_DIM ** -0.5

    @pl.when(c == 0)
    def _():
        bsum_ref[...] = jnp.zeros_like(bsum_ref)

    qf = [_pair_queries(q_ref[0, :, p * LANES:(p + 1) * LANES]) for p in range(n_pairs)]
    q16 = [(x * scale).astype(bf16) for x in qf]
    for j in range(PAGES_PER_STEP):
        kpage = _page_rows(kp_refs[j])
        blk_row = (c * PAGES_PER_STEP + j) // pages_per_block
        bsum_ref[pl.ds(blk_row, 1), :] += jnp.sum(kpage, axis=0, keepdims=True)
        for p in range(n_pairs):
            logit_ref[p, c, :, j * page:(j + 1) * page] = _dot_nt(
                q16[p], kpage[:, p * LANES:(p + 1) * LANES].astype(bf16))

    @pl.when(c == n_steps - 1)
    def _():
        n_keys = n_steps * step_keys
        blk_of_key = lax.broadcasted_iota(jnp.int32, (n_blocks, n_keys), 1) // MOBA_BLOCK
        expand = (blk_of_key == lax.broadcasted_iota(jnp.int32, (n_blocks, n_keys), 0)).astype(bf16)
        rq = lax.broadcasted_iota(jnp.int32, (2 * t_q, LANES), 0) % t_q
        ck = lax.broadcasted_iota(jnp.int32, (2 * t_q, LANES), 1)
        own_ok = ck <= rq
        zpad = jnp.zeros((LANES - t_q, LANES), f32)
        for p in range(n_pairs):
            lanes = slice(p * LANES, (p + 1) * LANES)
            kmean16 = (bsum_ref[:, lanes] * (1.0 / MOBA_BLOCK)).astype(bf16)
            sel_t = _topk_rows(_dot_nt(kmean16, qf[p].astype(bf16)), n_blocks)
            chosen = _dot(sel_t.T.astype(bf16), expand)
            lg = [jnp.where(chosen[:, s * step_keys:(s + 1) * step_keys] > 0.5,
                            logit_ref[p, s] + bias_ref[p, s], NEG_BIG) for s in range(n_steps)]
            kn = jnp.concatenate([kn_ref[0, :, lanes], zpad], axis=0).astype(bf16)
            own = jnp.where(own_ok, _dot_nt(q16[p], kn) + bown_ref[p], NEG_BIG)
            m = jnp.max(own, axis=1, keepdims=True)
            for x in lg:
                m = jnp.maximum(m, jnp.max(x, axis=1, keepdims=True))
            e_own = jnp.exp(own - m)
            e = [jnp.exp(x - m) for x in lg]
            denom = jnp.sum(e_own, axis=1, keepdims=True)
            for x in e:
                denom = denom + jnp.sum(x, axis=1, keepdims=True)
            inv = 1.0 / denom
            for s in range(n_steps):
                p_ref[0, p, s] = e[s] * inv
            pown_ref[0, p] = e_own * inv


def _moba_paged_mix_kernel(pt_ref, p_ref, pown_ref, vn_ref, *refs, n_pairs):
    vp_refs = refs[:PAGES_PER_STEP]
    o_ref, acc_ref = refs[PAGES_PER_STEP:]
    c = pl.program_id(1)
    n_steps = pl.num_programs(1)
    t_q = vn_ref.shape[1]
    page = vp_refs[0].shape[2]

    @pl.when(c == 0)
    def _():
        acc_ref[...] = jnp.zeros_like(acc_ref)

    for j in range(PAGES_PER_STEP):
        vpage = _page_rows(vp_refs[j])
        for p in range(n_pairs):
            acc_ref[p] += _dot(p_ref[0, p, 0, :, j * page:(j + 1) * page].astype(bf16),
                               vpage[:, p * LANES:(p + 1) * LANES].astype(bf16))

    @pl.when(c == n_steps - 1)
    def _():
        lane = lax.broadcasted_iota(jnp.int32, (1, LANES), 1)
        zpad = jnp.zeros((LANES - t_q, LANES), f32)
        for p in range(n_pairs):
            lanes = slice(p * LANES, (p + 1) * LANES)
            vn = jnp.concatenate([vn_ref[0, :, lanes], zpad], axis=0).astype(bf16)
            acc = acc_ref[p] + _dot(pown_ref[0, p].astype(bf16), vn)
            o_ref[0, :, lanes] = jnp.where(lane < HEAD_DIM, acc[0:t_q], acc[t_q:2 * t_q])


def moba_paged(q, k_new, v_new, cache_k, cache_v, layer, page_table, bias_past, bias_own):
    b, t_q, hw = q.shape
    n_pairs = hw // LANES
    n_pages = page_table.shape[1]
    page, n_heads, hd = cache_k.shape[2:]
    assert page == LANES and n_pages % PAGES_PER_STEP == 0 and MOBA_BLOCK % page == 0 and n_heads * hd == hw
    n_steps = n_pages // PAGES_PER_STEP
    step_keys = PAGES_PER_STEP * page
    rows = 2 * t_q

    def page_spec(j):
        return pl.BlockSpec((1, 1, page, n_heads, hd),
                            lambda i, c, pt: (layer, pt[i, c * PAGES_PER_STEP + j], 0, 0, 0))

    new_spec = pl.BlockSpec((1, t_q, hw), lambda i, c, pt: (i, 0, 0))
    own_spec = pl.BlockSpec((1, n_pairs, rows, LANES), lambda i, c, pt: (i, 0, 0, 0))
    probs, p_own = pl.pallas_call(
        functools.partial(_moba_paged_probs_kernel, n_pairs=n_pairs),
        grid_spec=pltpu.PrefetchScalarGridSpec(
            num_scalar_prefetch=1,
            grid=(b, n_steps),
            in_specs=[new_spec, new_spec,
                      pl.BlockSpec(bias_past.shape, lambda i, c, pt: (0, 0, 0, 0)),
                      pl.BlockSpec(bias_own.shape, lambda i, c, pt: (0, 0, 0))]
                     + [page_spec(j) for j in range(PAGES_PER_STEP)],
            out_specs=[pl.BlockSpec((1, n_pairs, n_steps, rows, step_keys), lambda i, c, pt: (i, 0, 0, 0, 0)),
                       own_spec],
            scratch_shapes=[pltpu.VMEM((n_pairs, n_steps, rows, step_keys), f32),
                            pltpu.VMEM((n_pages * page // MOBA_BLOCK, hw), f32)]),
        out_shape=[jax.ShapeDtypeStruct((b, n_pairs, n_steps, rows, step_keys), f32),
                   jax.ShapeDtypeStruct((b, n_pairs, rows, LANES), f32)],
        name="moba_paged_probs",
        compiler_params=_cparams(("arbitrary", "arbitrary")),
    )(page_table, q, k_new, bias_past, bias_own, *([cache_k] * PAGES_PER_STEP))
    return pl.pallas_call(
        functools.partial(_moba_paged_mix_kernel, n_pairs=n_pairs),
        grid_spec=pltpu.PrefetchScalarGridSpec(
            num_scalar_prefetch=1,
            grid=(b, n_steps),
            in_specs=[pl.BlockSpec((1, n_pairs, 1, rows, step_keys), lambda i, c, pt: (i, 0, c, 0, 0)),
                      own_spec, new_spec]
                     + [page_spec(j) for j in range(PAGES_PER_STEP)],
            out_specs=new_spec,
            scratch_shapes=[pltpu.VMEM((n_pairs, rows, LANES), f32)]),
        out_shape=jax.ShapeDtypeStruct((b, t_q, hw), f32),
        name="moba_paged_mix",
        compiler_params=_cparams(("arbitrary", "arbitrary")),
    )(page_table, probs, p_own, v_new, *([cache_v] * PAGES_PER_STEP))


def paged_bias_layout(bias_rows_arr, n_past):
    h, t_q, _ = bias_rows_arr.shape
    step_keys = PAGES_PER_STEP * LANES
    n_steps = n_past // step_keys
    past = bias_rows_arr[:, :, :n_past].reshape(h // 2, 2 * t_q, n_steps, step_keys)
    own = bias_rows_arr[:, :, n_past:].reshape(h // 2, 2 * t_q, LANES)
    return past.transpose(0, 2, 1, 3), own


def _rwkv_prep_kernel(p_ref, prev_ref, shift_ref, mu_ref, w0_ref, a0_ref, kk_ref, ka_ref, rk_ref,
                      ww_ref, wa_ref, wg_ref, *o_refs, hw):
    bt, tt, n = p_ref.shape
    j = pl.program_id(1)
    p = p_ref[...].reshape(bt * tt, n)
    first = jnp.where(j == 0, shift_ref[...], prev_ref[:, 7:8, :])
    first = jnp.broadcast_to(first, (bt, tt, n)).reshape(bt * tt, n)
    row = lax.broadcasted_iota(jnp.int32, (bt * tt, 1), 0)
    prev = jnp.where(row % tt == 0, first, pltpu.roll(p, 1, 0))
    xs = p + (prev - p) * mu_ref[...]
    r = xs[:, 0:hw]
    k = xs[:, hw:2 * hw]
    v = xs[:, 2 * hw:3 * hw]
    xwa = xs[:, 3 * hw:3 * hw + LANES]
    xg = xs[:, 3 * hw + LANES:3 * hw + 2 * LANES]
    z = -(w0_ref[...] + _dot_f32(jnp.tanh(xwa), ww_ref[...]))
    softplus = jnp.maximum(z, 0.0) + jnp.log(1.0 + jnp.exp(-jnp.abs(z)))
    log_decay = -jnp.exp(-softplus - 0.5)
    a = _sigmoid(a0_ref[...] + _dot_f32(xwa, wa_ref[...]))
    g = _dot_f32(_sigmoid(xg), wg_ref[...])
    mseg = _seg_sum_matrix(hw, HEAD_DIM)
    kk = k * kk_ref[...]
    kk = kk / jnp.maximum(jnp.sqrt(_seg_reduce(kk * kk, mseg)), 1e-12)
    k2 = k * (1.0 + (a - 1.0) * ka_ref[...])
    bonus = _seg_reduce(r * k2 * rk_ref[...], mseg) * v
    outs = (r, log_decay, k2, v, -kk, kk * a, g, bonus)
    for o_ref, val in zip(o_refs, outs):
        o_ref[...] = val.reshape(bt, tt, hw)


def rwkv_prep(prw, shift0, prm, bt, tt):
    b, t, n = prw.shape
    hw = prm['w0'].shape[-1]
    assert n == 3 * hw + 2 * LANES and tt % 8 == 0
    row = lambda x: x.reshape(1, -1)
    vec = lambda: pl.BlockSpec((1, hw), lambda i, j: (0, 0))
    mat = lambda: pl.BlockSpec((LANES, hw), lambda i, j: (0, 0))
    prev_blk = tt // 8
    return pl.pallas_call(
        functools.partial(_rwkv_prep_kernel, hw=hw),
        grid=(b // bt, t // tt),
        in_specs=[pl.BlockSpec((bt, tt, n), lambda i, j: (i, j, 0)),
                  pl.BlockSpec((bt, 8, n), lambda i, j: (i, jnp.maximum(j * prev_blk - 1, 0), 0)),
                  pl.BlockSpec((bt, 1, n), lambda i, j: (i, 0, 0)),
                  pl.BlockSpec((1, n), lambda i, j: (0, 0)),
                  vec(), vec(), vec(), vec(), vec(), mat(), mat(), mat()],
        out_specs=[pl.BlockSpec((bt, tt, hw), lambda i, j: (i, j, 0))] * 8,
        out_shape=[jax.ShapeDtypeStruct((b, t, hw), f32)] * 8,
        name="rwkv_prep",
        compiler_params=_cparams(("arbitrary", "arbitrary")),
    )(prw, prw, shift0.reshape(b, 1, n), row(prm['mu']), row(prm['w0']), row(prm['a0']),
      row(prm['k_k']), row(prm['k_a']), row(prm['r_k']), prm['w_w2p'], prm['w_a2p'], prm['w_g2'])


def rwkv_prep_params(mu, w0, w_w2, a0, w_a2, w_g2, k_k, k_a, r_k):
    hw = w0.shape[-1]
    zeros = jnp.zeros((LANES - DECAY_LORA, hw), w_w2.dtype)
    return dict(mu=mu, w0=w0, a0=a0, k_k=k_k, k_a=k_a, r_k=r_k.reshape(-1),
                w_w2p=jnp.concatenate([w_w2, zeros], axis=0),
                w_a2p=jnp.concatenate([zeros, w_a2], axis=0), w_g2=w_g2)


def _split3(x):
    hi = x.astype(bf16)
    return hi, (x - hi.astype(f32)).astype(bf16)


def _dot3(a, b):
    ah, al = _split3(a)
    bh, bl = _split3(b)
    return _dot(ah, bh) + (_dot(ah, bl) + _dot(al, bh))


def _dot3_nt(a, b):
    ah, al = _split3(a)
    bh, bl = _split3(b)
    return _dot_nt(ah, bh) + (_dot_nt(ah, bl) + _dot_nt(al, bh))


def _dot3_tn(a, b):
    return _dot3(a.T, b)


def _rwkv_chunk_maps_kernel(r_ref, lw_ref, k_ref, v_ref, x_ref, b_ref, m_ref, n_ref, p_ref, q_ref,
                            *, n_pairs, c_len):
    two_c = 2 * c_len
    rr = lax.broadcasted_iota(jnp.int32, (two_c, two_c), 0)
    cc = lax.broadcasted_iota(jnp.int32, (two_c, two_c), 1)
    same_head = (rr // c_len) == (cc // c_len)
    strict = jnp.logical_and(same_head, (cc % c_len) < (rr % c_len))
    incl = jnp.logical_and(same_head, (cc % c_len) <= (rr % c_len))
    eye = (rr == cc).astype(f32)
    row_head = lax.broadcasted_iota(jnp.int32, (two_c, LANES), 0) // c_len
    lane_head = lax.broadcasted_iota(jnp.int32, (two_c, LANES), 1) // HEAD_DIM
    bd = row_head == lane_head
    eye_k = (lax.broadcasted_iota(jnp.int32, (LANES, LANES), 0)
             == lax.broadcasted_iota(jnp.int32, (LANES, LANES), 1)).astype(f32)
    trow = lax.broadcasted_iota(jnp.int32, (c_len, LANES), 0)
    n_doubling = max(1, (c_len - 1).bit_length())

    def stack(a):
        return jnp.where(bd, jnp.concatenate([a, a], axis=0), 0.0)

    jobs = [(ci, p) for ci in range(r_ref.shape[1] // c_len) for p in range(n_pairs)]
    ops = []
    for ci, p in jobs:
        rows = slice(ci * c_len, (ci + 1) * c_len)
        lanes = slice(p * LANES, (p + 1) * LANES)
        lw = lw_ref[0, rows, lanes]
        cum = lw
        sh = 1
        while sh < c_len:
            cum = cum + jnp.where(trow >= sh, pltpu.roll(cum, sh, 0), 0.0)
            sh *= 2
        gam = jnp.exp(cum)
        inv_gam = jnp.exp(-cum)
        ops.append(dict(
            a=stack(x_ref[0, rows, lanes] * jnp.exp(cum - lw)), r=stack(r_ref[0, rows, lanes] * gam),
            b=stack(b_ref[0, rows, lanes] * inv_gam), k=stack(k_ref[0, rows, lanes] * inv_gam),
            v=stack(v_ref[0, rows, lanes]), g_end=gam[c_len - 1:c_len, :]))
    for o in ops:
        gram = _dot3_nt(jnp.concatenate([o['a'], o['r']], axis=0), jnp.concatenate([o['b'], o['k']], axis=0))
        o['l_ab'] = jnp.where(strict, gram[0:two_c, 0:two_c], 0.0)
        o['l_ak'] = jnp.where(strict, gram[0:two_c, two_c:2 * two_c], 0.0)
        o['m_rb'] = jnp.where(incl, gram[two_c:2 * two_c, 0:two_c], 0.0)
        o['m_rk'] = jnp.where(incl, gram[two_c:2 * two_c, two_c:2 * two_c], 0.0)
        o['pw'] = o['l_ab']
        o['u'] = o['l_ab']
    for it in range(n_doubling - 1):
        mm = _dot3 if it == 0 else _dot_f32
        for o in ops:
            o['pw'] = mm(o['pw'], o['pw'])
        for o in ops:
            o['u'] = o['u'] + o['pw'] + mm(o['u'], o['pw'])
    for o in ops:
        o['lkv'] = _dot3(o['l_ak'], o['v'])
    for o in ops:
        o['pq1'] = _dot3(eye + o['u'], jnp.concatenate([o['a'], o['lkv']], axis=1))
    for o in ops:
        o['pq2'] = _dot3(o['m_rb'], o['pq1'])
        o['mkv'] = _dot3(o['m_rk'], o['v'])
    for o in ops:
        o['mn'] = _dot3_tn(o['pq1'], o['b'] * o['g_end'])
        o['vk'] = _dot3_tn(o['v'], o['k'] * o['g_end'])
    for (ci, p), o in zip(jobs, ops):
        p_ref[0, ci, p] = o['r'] + o['pq2'][:, 0:LANES]
        q_ref[0, ci, p] = o['pq2'][:, LANES:2 * LANES] + o['mkv']
        m_ref[0, ci, p] = eye_k * o['g_end'] + o['mn'][0:LANES]
        n_ref[0, ci, p] = o['mn'][LANES:2 * LANES] + o['vk']


def _rwkv_state_kernel(m_ref, n_ref, p_ref, q_ref, g_ref, bonus_ref, s0_ref, lng_ref, lnb_ref,
                       y_ref, sout_ref, state_ref, *, n_pairs, c_len):
    jc = pl.program_id(1)
    n_steps = pl.num_programs(1)

    @pl.when(jc == 0)
    def _():
        state_ref[...] = s0_ref[0]

    mseg = _seg_mean_matrix(LANES, HEAD_DIM)
    states = [state_ref[p] for p in range(n_pairs)]
    for ci in range(m_ref.shape[1]):
        rows = slice(ci * c_len, (ci + 1) * c_len)
        outs = [_dot3_nt(p_ref[0, ci, p], states[p]) + q_ref[0, ci, p] for p in range(n_pairs)]
        states = [_dot3(states[p], m_ref[0, ci, p]) + n_ref[0, ci, p] for p in range(n_pairs)]
        for p in range(n_pairs):
            lanes = slice(p * LANES, (p + 1) * LANES)
            o = outs[p][0:c_len] + outs[p][c_len:2 * c_len]
            mu = _seg_reduce(o, mseg)
            cen = o - mu
            var = _seg_reduce(cen * cen, mseg)
            y = cen * lax.rsqrt(var + GN_EPS) * lng_ref[:, lanes] + lnb_ref[:, lanes]
            y_ref[0, rows, lanes] = (y + bonus_ref[0, rows, lanes]) * g_ref[0, rows, lanes]
    for p in range(n_pairs):
        state_ref[p] = states[p]

    @pl.when(jc == n_steps - 1)
    def _():
        sout_ref[0] = state_ref[...]


def rwkv_chunked(seq, s0_bd, lnx_g, lnx_b, c_len, maps_chunks, scan_chunks):
    r, lw, k2, v, x, kka, g, bonus = seq
    b, t, hw = r.shape
    n_pairs = hw // LANES
    n_chunks = t // c_len
    assert t % c_len == 0 and n_chunks % maps_chunks == 0 and n_chunks % scan_chunks == 0
    two_c = 2 * c_len

    def map_spec(rows, per_step):
        return pl.BlockSpec((1, per_step, n_pairs, rows, LANES), lambda i, j: (i, j, 0, 0, 0))

    def map_shape(rows):
        return jax.ShapeDtypeStruct((b, n_chunks, n_pairs, rows, LANES), f32)

    seq_spec = lambda per_step: pl.BlockSpec((1, per_step * c_len, hw), lambda i, j: (i, j, 0))
    maps = pl.pallas_call(
        functools.partial(_rwkv_chunk_maps_kernel, n_pairs=n_pairs, c_len=c_len),
        grid=(b, n_chunks // maps_chunks),
        in_specs=[seq_spec(maps_chunks)] * 6,
        out_specs=[map_spec(LANES, maps_chunks), map_spec(LANES, maps_chunks),
                   map_spec(two_c, maps_chunks), map_spec(two_c, maps_chunks)],
        out_shape=[map_shape(LANES), map_shape(LANES), map_shape(two_c), map_shape(two_c)],
        name="rwkv_chunk_maps",
        compiler_params=_cparams(("arbitrary", "arbitrary")),
    )(r, lw, k2, v, x, kka)
    st_spec = pl.BlockSpec((1, n_pairs, LANES, LANES), lambda i, j: (i, 0, 0, 0))
    vec = pl.BlockSpec((1, hw), lambda i, j: (0, 0))
    return pl.pallas_call(
        functools.partial(_rwkv_state_kernel, n_pairs=n_pairs, c_len=c_len),
        grid=(b, n_chunks // scan_chunks),
        in_specs=[map_spec(LANES, scan_chunks), map_spec(LANES, scan_chunks),
                  map_spec(two_c, scan_chunks), map_spec(two_c, scan_chunks),
                  seq_spec(scan_chunks), seq_spec(scan_chunks), st_spec, vec, vec],
        out_specs=[seq_spec(scan_chunks), st_spec],
        out_shape=[jax.ShapeDtypeStruct((b, t, hw), f32),
                   jax.ShapeDtypeStruct((b, n_pairs, LANES, LANES), f32)],
        scratch_shapes=[pltpu.VMEM((n_pairs, LANES, LANES), f32)],
        name="rwkv_state_scan",
        compiler_params=_cparams(("arbitrary", "arbitrary")),
    )(*maps, g, bonus, s0_bd, lnx_g.reshape(1, hw), lnx_b.reshape(1, hw))


def state_to_blockdiag(s):
    b, h, n, _ = s.shape
    sp = s.reshape(b, h // 2, 2, n, n)
    z = jnp.zeros_like(sp[:, :, 0])
    top = jnp.concatenate([sp[:, :, 0], z], axis=-1)
    bot = jnp.concatenate([z, sp[:, :, 1]], axis=-1)
    return jnp.concatenate([top, bot], axis=-2)


def state_from_blockdiag(sbd):
    b, hp, _, _ = sbd.shape
    n = HEAD_DIM
    return jnp.stack([sbd[:, :, :n, :n], sbd[:, :, n:, n:]], axis=2).reshape(b, hp * 2, n, n)


def _layer_norm(z, g, b):
    mu = jnp.mean(z, axis=-1, keepdims=True)
    c = z - mu
    var = jnp.mean(c * c, axis=-1, keepdims=True)
    return c * lax.rsqrt(var + LN_EPS) * g + b


def _route(sel, aff, n_groups, per_group, top_k):
    rows = sel.shape[1]
    srow = [sel[e:e + 1, :] for e in range(n_groups * per_group)]
    arow = [aff[e:e + 1, :] for e in range(n_groups * per_group)]
    scores = []
    for g in range(n_groups):
        s = srow[g * per_group:(g + 1) * per_group]
        best = None
        for i in range(per_group):
            for j in range(i + 1, per_group):
                pair = s[i] + s[j]
                best = pair if best is None else jnp.maximum(best, pair)
        scores.append(best)
    top = scores[0]
    for g in range(1, n_groups):
        top = jnp.maximum(top, scores[g])
    taken = jnp.zeros((1, rows), jnp.bool_)
    eidx = lax.broadcasted_iota(jnp.int32, (n_groups * per_group, rows), 0)
    gates = jnp.zeros((n_groups * per_group, rows), f32)
    for g in range(n_groups):
        is_grp = jnp.logical_and(scores[g] == top, jnp.logical_not(taken))
        taken = jnp.logical_or(taken, is_grp)
        s = srow[g * per_group:(g + 1) * per_group]
        a = arow[g * per_group:(g + 1) * per_group]
        chosen = []
        for i in range(per_group):
            cnt = jnp.zeros((1, rows), f32)
            for j in range(per_group):
                if j != i:
                    ahead = (s[j] > s[i]) if j > i else (s[j] >= s[i])
                    cnt = cnt + ahead.astype(f32)
            chosen.append(jnp.logical_and(cnt < top_k, is_grp))
        denom = jnp.zeros((1, rows), f32)
        for i in range(per_group):
            denom = denom + jnp.where(chosen[i], a[i], 0.0)
        for i in range(per_group):
            gate = jnp.where(chosen[i], a[i] / denom, 0.0)
            gates = jnp.where(eidx == g * per_group + i, gate, gates)
    return gates


def _outproj_kernel(x_ref, ya_ref, yb_ref, yc_ref, mod_ref, w_ref, g_ref, b_ref, wr_ref, rb_ref,
                    x1_ref, h2_ref, gates_ref, *, d, alpha):
    bt, tt, _ = x_ref.shape
    rows = bt * tt
    off = 0
    mix = jnp.zeros((rows, d), f32)
    for y_ref in (ya_ref, yb_ref, yc_ref):
        wd = y_ref.shape[2]
        mix = mix + _dot(y_ref[...].reshape(rows, wd).astype(bf16), w_ref[off:off + wd, :])
        off += wd
    g1 = mod_ref[:, :, 2 * d:3 * d]
    z = alpha * x_ref[...] + (1.0 + g1) * mix.reshape(bt, tt, d)
    x1 = _layer_norm(z, g_ref[...], b_ref[...])
    x1_ref[...] = x1
    sh2 = mod_ref[:, :, 3 * d:4 * d]
    sc2 = mod_ref[:, :, 4 * d:5 * d]
    h2 = (x1 * (1.0 + sc2) + sh2).reshape(rows, d)
    h2_ref[...] = h2.astype(bf16).reshape(bt, tt, d)
    aff = _sigmoid(_dot_nt_f32(wr_ref[...], h2))
    gates_ref[...] = _route(aff + rb_ref[...], aff, N_EXPERT_GROUPS,
                            wr_ref.shape[0] // N_EXPERT_GROUPS, TOP_K_EXPERTS)


def out_projection(x, ya, yb, yc, mod, w_out_bf16, ln_g, ln_b, w_router_t, router_bias, alpha, bt, tt):
    b, t, d = x.shape
    n_exp = w_router_t.shape[0]
    rows = bt * tt
    blk = lambda wd: pl.BlockSpec((bt, tt, wd), lambda i, j: (i, j, 0))
    full = lambda a: pl.BlockSpec(a.shape, lambda i, j: (0,) * a.ndim)
    n_tblk = t // tt
    ln_g = ln_g.reshape(1, d)
    ln_b = ln_b.reshape(1, d)
    rb = router_bias.reshape(n_exp, 1)
    return pl.pallas_call(
        functools.partial(_outproj_kernel, d=d, alpha=alpha),
        grid=(b // bt, n_tblk),
        in_specs=[blk(d), blk(ya.shape[2]), blk(yb.shape[2]), blk(yc.shape[2]),
                  pl.BlockSpec((bt, 1, mod.shape[2]), lambda i, j: (i, 0, 0)),
                  full(w_out_bf16), full(ln_g), full(ln_b), full(w_router_t), full(rb)],
        out_specs=[blk(d), blk(d), pl.BlockSpec((n_exp, rows), lambda i, j: (0, i * n_tblk + j))],
        out_shape=[jax.ShapeDtypeStruct((b, t, d), f32), jax.ShapeDtypeStruct((b, t, d), bf16),
                   jax.ShapeDtypeStruct((n_exp, b * t), f32)],
        name="out_projection",
        compiler_params=_cparams(("arbitrary", "arbitrary")),
    )(x, ya, yb, yc, mod, w_out_bf16, ln_g, ln_b, w_router_t, rb)


def _moe_kernel(h_ref, gates_ref, x1_ref, mod_ref, wg_ref, wu_ref, wd_ref, g_ref, b_ref,
                o_ref, acc_ref, *, d, alpha):
    bt, tt, _ = h_ref.shape
    rows = bt * tt
    e = pl.program_id(2)
    n_exp = pl.num_programs(2)

    @pl.when(e == 0)
    def _():
        acc_ref[...] = jnp.zeros_like(acc_ref)

    h = h_ref[...].reshape(rows, d)
    gates = gates_ref[...].reshape(rows, gates_ref.shape[2])
    lane = lax.broadcasted_iota(jnp.int32, gates.shape, 1)
    gate = jnp.sum(jnp.where(lane == e, gates, 0.0), axis=1, keepdims=True)
    hg = _dot(h, wg_ref[0])
    hu = _dot(h, wu_ref[0])
    act = _silu(hg) * hu * gate
    acc_ref[...] += _dot(act.astype(bf16), wd_ref[0])

    @pl.when(e == n_exp - 1)
    def _():
        g2 = mod_ref[:, :, 5 * d:6 * d]
        z = alpha * x1_ref[...] + (1.0 + g2) * acc_ref[...].reshape(bt, tt, d)
        o_ref[...] = _layer_norm(z, g_ref[...], b_ref[...])


def moe_ffn(h2, gates, x1, mod, wg_bf16, wu_bf16, wd_bf16, ln_g, ln_b, alpha, bt, tt):
    b, t, d = x1.shape
    n_exp = wg_bf16.shape[0]
    f = wg_bf16.shape[2]
    blk = lambda wd: pl.BlockSpec((bt, tt, wd), lambda i, j, e: (i, j, 0))
    return pl.pallas_call(
        functools.partial(_moe_kernel, d=d, alpha=alpha),
        grid=(b // bt, t // tt, n_exp),
        in_specs=[blk(d), blk(n_exp), blk(d),
                  pl.BlockSpec((bt, 1, mod.shape[2]), lambda i, j, e: (i, 0, 0)),
                  pl.BlockSpec((1, d, f), lambda i, j, e: (e, 0, 0)),
                  pl.BlockSpec((1, d, f), lambda i, j, e: (e, 0, 0)),
                  pl.BlockSpec((1, f, d), lambda i, j, e: (e, 0, 0)),
                  pl.BlockSpec((1, d), lambda i, j, e: (0, 0)),
                  pl.BlockSpec((1, d), lambda i, j, e: (0, 0))],
        out_specs=blk(d),
        out_shape=jax.ShapeDtypeStruct((b, t, d), f32),
        scratch_shapes=[pltpu.VMEM((bt * tt, d), f32)],
        name="moe_ffn",
        compiler_params=_cparams(("arbitrary", "arbitrary", "arbitrary")),
    )(h2, gates, x1, mod, wg_bf16, wu_bf16, wd_bf16, ln_g.reshape(1, d), ln_b.reshape(1, d))


def sgu_full_params(w_s, b_s):
    return w_s, jnp.repeat(b_s.T, HEAD_DIM, axis=1)


def sgu_short_params(w_s, b_s, t_len):
    reps = SGU_CHUNK // t_len
    eye = jnp.eye(reps, dtype=w_s.dtype)
    w_small = w_s[:, :t_len, :t_len]
    w_big = jnp.einsum('ab,gts->gatbs', eye, w_small).reshape(w_s.shape[0], SGU_CHUNK, SGU_CHUNK)
    bias = jnp.tile(jnp.repeat(b_s[:, :t_len].T, HEAD_DIM, axis=1), (reps, 1))
    return w_big, bias


PROMPT_ROWS = 512
RWKV_CHUNK = (64, 2, 4)


def _hybrid_layer(x, mod, tile, wts, attend, sgu_prm, shift0, s0_bd, rwkv_chunk, emit_sgu_v, alpha):
    bt, tt = tile
    b, t, d = x.shape
    hw_sgu = sgu_prm[0].shape[0] * HEAD_DIM
    hw_attn = wts['attn_width']
    widths = (2 * hw_sgu, hw_attn, hw_attn, hw_attn, wts['w_in'].shape[1] - 2 * hw_sgu - 3 * hw_attn)
    puv, q, k, v, prw = in_projection(x, mod, wts['w_in'], widths, bt, tt)
    n_rows = b * t
    sgu_rows = min(PROMPT_ROWS, n_rows) if t < SGU_CHUNK else tt
    sgu_in = puv.reshape(1, n_rows, 2 * hw_sgu) if t < SGU_CHUNK else puv
    sgu_out = sgu(sgu_in, sgu_prm[0], sgu_prm[1], wts['sgu_ln_g'], wts['sgu_ln_b'], sgu_rows, emit_sgu_v)
    ya = sgu_out[0].reshape(b, t, hw_sgu)
    sgu_v = sgu_out[1].reshape(b, t, hw_sgu) if emit_sgu_v else None
    yb = attend(q, k, v)
    seq = rwkv_prep(prw, shift0, wts['rwkv'], bt, tt)
    yc, s_bd = rwkv_chunked(seq, s0_bd, wts['lnx_g'], wts['lnx_b'], *rwkv_chunk)
    x1, h2, gates_t = out_projection(x, ya, yb, yc, mod, wts['w_out'], wts['ln1_g'], wts['ln1_b'],
                                     wts['w_router_t'], wts['router_bias'], alpha, bt, tt)
    gates = gates_t.T.reshape(b, t, gates_t.shape[0])
    x2 = moe_ffn(h2, gates, x1, mod, wts['moe_wg'], wts['moe_wu'], wts['moe_wd'],
                 wts['ln2_g'], wts['ln2_b'], alpha, bt, tt)
    return x2, k, v, state_from_blockdiag(s_bd), prw[:, -1], sgu_v


def kernel(x_prompt, x_sample, cache_k, cache_v, state_rwkv, state_shift, page_table, c_prompt, c_sample, w_ada, b_ada, w_in, w_out, sgu_w, sgu_b, sgu_ln_g, sgu_ln_b, rel_bias, rwkv_mu, rwkv_w0, rwkv_w_w2, rwkv_a0, rwkv_w_a2, rwkv_w_g2, rwkv_k_k, rwkv_k_a, rwkv_r_k, rwkv_lnx_g, rwkv_lnx_b, ln1_g, ln1_b, ln2_g, ln2_b, w_router, router_bias, moe_w_gate, moe_w_up, moe_w_down):
    depth = w_ada.shape[0]
    bp, t_p, d = x_prompt.shape
    bs, t_s, _ = x_sample.shape
    n_heads = cache_k.shape[3]
    hw_attn = n_heads * HEAD_DIM
    n_pool, page = cache_k.shape[1], cache_k.shape[2]
    past_len = page_table.shape[1] * page
    rwkv_in = state_shift.shape[2]
    alpha = (2 * depth) ** 0.25

    mods = ada_modulation(jnp.concatenate([c_prompt, c_sample], axis=0), w_ada, b_ada)
    bias_p = bias_tiles(rel_bias, n_heads, t_p // MOBA_BLOCK).reshape(n_heads, t_p, MOBA_BLOCK)
    bias_past, bias_own = paged_bias_layout(
        bias_rows(rel_bias, n_heads, past_len, t_s, past_len + LANES), past_len)

    def attend_paged(layer):
        def attend(q, k, v):
            return moba_paged(q, k, v, cache_k, cache_v, layer, page_table, bias_past, bias_own)
        return attend
    w_in16, w_out16 = w_in.astype(bf16), w_out.astype(bf16)
    wg16, wu16, wd16 = moe_w_gate.astype(bf16), moe_w_up.astype(bf16), moe_w_down.astype(bf16)
    w_router_t = w_router.T
    zero_shift = jnp.zeros((bp, rwkv_in), f32)
    zero_state = jnp.zeros((bp, n_heads // 2, LANES, LANES), f32)

    xp, xs = x_prompt, x_sample
    outs = [[] for _ in range(9)]
    for l in range(depth):
        wts = dict(
            attn_width=hw_attn, w_in=w_in16[l], w_out=w_out16[l], sgu_ln_g=sgu_ln_g[l], sgu_ln_b=sgu_ln_b[l],
            rwkv=rwkv_prep_params(rwkv_mu[l], rwkv_w0[l], rwkv_w_w2[l], rwkv_a0[l], rwkv_w_a2[l],
                                  rwkv_w_g2[l], rwkv_k_k[l], rwkv_k_a[l], rwkv_r_k[l]),
            lnx_g=rwkv_lnx_g[l], lnx_b=rwkv_lnx_b[l], ln1_g=ln1_g[l], ln1_b=ln1_b[l],
            ln2_g=ln2_g[l], ln2_b=ln2_b[l], w_router_t=w_router_t, router_bias=router_bias,
            moe_wg=wg16[l], moe_wu=wu16[l], moe_wd=wd16[l])
        mod_p = mods[l, :bp][:, None, :]
        mod_s = mods[l, bp:][:, None, :]
        xp, kp, vp, sp, shp, _ = _hybrid_layer(
            xp, mod_p, (1, PROMPT_ROWS), wts, lambda q, k, v: moba_full(q, k, v, bias_p),
            sgu_full_params(sgu_w[l], sgu_b[l]), zero_shift, zero_state, RWKV_CHUNK, False, alpha)
        xs, ks_, vs_, ss, shs, sgu_v = _hybrid_layer(
            xs, mod_s, (bs, t_s), wts, attend_paged(l),
            sgu_short_params(sgu_w[l], sgu_b[l], t_s), state_shift[l],
            state_to_blockdiag(state_rwkv[l]), (t_s, 1, 1), True, alpha)
        heads = lambda a: a.reshape(a.shape[0], a.shape[1], n_heads, HEAD_DIM)
        for lst, val in zip(outs, (heads(kp), heads(vp), heads(ks_), heads(vs_), sp, ss, shp, shs, sgu_v)):
            lst.append(val)
    return (xp, xs) + tuple(jnp.stack(lst) for lst in outs)
```

```python
import functools
import math

import jax
import jax.numpy as jnp
import numpy as np
from jax import lax
from jax.experimental import pallas as pl
from jax.experimental.pallas import tpu as pltpu

HEAD_DIM = 64
LANES = 128
SGU_CHUNK = 128
MOBA_BLOCK = 256
MOBA_TOPK = 3
REL_BUCKETS = 32
REL_MAX_DIST = 1024
DECAY_LORA = 64
AAA_LORA = 64
GATE_LORA = 128
GN_EPS = 64e-5
LN_EPS = 1e-5
N_EXPERT_GROUPS = 4
TOP_K_EXPERTS = 2
VMEM_LIMIT = 56 * 1024 * 1024
NEG_BIG = -1e30

f32 = jnp.float32
bf16 = jnp.bfloat16


def _cparams(sem):
    return pltpu.CompilerParams(dimension_semantics=sem, vmem_limit_bytes=VMEM_LIMIT)


def _dot(a, b):
    return jnp.dot(a, b, preferred_element_type=f32)


def _dot_nt(a, b):
    return lax.dot_general(a, b, (((1,), (1,)), ((), ())), preferred_element_type=f32)


def _dot_f32(a, b):
    return _dot(a.astype(bf16), b.astype(bf16))


def _dot_nt_f32(a, b):
    return _dot_nt(a.astype(bf16), b.astype(bf16))


def _gelu(x):
    c = math.sqrt(2.0 / math.pi)
    return 0.5 * x * (1.0 + jnp.tanh(c * (x + 0.044715 * (x * x * x))))


def _sigmoid(x):
    return 1.0 / (1.0 + jnp.exp(-x))


def _silu(x):
    return x * _sigmoid(x)


def _t5_bias(dist, rel_ref, head):
    max_exact = REL_BUCKETS // 2
    d = jnp.maximum(dist, 0)
    df = jnp.maximum(d, 1).astype(f32)
    large = max_exact + (jnp.log(df / max_exact) / math.log(REL_MAX_DIST / max_exact)
                         * (REL_BUCKETS - max_exact)).astype(jnp.int32)
    large = jnp.minimum(large, REL_BUCKETS - 1)
    bucket = jnp.where(d < max_exact, d, large)
    out = jnp.zeros(dist.shape, f32)
    for r in range(REL_BUCKETS):
        out = jnp.where(bucket == r, rel_ref[r, head], out)
    return out


def _bias_tiles_kernel(rel_ref, o_ref):
    h = pl.program_id(0)
    delta = pl.num_programs(1) - 1 - pl.program_id(1)
    i = lax.broadcasted_iota(jnp.int32, (MOBA_BLOCK, MOBA_BLOCK), 0)
    j = lax.broadcasted_iota(jnp.int32, (MOBA_BLOCK, MOBA_BLOCK), 1)
    dist = delta * MOBA_BLOCK + j - i
    o_ref[0, 0] = _t5_bias(dist, rel_ref, h)


def bias_tiles(rel_bias, n_heads, n_blocks):
    return pl.pallas_call(
        _bias_tiles_kernel,
        grid=(n_heads, n_blocks),
        in_specs=[pl.BlockSpec(memory_space=pltpu.SMEM)],
        out_specs=pl.BlockSpec((1, 1, MOBA_BLOCK, MOBA_BLOCK), lambda h, d: (h, d, 0, 0)),
        out_shape=jax.ShapeDtypeStruct((n_heads, n_blocks, MOBA_BLOCK, MOBA_BLOCK), f32),
        name="bias_tiles",
        compiler_params=_cparams(("arbitrary", "arbitrary")),
    )(rel_bias)


def _bias_rows_kernel(rel_ref, o_ref, *, q_start, t_q):
    h = pl.program_id(0)
    n_keys = o_ref.shape[2]
    i = lax.broadcasted_iota(jnp.int32, (t_q, n_keys), 0)
    j = lax.broadcasted_iota(jnp.int32, (t_q, n_keys), 1)
    o_ref[0] = _t5_bias(q_start + i - j, rel_ref, h)


def bias_rows(rel_bias, n_heads, q_start, t_q, n_keys):
    return pl.pallas_call(
        functools.partial(_bias_rows_kernel, q_start=q_start, t_q=t_q),
        grid=(n_heads,),
        in_specs=[pl.BlockSpec(memory_space=pltpu.SMEM)],
        out_specs=pl.BlockSpec((1, t_q, n_keys), lambda h: (h, 0, 0)),
        out_shape=jax.ShapeDtypeStruct((n_heads, t_q, n_keys), f32),
        name="bias_rows",
        compiler_params=_cparams(("arbitrary",)),
    )(rel_bias)


def _ada_kernel(c_ref, w_ref, b_ref, o_ref):
    c = _silu(c_ref[...]).astype(bf16)
    o_ref[0] = _dot(c, w_ref[0].astype(bf16)) + b_ref[0]


def ada_modulation(c_all, w_ada, b_ada, tn=1536):
    depth, d, n = w_ada.shape
    rows = c_all.shape[0]
    return pl.pallas_call(
        _ada_kernel,
        grid=(depth, n // tn),
        in_specs=[pl.BlockSpec((rows, d), lambda l, j: (0, 0)),
                  pl.BlockSpec((1, d, tn), lambda l, j: (l, 0, j)),
                  pl.BlockSpec((1, 1, tn), lambda l, j: (l, 0, j))],
        out_specs=pl.BlockSpec((1, rows, tn), lambda l, j: (l, 0, j)),
        out_shape=jax.ShapeDtypeStruct((depth, rows, n), f32),
        name="ada_modulation",
        compiler_params=_cparams(("arbitrary", "arbitrary")),
    )(c_all, w_ada, b_ada.reshape(depth, 1, n))


def _inproj_kernel(x_ref, mod_ref, w_ref, *o_refs, d, widths):
    bt, tt, _ = x_ref.shape
    sh = mod_ref[:, :, 0:d]
    sc = mod_ref[:, :, d:2 * d]
    h = (x_ref[...] * (1.0 + sc) + sh).reshape(bt * tt, d).astype(bf16)
    off = 0
    for o_ref, wd in zip(o_refs, widths):
        o_ref[...] = _dot(h, w_ref[:, off:off + wd]).reshape(bt, tt, wd)
        off += wd


def in_projection(x, mod, w_in_bf16, widths, bt, tt):
    b, t, d = x.shape
    n = w_in_bf16.shape[1]
    assert sum(widths) == n and b % bt == 0 and t % tt == 0
    return pl.pallas_call(
        functools.partial(_inproj_kernel, d=d, widths=widths),
        grid=(b // bt, t // tt),
        in_specs=[pl.BlockSpec((bt, tt, d), lambda i, j: (i, j, 0)),
                  pl.BlockSpec((bt, 1, mod.shape[2]), lambda i, j: (i, 0, 0)),
                  pl.BlockSpec((d, n), lambda i, j: (0, 0))],
        out_specs=[pl.BlockSpec((bt, tt, wd), lambda i, j: (i, j, 0)) for wd in widths],
        out_shape=[jax.ShapeDtypeStruct((b, t, wd), f32) for wd in widths],
        name="in_projection",
        compiler_params=_cparams(("arbitrary", "arbitrary")),
    )(x, mod, w_in_bf16)


def _seg_mean_matrix(width, seg):
    r = lax.broadcasted_iota(jnp.int32, (width, width), 0) // seg
    c = lax.broadcasted_iota(jnp.int32, (width, width), 1) // seg
    return jnp.where(r == c, 1.0 / seg, 0.0).astype(bf16)


def _seg_sum_matrix(width, seg):
    r = lax.broadcasted_iota(jnp.int32, (width, width), 0) // seg
    c = lax.broadcasted_iota(jnp.int32, (width, width), 1) // seg
    return jnp.where(r == c, 1.0, 0.0).astype(bf16)


def _seg_reduce(x, m):
    hi = x.astype(bf16)
    lo = (x - hi.astype(f32)).astype(bf16)
    return _dot(hi, m) + _dot(lo, m)


def _sgu_kernel(x_ref, w_ref, bias_ref, g_ref, b_ref, *o_refs, n_chunks, emit_v):
    wd = x_ref.shape[2] // 2
    n_groups = wd // HEAD_DIM
    x = x_ref[0]
    u = _gelu(x[:, :wd])
    v = _gelu(x[:, wd:])
    m = _seg_mean_matrix(wd, HEAD_DIM)
    mu = _seg_reduce(v, m)
    c = v - mu
    var = _seg_reduce(c * c, m)
    vn = c * lax.rsqrt(var + LN_EPS) * g_ref[...] + b_ref[...]
    if emit_v:
        o_refs[1][0] = vn
    row = lax.broadcasted_iota(jnp.int32, (SGU_CHUNK, SGU_CHUNK), 0)
    col = lax.broadcasted_iota(jnp.int32, (SGU_CHUNK, SGU_CHUNK), 1)
    lane_grp = lax.broadcasted_iota(jnp.int32, (SGU_CHUNK, wd), 1) // HEAD_DIM
    w_tril = [jnp.where(col <= row, w_ref[g], 0.0).astype(bf16) for g in range(n_groups)]
    for ci in range(n_chunks):
        rows = slice(ci * SGU_CHUNK, (ci + 1) * SGU_CHUNK)
        vc = vn[rows]
        s = bias_ref[...]
        for g in range(n_groups):
            s = s + _dot(w_tril[g], jnp.where(lane_grp == g, vc, 0.0).astype(bf16))
        o_refs[0][0, rows, :] = u[rows] * s


def sgu(puv, w_s, bias_tile, ln_g, ln_b, rows_per_step, emit_v):
    g_, r_, w2 = puv.shape
    wd = w2 // 2
    assert r_ % rows_per_step == 0 and rows_per_step % SGU_CHUNK == 0
    n_out = 2 if emit_v else 1
    outs = pl.pallas_call(
        functools.partial(_sgu_kernel, n_chunks=rows_per_step // SGU_CHUNK, emit_v=emit_v),
        grid=(g_, r_ // rows_per_step),
        in_specs=[pl.BlockSpec((1, rows_per_step, w2), lambda i, j: (i, j, 0)),
                  pl.BlockSpec(w_s.shape, lambda i, j: (0, 0, 0)),
                  pl.BlockSpec(bias_tile.shape, lambda i, j: (0, 0)),
                  pl.BlockSpec((1, wd), lambda i, j: (0, 0)),
                  pl.BlockSpec((1, wd), lambda i, j: (0, 0))],
        out_specs=[pl.BlockSpec((1, rows_per_step, wd), lambda i, j: (i, j, 0))] * n_out,
        out_shape=[jax.ShapeDtypeStruct((g_, r_, wd), f32)] * n_out,
        name="sgu",
        compiler_params=_cparams(("arbitrary", "arbitrary")),
    )(puv, w_s, bias_tile, ln_g.reshape(1, wd), ln_b.reshape(1, wd))
    return outs


def _topk_rows(scores, n_valid):
    row = lax.broadcasted_iota(jnp.int32, scores.shape, 0)
    valid = row < n_valid
    sel = jnp.zeros(scores.shape, f32)
    for j in range(n_valid):
        sj = scores[j:j + 1, :]
        beats = jnp.logical_and(valid, jnp.logical_or(scores > sj, jnp.logical_and(scores == sj, row < j)))
        cnt = jnp.sum(beats.astype(f32), axis=0, keepdims=True)
        sel = jnp.where(jnp.logical_and(row == j, cnt < MOBA_TOPK), 1.0, sel)
    return sel


def _moba_full_kernel(q_ref, k_ref, v_ref, bias_ref, o_ref, k16_ref, vt16_ref):
    t_len = q_ref.shape[1]
    nb = t_len // MOBA_BLOCK
    blk = MOBA_BLOCK
    scale = HEAD_DIM ** -0.5
    k2 = k_ref[0]
    k16_ref[...] = k2.astype(bf16)
    vt16_ref[...] = v_ref[0].T.astype(bf16)
    kmean16 = jnp.mean(k2.reshape(nb, blk, LANES), axis=1).astype(bf16)
    lane = lax.broadcasted_iota(jnp.int32, (1, LANES), 1)
    hmask = [(lane // HEAD_DIM == hh).astype(f32) for hh in range(2)]
    krow = lax.broadcasted_iota(jnp.int32, (blk, blk), 0)
    qcol = lax.broadcasted_iota(jnp.int32, (blk, blk), 1)
    causal = krow <= qcol
    drow = lax.broadcasted_iota(jnp.int32, (LANES, blk), 0)

    for qb in range(nb):
        n_keys = (qb + 1) * blk
        q2 = q_ref[0, qb * blk:(qb + 1) * blk, :]
        k_all = k16_ref[0:n_keys, :]
        vt_all = vt16_ref[:, 0:n_keys]
        outs = []
        for hh in range(2):
            qm = q2 * hmask[hh]
            q16 = (qm * scale).astype(bf16)
            ok = causal
            if qb > 0:
                sel_t = _topk_rows(_dot_nt(kmean16, qm.astype(bf16)), qb)
                past_ok = jnp.broadcast_to(sel_t[0:qb][:, None, :] > 0.5, (qb, blk, blk))
                ok = jnp.concatenate([past_ok.reshape(qb * blk, blk), causal], axis=0)
            bias = bias_ref[hh, (nb - 1 - qb) * blk:nb * blk, :]
            logits = jnp.where(ok, _dot_nt(k_all, q16) + bias, NEG_BIG)
            m = jnp.max(logits, axis=0, keepdims=True)
            p = jnp.exp(logits - m)
            denom = jnp.sum(p, axis=0, keepdims=True)
            outs.append(_dot(vt_all, p.astype(bf16)) / denom)
        o_ref[0, qb * blk:(qb + 1) * blk, :] = jnp.where(drow < HEAD_DIM, outs[0], outs[1]).T


def moba_full(q, k, v, bias_t):
    b, t, hw = q.shape
    n_pairs = hw // LANES
    qkv_spec = pl.BlockSpec((1, t, LANES), lambda p, i: (i, 0, p))
    return pl.pallas_call(
        _moba_full_kernel,
        grid=(n_pairs, b),
        in_specs=[qkv_spec, qkv_spec, qkv_spec,
                  pl.BlockSpec((2, t, MOBA_BLOCK), lambda p, i: (p, 0, 0))],
        out_specs=pl.BlockSpec((1, t, LANES), lambda p, i: (i, 0, p)),
        out_shape=jax.ShapeDtypeStruct((b, t, hw), f32),
        scratch_shapes=[pltpu.VMEM((t, LANES), bf16), pltpu.VMEM((LANES, t), bf16)],
        name="moba_full",
        compiler_params=_cparams(("arbitrary", "arbitrary")),
    )(q, k, v, bias_t)


PAGES_PER_STEP = 8


def _pair_queries(q2):
    lane = lax.broadcasted_iota(jnp.int32, (1, LANES), 1)
    return jnp.concatenate([q2 * (lane // HEAD_DIM == hh).astype(f32) for hh in range(2)], axis=0)


def _page_pair(page_ref, p):
    hd, page = page_ref.shape[3:]
    return page_ref[0, 0, 2 * p:2 * p + 2].reshape(2 * hd, page)


def _moba_paged_probs_kernel(pt_ref, q_ref, kn_ref, bias_ref, bown_ref, *refs, n_pairs):
    kp_refs = refs[:PAGES_PER_STEP]
    p_ref, pown_ref, logit_ref, bsum_ref = refs[PAGES_PER_STEP:]
    c = pl.program_id(1)
    n_steps = pl.num_programs(1)
    t_q = q_ref.shape[1]
    page = kp_refs[0].shape[4]
    pages_per_block = MOBA_BLOCK // page
    n_blocks = bsum_ref.shape[0]
    blocks_per_step = PAGES_PER_STEP // pages_per_block
    step_keys = PAGES_PER_STEP * page
    scale = HEAD_DIM ** -0.5

    qf = [_pair_queries(q_ref[0, :, p * LANES:(p + 1) * LANES]) for p in range(n_pairs)]
    q16 = [(x * scale).astype(bf16) for x in qf]
    for p in range(n_pairs):
        for jb in range(blocks_per_step):
            ksum = None
            for j in range(jb * pages_per_block, (jb + 1) * pages_per_block):
                kt = _page_pair(kp_refs[j], p)
                ksum = kt if ksum is None else ksum + kt
                logit_ref[p, c, :, j * page:(j + 1) * page] = _dot(q16[p], kt.astype(bf16))
            bsum_ref[c * blocks_per_step + jb, p] = ksum

    @pl.when(c == n_steps - 1)
    def _():
        n_keys = n_steps * step_keys
        blk_of_key = lax.broadcasted_iota(jnp.int32, (n_blocks, n_keys), 1) // MOBA_BLOCK
        expand = (blk_of_key == lax.broadcasted_iota(jnp.int32, (n_blocks, n_keys), 0)).astype(bf16)
        rq = lax.broadcasted_iota(jnp.int32, (2 * t_q, LANES), 0) % t_q
        ck = lax.broadcasted_iota(jnp.int32, (2 * t_q, LANES), 1)
        own_ok = ck <= rq
        zpad = jnp.zeros((LANES - t_q, LANES), f32)
        blk_lane = lax.broadcasted_iota(jnp.int32, (LANES, LANES), 1)
        for p in range(n_pairs):
            lanes = slice(p * LANES, (p + 1) * LANES)
            kmean_t = jnp.zeros((LANES, LANES), f32)
            for blk in range(n_blocks):
                col = jnp.sum(bsum_ref[blk, p], axis=1, keepdims=True) * (1.0 / MOBA_BLOCK)
                kmean_t = jnp.where(blk_lane == blk, col, kmean_t)
            scores = _dot(qf[p].astype(bf16), kmean_t.astype(bf16))
            sel_t = _topk_rows(scores.T[0:n_blocks], n_blocks)
            chosen = _dot(sel_t.T.astype(bf16), expand)
            lg = [jnp.where(chosen[:, s * step_keys:(s + 1) * step_keys] > 0.5,
                            logit_ref[p, s] + bias_ref[p, s], NEG_BIG) for s in range(n_steps)]
            kn = jnp.concatenate([kn_ref[0, :, lanes], zpad], axis=0).astype(bf16)
            own = jnp.where(own_ok, _dot_nt(q16[p], kn) + bown_ref[p], NEG_BIG)
            m = jnp.max(own, axis=1, keepdims=True)
            for x in lg:
                m = jnp.maximum(m, jnp.max(x, axis=1, keepdims=True))
            e_own = jnp.exp(own - m)
            e = [jnp.exp(x - m) for x in lg]
            denom = jnp.sum(e_own, axis=1, keepdims=True)
            for x in e:
                denom = denom + jnp.sum(x, axis=1, keepdims=True)
            inv = 1.0 / denom
            for s in range(n_steps):
                p_ref[0, p, s] = e[s] * inv
            pown_ref[0, p] = e_own * inv


def _moba_paged_mix_kernel(pt_ref, p_ref, pown_ref, vn_ref, *refs, n_pairs):
    vp_refs = refs[:PAGES_PER_STEP]
    o_ref, acc_ref = refs[PAGES_PER_STEP:]
    c = pl.program_id(1)
    n_steps = pl.num_programs(1)
    t_q = vn_ref.shape[1]
    page = vp_refs[0].shape[4]

    @pl.when(c == 0)
    def _():
        acc_ref[...] = jnp.zeros_like(acc_ref)

    for p in range(n_pairs):
        acc = acc_ref[p]
        for j in range(PAGES_PER_STEP):
            acc = acc + _dot_nt(p_ref[0, p, 0, :, j * page:(j + 1) * page].astype(bf16),
                                _page_pair(vp_refs[j], p).astype(bf16))
        acc_ref[p] = acc

    @pl.when(c == n_steps - 1)
    def _():
        lane = lax.broadcasted_iota(jnp.int32, (1, LANES), 1)
        zpad = jnp.zeros((LANES - t_q, LANES), f32)
        for p in range(n_pairs):
            lanes = slice(p * LANES, (p + 1) * LANES)
            vn = jnp.concatenate([vn_ref[0, :, lanes], zpad], axis=0).astype(bf16)
            acc = acc_ref[p] + _dot(pown_ref[0, p].astype(bf16), vn)
            o_ref[0, :, lanes] = jnp.where(lane < HEAD_DIM, acc[0:t_q], acc[t_q:2 * t_q])


def moba_paged(q, k_new, v_new, cache_k, cache_v, layer, page_table, bias_past, bias_own):
    b, t_q, hw = q.shape
    n_pairs = hw // LANES
    n_pages = page_table.shape[1]
    n_heads, hd, page = cache_k.shape[2:]
    assert page == LANES and hd == HEAD_DIM and n_heads * hd == hw
    assert n_pages % PAGES_PER_STEP == 0 and PAGES_PER_STEP % (MOBA_BLOCK // page) == 0
    n_steps = n_pages // PAGES_PER_STEP
    step_keys = PAGES_PER_STEP * page
    rows = 2 * t_q

    def page_spec(j):
        return pl.BlockSpec((1, 1, n_heads, hd, page),
                            lambda i, c, pt: (layer, pt[i, c * PAGES_PER_STEP + j], 0, 0, 0))

    new_spec = pl.BlockSpec((1, t_q, hw), lambda i, c, pt: (i, 0, 0))
    own_spec = pl.BlockSpec((1, n_pairs, rows, LANES), lambda i, c, pt: (i, 0, 0, 0))
    probs, p_own = pl.pallas_call(
        functools.partial(_moba_paged_probs_kernel, n_pairs=n_pairs),
        grid_spec=pltpu.PrefetchScalarGridSpec(
            num_scalar_prefetch=1,
            grid=(b, n_steps),
            in_specs=[new_spec, new_spec,
                      pl.BlockSpec(bias_past.shape, lambda i, c, pt: (0, 0, 0, 0)),
                      pl.BlockSpec(bias_own.shape, lambda i, c, pt: (0, 0, 0))]
                     + [page_spec(j) for j in range(PAGES_PER_STEP)],
            out_specs=[pl.BlockSpec((1, n_pairs, n_steps, rows, step_keys), lambda i, c, pt: (i, 0, 0, 0, 0)),
                       own_spec],
            scratch_shapes=[pltpu.VMEM((n_pairs, n_steps, rows, step_keys), f32),
                            pltpu.VMEM((n_pages * page // MOBA_BLOCK, n_pairs, LANES, page), f32)]),
        out_shape=[jax.ShapeDtypeStruct((b, n_pairs, n_steps, rows, step_keys), f32),
                   jax.ShapeDtypeStruct((b, n_pairs, rows, LANES), f32)],
        name="moba_paged_probs",
        compiler_params=_cparams(("arbitrary", "arbitrary")),
    )(page_table, q, k_new, bias_past, bias_own, *([cache_k] * PAGES_PER_STEP))
    return pl.pallas_call(
        functools.partial(_moba_paged_mix_kernel, n_pairs=n_pairs),
        grid_spec=pltpu.PrefetchScalarGridSpec(
            num_scalar_prefetch=1,
            grid=(b, n_steps),
            in_specs=[pl.BlockSpec((1, n_pairs, 1, rows, step_keys), lambda i, c, pt: (i, 0, c, 0, 0)),
                      own_spec, new_spec]
                     + [page_spec(j) for j in range(PAGES_PER_STEP)],
            out_specs=new_spec,
            scratch_shapes=[pltpu.VMEM((n_pairs, rows, LANES), f32)]),
        out_shape=jax.ShapeDtypeStruct((b, t_q, hw), f32),
        name="moba_paged_mix",
        compiler_params=_cparams(("arbitrary", "arbitrary")),
    )(page_table, probs, p_own, v_new, *([cache_v] * PAGES_PER_STEP))


def paged_bias_layout(bias_rows_arr, n_past):
    h, t_q, _ = bias_rows_arr.shape
    step_keys = PAGES_PER_STEP * LANES
    n_steps = n_past // step_keys
    past = bias_rows_arr[:, :, :n_past].reshape(h // 2, 2 * t_q, n_steps, step_keys)
    own = bias_rows_arr[:, :, n_past:].reshape(h // 2, 2 * t_q, LANES)
    return past.transpose(0, 2, 1, 3), own


def _rwkv_prep_kernel(p_ref, prev_ref, shift_ref, mu_ref, w0_ref, a0_ref, kk_ref, ka_ref, rk_ref,
                      ww_ref, wa_ref, wg_ref, *o_refs, hw):
    bt, tt, n = p_ref.shape
    j = pl.program_id(1)
    p = p_ref[...].reshape(bt * tt, n)
    first = jnp.where(j == 0, shift_ref[...], prev_ref[:, 7:8, :])
    first = jnp.broadcast_to(first, (bt, tt, n)).reshape(bt * tt, n)
    row = lax.broadcasted_iota(jnp.int32, (bt * tt, 1), 0)
    prev = jnp.where(row % tt == 0, first, pltpu.roll(p, 1, 0))
    xs = p + (prev - p) * mu_ref[...]
    r = xs[:, 0:hw]
    k = xs[:, hw:2 * hw]
    v = xs[:, 2 * hw:3 * hw]
    xwa = xs[:, 3 * hw:3 * hw + LANES]
    xg = xs[:, 3 * hw + LANES:3 * hw + 2 * LANES]
    z = -(w0_ref[...] + _dot_f32(jnp.tanh(xwa), ww_ref[...]))
    softplus = jnp.maximum(z, 0.0) + jnp.log(1.0 + jnp.exp(-jnp.abs(z)))
    log_decay = -jnp.exp(-softplus - 0.5)
    a = _sigmoid(a0_ref[...] + _dot_f32(xwa, wa_ref[...]))
    g = _dot_f32(_sigmoid(xg), wg_ref[...])
    mseg = _seg_sum_matrix(hw, HEAD_DIM)
    kk = k * kk_ref[...]
    kk = kk / jnp.maximum(jnp.sqrt(_seg_reduce(kk * kk, mseg)), 1e-12)
    k2 = k * (1.0 + (a - 1.0) * ka_ref[...])
    bonus = _seg_reduce(r * k2 * rk_ref[...], mseg) * v
    outs = (r, log_decay, k2, v, -kk, kk * a, g, bonus)
    for o_ref, val in zip(o_refs, outs):
        o_ref[...] = val.reshape(bt, tt, hw)


def rwkv_prep(prw, shift0, prm, bt, tt):
    b, t, n = prw.shape
    hw = prm['w0'].shape[-1]
    assert n == 3 * hw + 2 * LANES and tt % 8 == 0
    row = lambda x: x.reshape(1, -1)
    vec = lambda: pl.BlockSpec((1, hw), lambda i, j: (0, 0))
    mat = lambda: pl.BlockSpec((LANES, hw), lambda i, j: (0, 0))
    prev_blk = tt // 8
    return pl.pallas_call(
        functools.partial(_rwkv_prep_kernel, hw=hw),
        grid=(b // bt, t // tt),
        in_specs=[pl.BlockSpec((bt, tt, n), lambda i, j: (i, j, 0)),
                  pl.BlockSpec((bt, 8, n), lambda i, j: (i, jnp.maximum(j * prev_blk - 1, 0), 0)),
                  pl.BlockSpec((bt, 1, n), lambda i, j: (i, 0, 0)),
                  pl.BlockSpec((1, n), lambda i, j: (0, 0)),
                  vec(), vec(), vec(), vec(), vec(), mat(), mat(), mat()],
        out_specs=[pl.BlockSpec((bt, tt, hw), lambda i, j: (i, j, 0))] * 8,
        out_shape=[jax.ShapeDtypeStruct((b, t, hw), f32)] * 8,
        name="rwkv_prep",
        compiler_params=_cparams(("arbitrary", "arbitrary")),
    )(prw, prw, shift0.reshape(b, 1, n), row(prm['mu']), row(prm['w0']), row(prm['a0']),
      row(prm['k_k']), row(prm['k_a']), row(prm['r_k']), prm['w_w2p'], prm['w_a2p'], prm['w_g2'])


def rwkv_prep_params(mu, w0, w_w2, a0, w_a2, w_g2, k_k, k_a, r_k):
    hw = w0.shape[-1]
    zeros = jnp.zeros((LANES - DECAY_LORA, hw), w_w2.dtype)
    return dict(mu=mu, w0=w0, a0=a0, k_k=k_k, k_a=k_a, r_k=r_k.reshape(-1),
                w_w2p=jnp.concatenate([w_w2, zeros], axis=0),
                w_a2p=jnp.concatenate([zeros, w_a2], axis=0), w_g2=w_g2)


def _split3(x):
    hi = x.astype(bf16)
    return hi, (x - hi.astype(f32)).astype(bf16)


def _dot3(a, b):
    ah, al = _split3(a)
    bh, bl = _split3(b)
    return _dot(ah, bh) + (_dot(ah, bl) + _dot(al, bh))


def _dot3_nt(a, b):
    ah, al = _split3(a)
    bh, bl = _split3(b)
    return _dot_nt(ah, bh) + (_dot_nt(ah, bl) + _dot_nt(al, bh))


def _dot3_tn(a, b):
    return _dot3(a.T, b)


def _rwkv_chunk_maps_kernel(r_ref, lw_ref, k_ref, v_ref, x_ref, b_ref, m_ref, n_ref, p_ref, q_ref,
                            *, n_pairs, c_len):
    two_c = 2 * c_len
    rr = lax.broadcasted_iota(jnp.int32, (two_c, two_c), 0)
    cc = lax.broadcasted_iota(jnp.int32, (two_c, two_c), 1)
    same_head = (rr // c_len) == (cc // c_len)
    strict = jnp.logical_and(same_head, (cc % c_len) < (rr % c_len))
    incl = jnp.logical_and(same_head, (cc % c_len) <= (rr % c_len))
    eye = (rr == cc).astype(f32)
    row_head = lax.broadcasted_iota(jnp.int32, (two_c, LANES), 0) // c_len
    lane_head = lax.broadcasted_iota(jnp.int32, (two_c, LANES), 1) // HEAD_DIM
    bd = row_head == lane_head
    eye_k = (lax.broadcasted_iota(jnp.int32, (LANES, LANES), 0)
             == lax.broadcasted_iota(jnp.int32, (LANES, LANES), 1)).astype(f32)
    trow = lax.broadcasted_iota(jnp.int32, (c_len, LANES), 0)
    n_doubling = max(1, (c_len - 1).bit_length())

    def stack(a):
        return jnp.where(bd, jnp.concatenate([a, a], axis=0), 0.0)

    jobs = [(ci, p) for ci in range(r_ref.shape[1] // c_len) for p in range(n_pairs)]
    ops = []
    for ci, p in jobs:
        rows = slice(ci * c_len, (ci + 1) * c_len)
        lanes = slice(p * LANES, (p + 1) * LANES)
        lw = lw_ref[0, rows, lanes]
        cum = lw
        sh = 1
        while sh < c_len:
            cum = cum + jnp.where(trow >= sh, pltpu.roll(cum, sh, 0), 0.0)
            sh *= 2
        gam = jnp.exp(cum)
        inv_gam = jnp.exp(-cum)
        ops.append(dict(
            a=stack(x_ref[0, rows, lanes] * jnp.exp(cum - lw)), r=stack(r_ref[0, rows, lanes] * gam),
            b=stack(b_ref[0, rows, lanes] * inv_gam), k=stack(k_ref[0, rows, lanes] * inv_gam),
            v=stack(v_ref[0, rows, lanes]), g_end=gam[c_len - 1:c_len, :]))
    for o in ops:
        bk = jnp.concatenate([o['b'], o['k']], axis=0)
        gram_a = _dot3_nt(o['a'], bk)
        gram_r = _dot_nt_f32(o['r'], bk)
        o['l_ab'] = jnp.where(strict, gram_a[:, 0:two_c], 0.0)
        o['l_ak'] = jnp.where(strict, gram_a[:, two_c:2 * two_c], 0.0)
        o['m_rb'] = jnp.where(incl, gram_r[:, 0:two_c], 0.0)
        o['m_rk'] = jnp.where(incl, gram_r[:, two_c:2 * two_c], 0.0)
        o['pw'] = o['l_ab']
        o['u'] = o['l_ab']
    for it in range(n_doubling - 1):
        mm = _dot3 if it == 0 else _dot_f32
        for o in ops:
            o['pw'] = mm(o['pw'], o['pw'])
        for o in ops:
            o['u'] = o['u'] + o['pw'] + mm(o['u'], o['pw'])
    for o in ops:
        o['lkv'] = _dot3(o['l_ak'], o['v'])
    for o in ops:
        o['pq1'] = _dot3(eye + o['u'], jnp.concatenate([o['a'], o['lkv']], axis=1))
    for o in ops:
        o['pq2'] = _dot_f32(o['m_rb'], o['pq1'])
        o['mkv'] = _dot_f32(o['m_rk'], o['v'])
    for o in ops:
        o['mn'] = _dot3_tn(o['pq1'], o['b'] * o['g_end'])
        o['vk'] = _dot3_tn(o['v'], o['k'] * o['g_end'])
    for (ci, p), o in zip(jobs, ops):
        p_ref[0, ci, p] = o['r'] + o['pq2'][:, 0:LANES]
        q_ref[0, ci, p] = o['pq2'][:, LANES:2 * LANES] + o['mkv']
        m_ref[0, ci, p] = eye_k * o['g_end'] + o['mn'][0:LANES]
        n_ref[0, ci, p] = o['mn'][LANES:2 * LANES] + o['vk']


def _rwkv_state_kernel(m_ref, n_ref, p_ref, q_ref, g_ref, bonus_ref, s0_ref, lng_ref, lnb_ref,
                       y_ref, sout_ref, state_ref, *, n_pairs, c_len):
    jc = pl.program_id(1)
    n_steps = pl.num_programs(1)

    @pl.when(jc == 0)
    def _():
        state_ref[...] = s0_ref[0]

    mseg = _seg_mean_matrix(LANES, HEAD_DIM)
    states = [state_ref[p] for p in range(n_pairs)]
    for ci in range(m_ref.shape[1]):
        rows = slice(ci * c_len, (ci + 1) * c_len)
        outs = [_dot3_nt(p_ref[0, ci, p], states[p]) + q_ref[0, ci, p] for p in range(n_pairs)]
        states = [_dot3(states[p], m_ref[0, ci, p]) + n_ref[0, ci, p] for p in range(n_pairs)]
        for p in range(n_pairs):
            lanes = slice(p * LANES, (p + 1) * LANES)
            o = outs[p][0:c_len] + outs[p][c_len:2 * c_len]
            mu = _seg_reduce(o, mseg)
            cen = o - mu
            var = _seg_reduce(cen * cen, mseg)
            y = cen * lax.rsqrt(var + GN_EPS) * lng_ref[:, lanes] + lnb_ref[:, lanes]
            y_ref[0, rows, lanes] = (y + bonus_ref[0, rows, lanes]) * g_ref[0, rows, lanes]
    for p in range(n_pairs):
        state_ref[p] = states[p]

    @pl.when(jc == n_steps - 1)
    def _():
        sout_ref[0] = state_ref[...]


def rwkv_chunked(seq, s0_bd, lnx_g, lnx_b, c_len, maps_chunks, scan_chunks):
    r, lw, k2, v, x, kka, g, bonus = seq
    b, t, hw = r.shape
    n_pairs = hw // LANES
    n_chunks = t // c_len
    assert t % c_len == 0 and n_chunks % maps_chunks == 0 and n_chunks % scan_chunks == 0
    two_c = 2 * c_len

    def map_spec(rows, per_step):
        return pl.BlockSpec((1, per_step, n_pairs, rows, LANES), lambda i, j: (i, j, 0, 0, 0))

    def map_shape(rows):
        return jax.ShapeDtypeStruct((b, n_chunks, n_pairs, rows, LANES), f32)

    seq_spec = lambda per_step: pl.BlockSpec((1, per_step * c_len, hw), lambda i, j: (i, j, 0))
    maps = pl.pallas_call(
        functools.partial(_rwkv_chunk_maps_kernel, n_pairs=n_pairs, c_len=c_len),
        grid=(b, n_chunks // maps_chunks),
        in_specs=[seq_spec(maps_chunks)] * 6,
        out_specs=[map_spec(LANES, maps_chunks), map_spec(LANES, maps_chunks),
                   map_spec(two_c, maps_chunks), map_spec(two_c, maps_chunks)],
        out_shape=[map_shape(LANES), map_shape(LANES), map_shape(two_c), map_shape(two_c)],
        name="rwkv_chunk_maps",
        compiler_params=_cparams(("arbitrary", "arbitrary")),
    )(r, lw, k2, v, x, kka)
    st_spec = pl.BlockSpec((1, n_pairs, LANES, LANES), lambda i, j: (i, 0, 0, 0))
    vec = pl.BlockSpec((1, hw), lambda i, j: (0, 0))
    return pl.pallas_call(
        functools.partial(_rwkv_state_kernel, n_pairs=n_pairs, c_len=c_len),
        grid=(b, n_chunks // scan_chunks),
        in_specs=[map_spec(LANES, scan_chunks), map_spec(LANES, scan_chunks),
                  map_spec(two_c, scan_chunks), map_spec(two_c, scan_chunks),
                  seq_spec(scan_chunks), seq_spec(scan_chunks), st_spec, vec, vec],
        out_specs=[seq_spec(scan_chunks), st_spec],
        out_shape=[jax.ShapeDtypeStruct((b, t, hw), f32),
                   jax.ShapeDtypeStruct((b, n_pairs, LANES, LANES), f32)],
        scratch_shapes=[pltpu.VMEM((n_pairs, LANES, LANES), f32)],
        name="rwkv_state_scan",
        compiler_params=_cparams(("arbitrary", "arbitrary")),
    )(*maps, g, bonus, s0_bd, lnx_g.reshape(1, hw), lnx_b.reshape(1, hw))


def state_to_blockdiag(s):
    b, h, n, _ = s.shape
    sp = s.reshape(b, h // 2, 2, n, n)
    z = jnp.zeros_like(sp[:, :, 0])
    top = jnp.concatenate([sp[:, :, 0], z], axis=-1)
    bot = jnp.concatenate([z, sp[:, :, 1]], axis=-1)
    return jnp.concatenate([top, bot], axis=-2)


def state_from_blockdiag(sbd):
    b, hp, _, _ = sbd.shape
    n = HEAD_DIM
    return jnp.stack([sbd[:, :, :n, :n], sbd[:, :, n:, n:]], axis=2).reshape(b, hp * 2, n, n)


def _layer_norm(z, g, b):
    mu = jnp.mean(z, axis=-1, keepdims=True)
    c = z - mu
    var = jnp.mean(c * c, axis=-1, keepdims=True)
    return c * lax.rsqrt(var + LN_EPS) * g + b


def _route(sel, aff, n_groups, per_group, top_k):
    rows = sel.shape[1]
    srow = [sel[e:e + 1, :] for e in range(n_groups * per_group)]
    arow = [aff[e:e + 1, :] for e in range(n_groups * per_group)]
    scores = []
    for g in range(n_groups):
        s = srow[g * per_group:(g + 1) * per_group]
        best = None
        for i in range(per_group):
            for j in range(i + 1, per_group):
                pair = s[i] + s[j]
                best = pair if best is None else jnp.maximum(best, pair)
        scores.append(best)
    top = scores[0]
    for g in range(1, n_groups):
        top = jnp.maximum(top, scores[g])
    taken = jnp.zeros((1, rows), jnp.bool_)
    eidx = lax.broadcasted_iota(jnp.int32, (n_groups * per_group, rows), 0)
    gates = jnp.zeros((n_groups * per_group, rows), f32)
    for g in range(n_groups):
        is_grp = jnp.logical_and(scores[g] == top, jnp.logical_not(taken))
        taken = jnp.logical_or(taken, is_grp)
        s = srow[g * per_group:(g + 1) * per_group]
        a = arow[g * per_group:(g + 1) * per_group]
        chosen = []
        for i in range(per_group):
            cnt = jnp.zeros((1, rows), f32)
            for j in range(per_group):
                if j != i:
                    ahead = (s[j] > s[i]) if j > i else (s[j] >= s[i])
                    cnt = cnt + ahead.astype(f32)
            chosen.append(jnp.logical_and(cnt < top_k, is_grp))
        denom = jnp.zeros((1, rows), f32)
        for i in range(per_group):
            denom = denom + jnp.where(chosen[i], a[i], 0.0)
        for i in range(per_group):
            gate = jnp.where(chosen[i], a[i] / denom, 0.0)
            gates = jnp.where(eidx == g * per_group + i, gate, gates)
    return gates


def _outproj_kernel(x_ref, ya_ref, yb_ref, yc_ref, mod_ref, w_ref, g_ref, b_ref, wr_ref, rb_ref,
                    x1_ref, h2_ref, gates_ref, *, d, alpha):
    bt, tt, _ = x_ref.shape
    rows = bt * tt
    off = 0
    mix = jnp.zeros((rows, d), f32)
    for y_ref in (ya_ref, yb_ref, yc_ref):
        wd = y_ref.shape[2]
        mix = mix + _dot(y_ref[...].reshape(rows, wd).astype(bf16), w_ref[off:off + wd, :])
        off += wd
    g1 = mod_ref[:, :, 2 * d:3 * d]
    z = alpha * x_ref[...] + (1.0 + g1) * mix.reshape(bt, tt, d)
    x1 = _layer_norm(z, g_ref[...], b_ref[...])
    x1_ref[...] = x1
    sh2 = mod_ref[:, :, 3 * d:4 * d]
    sc2 = mod_ref[:, :, 4 * d:5 * d]
    h2 = (x1 * (1.0 + sc2) + sh2).reshape(rows, d)
    h2_ref[...] = h2.astype(bf16).reshape(bt, tt, d)
    aff = _sigmoid(_dot_nt_f32(wr_ref[...], h2))
    gates_ref[...] = _route(aff + rb_ref[...], aff, N_EXPERT_GROUPS,
                            wr_ref.shape[0] // N_EXPERT_GROUPS, TOP_K_EXPERTS)


def out_projection(x, ya, yb, yc, mod, w_out_bf16, ln_g, ln_b, w_router_t, router_bias, alpha, bt, tt):
    b, t, d = x.shape
    n_exp = w_router_t.shape[0]
    rows = bt * tt
    blk = lambda wd: pl.BlockSpec((bt, tt, wd), lambda i, j: (i, j, 0))
    full = lambda a: pl.BlockSpec(a.shape, lambda i, j: (0,) * a.ndim)
    n_tblk = t // tt
    ln_g = ln_g.reshape(1, d)
    ln_b = ln_b.reshape(1, d)
    rb = router_bias.reshape(n_exp, 1)
    return pl.pallas_call(
        functools.partial(_outproj_kernel, d=d, alpha=alpha),
        grid=(b // bt, n_tblk),
        in_specs=[blk(d), blk(ya.shape[2]), blk(yb.shape[2]), blk(yc.shape[2]),
                  pl.BlockSpec((bt, 1, mod.shape[2]), lambda i, j: (i, 0, 0)),
                  full(w_out_bf16), full(ln_g), full(ln_b), full(w_router_t), full(rb)],
        out_specs=[blk(d), blk(d), pl.BlockSpec((n_exp, rows), lambda i, j: (0, i * n_tblk + j))],
        out_shape=[jax.ShapeDtypeStruct((b, t, d), f32), jax.ShapeDtypeStruct((b, t, d), bf16),
                   jax.ShapeDtypeStruct((n_exp, b * t), f32)],
        name="out_projection",
        compiler_params=_cparams(("arbitrary", "arbitrary")),
    )(x, ya, yb, yc, mod, w_out_bf16, ln_g, ln_b, w_router_t, rb)


def _moe_kernel(h_ref, gates_ref, x1_ref, mod_ref, wg_ref, wu_ref, wd_ref, g_ref, b_ref,
                o_ref, acc_ref, *, d, alpha):
    bt, tt, _ = h_ref.shape
    rows = bt * tt
    per_step, f, _ = wd_ref.shape
    eg = pl.program_id(2)
    n_steps = pl.num_programs(2)

    @pl.when(eg == 0)
    def _():
        acc_ref[...] = jnp.zeros_like(acc_ref)

    h = h_ref[...].reshape(rows, d)
    gates = gates_ref[...].reshape(rows, gates_ref.shape[2])
    lane = lax.broadcasted_iota(jnp.int32, gates.shape, 1)
    acts = []
    for i in range(per_step):
        gate = jnp.sum(jnp.where(lane == eg * per_step + i, gates, 0.0), axis=1, keepdims=True)
        acts.append((_silu(_dot(h, wg_ref[i])) * _dot(h, wu_ref[i]) * gate).astype(bf16))
    acc_ref[...] += _dot(jnp.concatenate(acts, axis=1), wd_ref[...].reshape(per_step * f, d))

    @pl.when(eg == n_steps - 1)
    def _():
        g2 = mod_ref[:, :, 5 * d:6 * d]
        z = alpha * x1_ref[...] + (1.0 + g2) * acc_ref[...].reshape(bt, tt, d)
        o_ref[...] = _layer_norm(z, g_ref[...], b_ref[...])


def moe_ffn(h2, gates, x1, mod, wg_bf16, wu_bf16, wd_bf16, ln_g, ln_b, alpha, bt, tt):
    b, t, d = x1.shape
    n_exp = wg_bf16.shape[0]
    f = wg_bf16.shape[2]
    blk = lambda wd: pl.BlockSpec((bt, tt, wd), lambda i, j, e: (i, j, 0))
    per_step = n_exp // N_EXPERT_GROUPS
    return pl.pallas_call(
        functools.partial(_moe_kernel, d=d, alpha=alpha),
        grid=(b // bt, t // tt, n_exp // per_step),
        in_specs=[blk(d), blk(n_exp), blk(d),
                  pl.BlockSpec((bt, 1, mod.shape[2]), lambda i, j, e: (i, 0, 0)),
                  pl.BlockSpec((per_step, d, f), lambda i, j, e: (e, 0, 0)),
                  pl.BlockSpec((per_step, d, f), lambda i, j, e: (e, 0, 0)),
                  pl.BlockSpec((per_step, f, d), lambda i, j, e: (e, 0, 0)),
                  pl.BlockSpec((1, d), lambda i, j, e: (0, 0)),
                  pl.BlockSpec((1, d), lambda i, j, e: (0, 0))],
        out_specs=blk(d),
        out_shape=jax.ShapeDtypeStruct((b, t, d), f32),
        scratch_shapes=[pltpu.VMEM((bt * tt, d), f32)],
        name="moe_ffn",
        compiler_params=_cparams(("arbitrary", "arbitrary", "arbitrary")),
    )(h2, gates, x1, mod, wg_bf16, wu_bf16, wd_bf16, ln_g.reshape(1, d), ln_b.reshape(1, d))


def sgu_full_params(w_s, b_s):
    return w_s, jnp.repeat(b_s.T, HEAD_DIM, axis=1)


def sgu_short_params(w_s, b_s, t_len):
    reps = SGU_CHUNK // t_len
    eye = jnp.eye(reps, dtype=w_s.dtype)
    w_small = w_s[:, :t_len, :t_len]
    w_big = jnp.einsum('ab,gts->gatbs', eye, w_small).reshape(w_s.shape[0], SGU_CHUNK, SGU_CHUNK)
    bias = jnp.tile(jnp.repeat(b_s[:, :t_len].T, HEAD_DIM, axis=1), (reps, 1))
    return w_big, bias


PROMPT_ROWS = 512
RWKV_CHUNK = (64, 2, 4)


def _hybrid_layer(x, mod, tile, wts, attend, sgu_prm, shift0, s0_bd, rwkv_chunk, emit_sgu_v, alpha):
    bt, tt = tile
    b, t, d = x.shape
    hw_sgu = sgu_prm[0].shape[0] * HEAD_DIM
    hw_attn = wts['attn_width']
    widths = (2 * hw_sgu, hw_attn, hw_attn, hw_attn, wts['w_in'].shape[1] - 2 * hw_sgu - 3 * hw_attn)
    puv, q, k, v, prw = in_projection(x, mod, wts['w_in'], widths, bt, tt)
    n_rows = b * t
    sgu_rows = min(PROMPT_ROWS, n_rows) if t < SGU_CHUNK else tt
    sgu_in = puv.reshape(1, n_rows, 2 * hw_sgu) if t < SGU_CHUNK else puv
    sgu_out = sgu(sgu_in, sgu_prm[0], sgu_prm[1], wts['sgu_ln_g'], wts['sgu_ln_b'], sgu_rows, emit_sgu_v)
    ya = sgu_out[0].reshape(b, t, hw_sgu)
    sgu_v = sgu_out[1].reshape(b, t, hw_sgu) if emit_sgu_v else None
    yb = attend(q, k, v)
    seq = rwkv_prep(prw, shift0, wts['rwkv'], bt, tt)
    yc, s_bd = rwkv_chunked(seq, s0_bd, wts['lnx_g'], wts['lnx_b'], *rwkv_chunk)
    x1, h2, gates_t = out_projection(x, ya, yb, yc, mod, wts['w_out'], wts['ln1_g'], wts['ln1_b'],
                                     wts['w_router_t'], wts['router_bias'], alpha, bt, tt)
    gates = gates_t.T.reshape(b, t, gates_t.shape[0])
    x2 = moe_ffn(h2, gates, x1, mod, wts['moe_wg'], wts['moe_wu'], wts['moe_wd'],
                 wts['ln2_g'], wts['ln2_b'], alpha, bt, tt)
    return x2, k, v, state_from_blockdiag(s_bd), prw[:, -1], sgu_v


def kernel(x_prompt, x_sample, cache_k, cache_v, state_rwkv, state_shift, page_table, c_prompt, c_sample, w_ada, b_ada, w_in, w_out, sgu_w, sgu_b, sgu_ln_g, sgu_ln_b, rel_bias, rwkv_mu, rwkv_w0, rwkv_w_w2, rwkv_a0, rwkv_w_a2, rwkv_w_g2, rwkv_k_k, rwkv_k_a, rwkv_r_k, rwkv_lnx_g, rwkv_lnx_b, ln1_g, ln1_b, ln2_g, ln2_b, w_router, router_bias, moe_w_gate, moe_w_up, moe_w_down):
    depth = w_ada.shape[0]
    bp, t_p, d = x_prompt.shape
    bs, t_s, _ = x_sample.shape
    n_heads = cache_k.shape[3]
    hw_attn = n_heads * HEAD_DIM
    n_pool, page = cache_k.shape[1], cache_k.shape[2]
    past_len = page_table.shape[1] * page
    rwkv_in = state_shift.shape[2]
    alpha = (2 * depth) ** 0.25

    mods = ada_modulation(jnp.concatenate([c_prompt, c_sample], axis=0), w_ada, b_ada)
    bias_p = bias_tiles(rel_bias, n_heads, t_p // MOBA_BLOCK).reshape(n_heads, t_p, MOBA_BLOCK)
    bias_past, bias_own = paged_bias_layout(
        bias_rows(rel_bias, n_heads, past_len, t_s, past_len + LANES), past_len)

    def attend_paged(layer):
        def attend(q, k, v):
            return moba_paged(q, k, v, cache_kt, cache_vt, layer, page_table, bias_past, bias_own)
        return attend

    cache_kt = cache_k.transpose(0, 1, 3, 4, 2)
    cache_vt = cache_v.transpose(0, 1, 3, 4, 2)
    w_in16, w_out16 = w_in.astype(bf16), w_out.astype(bf16)
    wg16, wu16, wd16 = moe_w_gate.astype(bf16), moe_w_up.astype(bf16), moe_w_down.astype(bf16)
    w_router_t = w_router.T
    zero_shift = jnp.zeros((bp, rwkv_in), f32)
    zero_state = jnp.zeros((bp, n_heads // 2, LANES, LANES), f32)

    xp, xs = x_prompt, x_sample
    outs = [[] for _ in range(9)]
    for l in range(depth):
        wts = dict(
            attn_width=hw_attn, w_in=w_in16[l], w_out=w_out16[l], sgu_ln_g=sgu_ln_g[l], sgu_ln_b=sgu_ln_b[l],
            rwkv=rwkv_prep_params(rwkv_mu[l], rwkv_w0[l], rwkv_w_w2[l], rwkv_a0[l], rwkv_w_a2[l],
                                  rwkv_w_g2[l], rwkv_k_k[l], rwkv_k_a[l], rwkv_r_k[l]),
            lnx_g=rwkv_lnx_g[l], lnx_b=rwkv_lnx_b[l], ln1_g=ln1_g[l], ln1_b=ln1_b[l],
            ln2_g=ln2_g[l], ln2_b=ln2_b[l], w_router_t=w_router_t, router_bias=router_bias,
            moe_wg=wg16[l], moe_wu=wu16[l], moe_wd=wd16[l])
        mod_p = mods[l, :bp][:, None, :]
        mod_s = mods[l, bp:][:, None, :]
        xp, kp, vp, sp, shp, _ = _hybrid_layer(
            xp, mod_p, (1, PROMPT_ROWS), wts, lambda q, k, v: moba_full(q, k, v, bias_p),
            sgu_full_params(sgu_w[l], sgu_b[l]), zero_shift, zero_state, RWKV_CHUNK, False, alpha)
        xs, ks_, vs_, ss, shs, sgu_v = _hybrid_layer(
            xs, mod_s, (bs, t_s), wts, attend_paged(l),
            sgu_short_params(sgu_w[l], sgu_b[l], t_s), state_shift[l],
            state_to_blockdiag(state_rwkv[l]), (t_s, 1, 1), True, alpha)
        heads = lambda a: a.reshape(a.shape[0], a.shape[1], n_heads, HEAD_DIM)
        for lst, val in zip(outs, (heads(kp), heads(vp), heads(ks_), heads(vs_), sp, ss, shp, shs, sgu_v)):
            lst.append(val)
    return (xp, xs) + tuple(jnp.stack(lst) for lst in outs)
```

```python
import functools
import math

import jax
import jax.numpy as jnp
import numpy as np
from jax import lax
from jax.experimental import pallas as pl
from jax.experimental.pallas import tpu as pltpu

HEAD_DIM = 64
LANES = 128
SGU_CHUNK = 128
MOBA_BLOCK = 256
MOBA_TOPK = 3
REL_BUCKETS = 32
REL_MAX_DIST = 1024
DECAY_LORA = 64
AAA_LORA = 64
GATE_LORA = 128
GN_EPS = 64e-5
LN_EPS = 1e-5
N_EXPERT_GROUPS = 4
TOP_K_EXPERTS = 2
VMEM_LIMIT = 56 * 1024 * 1024
NEG_BIG = -1e30

f32 = jnp.float32
bf16 = jnp.bfloat16


def _cparams(sem):
    return pltpu.CompilerParams(dimension_semantics=sem, vmem_limit_bytes=VMEM_LIMIT)


def _dot(a, b):
    return jnp.dot(a, b, preferred_element_type=f32)


def _dot_nt(a, b):
    return lax.dot_general(a, b, (((1,), (1,)), ((), ())), preferred_element_type=f32)


def _dot_f32(a, b):
    return _dot(a.astype(bf16), b.astype(bf16))


def _dot_nt_f32(a, b):
    return _dot_nt(a.astype(bf16), b.astype(bf16))


def _gelu(x):
    c = math.sqrt(2.0 / math.pi)
    return 0.5 * x * (1.0 + jnp.tanh(c * (x + 0.044715 * (x * x * x))))


def _sigmoid(x):
    return 1.0 / (1.0 + jnp.exp(-x))


def _silu(x):
    return x * _sigmoid(x)


def _t5_bias(dist, rel_ref, head):
    max_exact = REL_BUCKETS // 2
    d = jnp.maximum(dist, 0)
    df = jnp.maximum(d, 1).astype(f32)
    large = max_exact + (jnp.log(df / max_exact) / math.log(REL_MAX_DIST / max_exact)
                         * (REL_BUCKETS - max_exact)).astype(jnp.int32)
    large = jnp.minimum(large, REL_BUCKETS - 1)
    bucket = jnp.where(d < max_exact, d, large)
    out = jnp.zeros(dist.shape, f32)
    for r in range(REL_BUCKETS):
        out = jnp.where(bucket == r, rel_ref[r, head], out)
    return out


def _bias_tiles_kernel(rel_ref, o_ref):
    h = pl.program_id(0)
    delta = pl.num_programs(1) - 1 - pl.program_id(1)
    i = lax.broadcasted_iota(jnp.int32, (MOBA_BLOCK, MOBA_BLOCK), 0)
    j = lax.broadcasted_iota(jnp.int32, (MOBA_BLOCK, MOBA_BLOCK), 1)
    dist = delta * MOBA_BLOCK + j - i
    o_ref[0, 0] = _t5_bias(dist, rel_ref, h)


def bias_tiles(rel_bias, n_heads, n_blocks):
    return pl.pallas_call(
        _bias_tiles_kernel,
        grid=(n_heads, n_blocks),
        in_specs=[pl.BlockSpec(memory_space=pltpu.SMEM)],
        out_specs=pl.BlockSpec((1, 1, MOBA_BLOCK, MOBA_BLOCK), lambda h, d: (h, d, 0, 0)),
        out_shape=jax.ShapeDtypeStruct((n_heads, n_blocks, MOBA_BLOCK, MOBA_BLOCK), f32),
        name="bias_tiles",
        compiler_params=_cparams(("arbitrary", "arbitrary")),
    )(rel_bias)


def _bias_rows_kernel(rel_ref, o_ref, *, q_start, t_q):
    h = pl.program_id(0)
    n_keys = o_ref.shape[2]
    i = lax.broadcasted_iota(jnp.int32, (t_q, n_keys), 0)
    j = lax.broadcasted_iota(jnp.int32, (t_q, n_keys), 1)
    o_ref[0] = _t5_bias(q_start + i - j, rel_ref, h)


def bias_rows(rel_bias, n_heads, q_start, t_q, n_keys):
    return pl.pallas_call(
        functools.partial(_bias_rows_kernel, q_start=q_start, t_q=t_q),
        grid=(n_heads,),
        in_specs=[pl.BlockSpec(memory_space=pltpu.SMEM)],
        out_specs=pl.BlockSpec((1, t_q, n_keys), lambda h: (h, 0, 0)),
        out_shape=jax.ShapeDtypeStruct((n_heads, t_q, n_keys), f32),
        name="bias_rows",
        compiler_params=_cparams(("arbitrary",)),
    )(rel_bias)


def _ada_kernel(c_ref, w_ref, b_ref, o_ref):
    c = _silu(c_ref[...]).astype(bf16)
    o_ref[0] = _dot(c, w_ref[0].astype(bf16)) + b_ref[0]


def ada_modulation(c_all, w_ada, b_ada, tn=1536):
    depth, d, n = w_ada.shape
    rows = c_all.shape[0]
    return pl.pallas_call(
        _ada_kernel,
        grid=(depth, n // tn),
        in_specs=[pl.BlockSpec((rows, d), lambda l, j: (0, 0)),
                  pl.BlockSpec((1, d, tn), lambda l, j: (l, 0, j)),
                  pl.BlockSpec((1, 1, tn), lambda l, j: (l, 0, j))],
        out_specs=pl.BlockSpec((1, rows, tn), lambda l, j: (l, 0, j)),
        out_shape=jax.ShapeDtypeStruct((depth, rows, n), f32),
        name="ada_modulation",
        compiler_params=_cparams(("arbitrary", "arbitrary")),
    )(c_all, w_ada, b_ada.reshape(depth, 1, n))


def _inproj_kernel(x_ref, mod_ref, w_ref, *o_refs, d, widths):
    bt, tt, _ = x_ref.shape
    sh = mod_ref[:, :, 0:d]
    sc = mod_ref[:, :, d:2 * d]
    h = (x_ref[...] * (1.0 + sc) + sh).reshape(bt * tt, d).astype(bf16)
    off = 0
    for o_ref, wd in zip(o_refs, widths):
        o_ref[...] = _dot(h, w_ref[:, off:off + wd]).reshape(bt, tt, wd)
        off += wd


def in_projection(x, mod, w_in_bf16, widths, bt, tt):
    b, t, d = x.shape
    n = w_in_bf16.shape[1]
    assert sum(widths) == n and b % bt == 0 and t % tt == 0
    return pl.pallas_call(
        functools.partial(_inproj_kernel, d=d, widths=widths),
        grid=(b // bt, t // tt),
        in_specs=[pl.BlockSpec((bt, tt, d), lambda i, j: (i, j, 0)),
                  pl.BlockSpec((bt, 1, mod.shape[2]), lambda i, j: (i, 0, 0)),
                  pl.BlockSpec((d, n), lambda i, j: (0, 0))],
        out_specs=[pl.BlockSpec((bt, tt, wd), lambda i, j: (i, j, 0)) for wd in widths],
        out_shape=[jax.ShapeDtypeStruct((b, t, wd), f32) for wd in widths],
        name="in_projection",
        compiler_params=_cparams(("arbitrary", "arbitrary")),
    )(x, mod, w_in_bf16)


def _seg_mean_matrix(width, seg):
    r = lax.broadcasted_iota(jnp.int32, (width, width), 0) // seg
    c = lax.broadcasted_iota(jnp.int32, (width, width), 1) // seg
    return jnp.where(r == c, 1.0 / seg, 0.0).astype(bf16)


def _seg_sum_matrix(width, seg):
    r = lax.broadcasted_iota(jnp.int32, (width, width), 0) // seg
    c = lax.broadcasted_iota(jnp.int32, (width, width), 1) // seg
    return jnp.where(r == c, 1.0, 0.0).astype(bf16)


def _seg_reduce(x, m):
    hi = x.astype(bf16)
    lo = (x - hi.astype(f32)).astype(bf16)
    return _dot(hi, m) + _dot(lo, m)


def _sgu_kernel(x_ref, w_ref, bias_ref, g_ref, b_ref, *o_refs, n_chunks, emit_v):
    wd = x_ref.shape[2] // 2
    n_groups = wd // HEAD_DIM
    x = x_ref[0]
    u = _gelu(x[:, :wd])
    v = _gelu(x[:, wd:])
    m = _seg_mean_matrix(wd, HEAD_DIM)
    mu = _seg_reduce(v, m)
    c = v - mu
    var = _seg_reduce(c * c, m)
    vn = c * lax.rsqrt(var + LN_EPS) * g_ref[...] + b_ref[...]
    if emit_v:
        o_refs[1][0] = vn
    row = lax.broadcasted_iota(jnp.int32, (SGU_CHUNK, SGU_CHUNK), 0)
    col = lax.broadcasted_iota(jnp.int32, (SGU_CHUNK, SGU_CHUNK), 1)
    lane_grp = lax.broadcasted_iota(jnp.int32, (SGU_CHUNK, wd), 1) // HEAD_DIM
    w_tril = [jnp.where(col <= row, w_ref[g], 0.0).astype(bf16) for g in range(n_groups)]
    for ci in range(n_chunks):
        rows = slice(ci * SGU_CHUNK, (ci + 1) * SGU_CHUNK)
        vc = vn[rows]
        s = bias_ref[...]
        for g in range(n_groups):
            s = s + _dot(w_tril[g], jnp.where(lane_grp == g, vc, 0.0).astype(bf16))
        o_refs[0][0, rows, :] = u[rows] * s


def sgu(puv, w_s, bias_tile, ln_g, ln_b, rows_per_step, emit_v):
    g_, r_, w2 = puv.shape
    wd = w2 // 2
    assert r_ % rows_per_step == 0 and rows_per_step % SGU_CHUNK == 0
    n_out = 2 if emit_v else 1
    outs = pl.pallas_call(
        functools.partial(_sgu_kernel, n_chunks=rows_per_step // SGU_CHUNK, emit_v=emit_v),
        grid=(g_, r_ // rows_per_step),
        in_specs=[pl.BlockSpec((1, rows_per_step, w2), lambda i, j: (i, j, 0)),
                  pl.BlockSpec(w_s.shape, lambda i, j: (0, 0, 0)),
                  pl.BlockSpec(bias_tile.shape, lambda i, j: (0, 0)),
                  pl.BlockSpec((1, wd), lambda i, j: (0, 0)),
                  pl.BlockSpec((1, wd), lambda i, j: (0, 0))],
        out_specs=[pl.BlockSpec((1, rows_per_step, wd), lambda i, j: (i, j, 0))] * n_out,
        out_shape=[jax.ShapeDtypeStruct((g_, r_, wd), f32)] * n_out,
        name="sgu",
        compiler_params=_cparams(("arbitrary", "arbitrary")),
    )(puv, w_s, bias_tile, ln_g.reshape(1, wd), ln_b.reshape(1, wd))
    return outs


def _topk_rows(scores, n_valid):
    row = lax.broadcasted_iota(jnp.int32, scores.shape, 0)
    valid = row < n_valid
    sel = jnp.zeros(scores.shape, f32)
    for j in range(n_valid):
        sj = scores[j:j + 1, :]
        beats = jnp.logical_and(valid, jnp.logical_or(scores > sj, jnp.logical_and(scores == sj, row < j)))
        cnt = jnp.sum(beats.astype(f32), axis=0, keepdims=True)
        sel = jnp.where(jnp.logical_and(row == j, cnt < MOBA_TOPK), 1.0, sel)
    return sel


def _moba_full_kernel(q_ref, k_ref, v_ref, bias_ref, o_ref, k16_ref, vt16_ref):
    t_len = q_ref.shape[1]
    nb = t_len // MOBA_BLOCK
    blk = MOBA_BLOCK
    scale = HEAD_DIM ** -0.5
    k2 = k_ref[0]
    k16_ref[...] = k2.astype(bf16)
    vt16_ref[...] = v_ref[0].T.astype(bf16)
    kmean16 = jnp.mean(k2.reshape(nb, blk, LANES), axis=1).astype(bf16)
    lane = lax.broadcasted_iota(jnp.int32, (1, LANES), 1)
    hmask = [(lane // HEAD_DIM == hh).astype(f32) for hh in range(2)]
    krow = lax.broadcasted_iota(jnp.int32, (blk, blk), 0)
    qcol = lax.broadcasted_iota(jnp.int32, (blk, blk), 1)
    causal = krow <= qcol
    drow = lax.broadcasted_iota(jnp.int32, (LANES, blk), 0)

    for qb in range(nb):
        n_keys = (qb + 1) * blk
        q2 = q_ref[0, qb * blk:(qb + 1) * blk, :]
        k_all = k16_ref[0:n_keys, :]
        vt_all = vt16_ref[:, 0:n_keys]
        qm = [q2 * hmask[hh] for hh in range(2)]
        raw = [_dot_nt(k_all, (qm[hh] * scale).astype(bf16)) for hh in range(2)]
        ok = [causal, causal]
        if qb > 0:
            for hh in range(2):
                sel_t = _topk_rows(_dot_nt(kmean16, qm[hh].astype(bf16)), qb)
                past_ok = jnp.broadcast_to(sel_t[0:qb][:, None, :] > 0.5, (qb, blk, blk))
                ok[hh] = jnp.concatenate([past_ok.reshape(qb * blk, blk), causal], axis=0)
        probs, denom = [], []
        for hh in range(2):
            bias = bias_ref[hh, (nb - 1 - qb) * blk:nb * blk, :]
            logits = jnp.where(ok[hh], raw[hh] + bias, NEG_BIG)
            p = jnp.exp(logits - jnp.max(logits, axis=0, keepdims=True))
            denom.append(jnp.sum(p, axis=0, keepdims=True))
            probs.append(p.astype(bf16))
        outs = [_dot(vt_all, probs[hh]) for hh in range(2)]
        o_ref[0, qb * blk:(qb + 1) * blk, :] = jnp.where(drow < HEAD_DIM, outs[0] / denom[0],
                                                         outs[1] / denom[1]).T


def moba_full(q, k, v, bias_t):
    b, t, hw = q.shape
    n_pairs = hw // LANES
    qkv_spec = pl.BlockSpec((1, t, LANES), lambda p, i: (i, 0, p))
    return pl.pallas_call(
        _moba_full_kernel,
        grid=(n_pairs, b),
        in_specs=[qkv_spec, qkv_spec, qkv_spec,
                  pl.BlockSpec((2, t, MOBA_BLOCK), lambda p, i: (p, 0, 0))],
        out_specs=pl.BlockSpec((1, t, LANES), lambda p, i: (i, 0, p)),
        out_shape=jax.ShapeDtypeStruct((b, t, hw), f32),
        scratch_shapes=[pltpu.VMEM((t, LANES), bf16), pltpu.VMEM((LANES, t), bf16)],
        name="moba_full",
        compiler_params=_cparams(("arbitrary", "arbitrary")),
    )(q, k, v, bias_t)


PAGES_PER_STEP = 16


def _pair_queries(q2):
    lane = lax.broadcasted_iota(jnp.int32, (1, LANES), 1)
    return jnp.concatenate([q2 * (lane // HEAD_DIM == hh).astype(f32) for hh in range(2)], axis=0)


def _page_pair(page_ref, p):
    hd, page = page_ref.shape[3:]
    return page_ref[0, 0, 2 * p:2 * p + 2].reshape(2 * hd, page)


def _moba_paged_probs_kernel(pt_ref, q_ref, kn_ref, bias_ref, bown_ref, *refs, n_pairs):
    kp_refs = refs[:PAGES_PER_STEP]
    p_ref, pown_ref, logit_ref, bsum_ref = refs[PAGES_PER_STEP:]
    c = pl.program_id(1)
    n_steps = pl.num_programs(1)
    t_q = q_ref.shape[1]
    page = kp_refs[0].shape[4]
    pages_per_block = MOBA_BLOCK // page
    n_blocks = bsum_ref.shape[0]
    blocks_per_step = PAGES_PER_STEP // pages_per_block
    step_keys = PAGES_PER_STEP * page
    scale = HEAD_DIM ** -0.5

    qf = [_pair_queries(q_ref[0, :, p * LANES:(p + 1) * LANES]) for p in range(n_pairs)]
    q16 = [(x * scale).astype(bf16) for x in qf]
    for p in range(n_pairs):
        kts = [_page_pair(kp_refs[j], p) for j in range(PAGES_PER_STEP)]
        logit_ref[p, c] = _dot(q16[p], jnp.concatenate(kts, axis=1).astype(bf16))
        for jb in range(blocks_per_step):
            ksum = kts[jb * pages_per_block]
            for j in range(jb * pages_per_block + 1, (jb + 1) * pages_per_block):
                ksum = ksum + kts[j]
            bsum_ref[c * blocks_per_step + jb, p] = ksum

    @pl.when(c == n_steps - 1)
    def _():
        n_keys = n_steps * step_keys
        blk_of_key = lax.broadcasted_iota(jnp.int32, (n_blocks, n_keys), 1) // MOBA_BLOCK
        expand = (blk_of_key == lax.broadcasted_iota(jnp.int32, (n_blocks, n_keys), 0)).astype(bf16)
        rq = lax.broadcasted_iota(jnp.int32, (2 * t_q, LANES), 0) % t_q
        ck = lax.broadcasted_iota(jnp.int32, (2 * t_q, LANES), 1)
        own_ok = ck <= rq
        zpad = jnp.zeros((LANES - t_q, LANES), f32)
        blk_lane = lax.broadcasted_iota(jnp.int32, (LANES, LANES), 1)
        for p in range(n_pairs):
            lanes = slice(p * LANES, (p + 1) * LANES)
            kmean_t = jnp.zeros((LANES, LANES), f32)
            for blk in range(n_blocks):
                col = jnp.sum(bsum_ref[blk, p], axis=1, keepdims=True) * (1.0 / MOBA_BLOCK)
                kmean_t = jnp.where(blk_lane == blk, col, kmean_t)
            scores = _dot(qf[p].astype(bf16), kmean_t.astype(bf16))
            sel_t = _topk_rows(scores.T[0:n_blocks], n_blocks)
            chosen = _dot(sel_t.T.astype(bf16), expand)
            lg = [jnp.where(chosen[:, s * step_keys:(s + 1) * step_keys] > 0.5,
                            logit_ref[p, s] + bias_ref[p, s], NEG_BIG) for s in range(n_steps)]
            kn = jnp.concatenate([kn_ref[0, :, lanes], zpad], axis=0).astype(bf16)
            own = jnp.where(own_ok, _dot_nt(q16[p], kn) + bown_ref[p], NEG_BIG)
            m = jnp.max(own, axis=1, keepdims=True)
            for x in lg:
                m = jnp.maximum(m, jnp.max(x, axis=1, keepdims=True))
            e_own = jnp.exp(own - m)
            e = [jnp.exp(x - m) for x in lg]
            denom = jnp.sum(e_own, axis=1, keepdims=True)
            for x in e:
                denom = denom + jnp.sum(x, axis=1, keepdims=True)
            inv = 1.0 / denom
            for s in range(n_steps):
                p_ref[0, p, s] = e[s] * inv
            pown_ref[0, p] = e_own * inv


def _moba_paged_mix_kernel(pt_ref, p_ref, pown_ref, vn_ref, *refs, n_pairs):
    vp_refs = refs[:PAGES_PER_STEP]
    o_ref, acc_ref = refs[PAGES_PER_STEP:]
    c = pl.program_id(1)
    n_steps = pl.num_programs(1)
    t_q = vn_ref.shape[1]
    page = vp_refs[0].shape[4]

    @pl.when(c == 0)
    def _():
        acc_ref[...] = jnp.zeros_like(acc_ref)

    for p in range(n_pairs):
        vt = jnp.concatenate([_page_pair(vp_refs[j], p) for j in range(PAGES_PER_STEP)], axis=1)
        acc_ref[p] += _dot_nt(p_ref[0, p, 0].astype(bf16), vt.astype(bf16))

    @pl.when(c == n_steps - 1)
    def _():
        lane = lax.broadcasted_iota(jnp.int32, (1, LANES), 1)
        zpad = jnp.zeros((LANES - t_q, LANES), f32)
        for p in range(n_pairs):
            lanes = slice(p * LANES, (p + 1) * LANES)
            vn = jnp.concatenate([vn_ref[0, :, lanes], zpad], axis=0).astype(bf16)
            acc = acc_ref[p] + _dot(pown_ref[0, p].astype(bf16), vn)
            o_ref[0, :, lanes] = jnp.where(lane < HEAD_DIM, acc[0:t_q], acc[t_q:2 * t_q])


def moba_paged(q, k_new, v_new, cache_k, cache_v, layer, page_table, bias_past, bias_own):
    b, t_q, hw = q.shape
    n_pairs = hw // LANES
    n_pages = page_table.shape[1]
    n_heads, hd, page = cache_k.shape[2:]
    assert page == LANES and hd == HEAD_DIM and n_heads * hd == hw
    assert n_pages % PAGES_PER_STEP == 0 and PAGES_PER_STEP % (MOBA_BLOCK // page) == 0
    n_steps = n_pages // PAGES_PER_STEP
    step_keys = PAGES_PER_STEP * page
    rows = 2 * t_q

    def page_spec(j):
        return pl.BlockSpec((1, 1, n_heads, hd, page),
                            lambda i, c, pt: (layer, pt[i, c * PAGES_PER_STEP + j], 0, 0, 0))

    new_spec = pl.BlockSpec((1, t_q, hw), lambda i, c, pt: (i, 0, 0))
    own_spec = pl.BlockSpec((1, n_pairs, rows, LANES), lambda i, c, pt: (i, 0, 0, 0))
    probs, p_own = pl.pallas_call(
        functools.partial(_moba_paged_probs_kernel, n_pairs=n_pairs),
        grid_spec=pltpu.PrefetchScalarGridSpec(
            num_scalar_prefetch=1,
            grid=(b, n_steps),
            in_specs=[new_spec, new_spec,
                      pl.BlockSpec(bias_past.shape, lambda i, c, pt: (0, 0, 0, 0)),
                      pl.BlockSpec(bias_own.shape, lambda i, c, pt: (0, 0, 0))]
                     + [page_spec(j) for j in range(PAGES_PER_STEP)],
            out_specs=[pl.BlockSpec((1, n_pairs, n_steps, rows, step_keys), lambda i, c, pt: (i, 0, 0, 0, 0)),
                       own_spec],
            scratch_shapes=[pltpu.VMEM((n_pairs, n_steps, rows, step_keys), f32),
                            pltpu.VMEM((n_pages * page // MOBA_BLOCK, n_pairs, LANES, page), f32)]),
        out_shape=[jax.ShapeDtypeStruct((b, n_pairs, n_steps, rows, step_keys), f32),
                   jax.ShapeDtypeStruct((b, n_pairs, rows, LANES), f32)],
        name="moba_paged_probs",
        compiler_params=_cparams(("arbitrary", "arbitrary")),
    )(page_table, q, k_new, bias_past, bias_own, *([cache_k] * PAGES_PER_STEP))
    return pl.pallas_call(
        functools.partial(_moba_paged_mix_kernel, n_pairs=n_pairs),
        grid_spec=pltpu.PrefetchScalarGridSpec(
            num_scalar_prefetch=1,
            grid=(b, n_steps),
            in_specs=[pl.BlockSpec((1, n_pairs, 1, rows, step_keys), lambda i, c, pt: (i, 0, c, 0, 0)),
                      own_spec, new_spec]
                     + [page_spec(j) for j in range(PAGES_PER_STEP)],
            out_specs=new_spec,
            scratch_shapes=[pltpu.VMEM((n_pairs, rows, LANES), f32)]),
        out_shape=jax.ShapeDtypeStruct((b, t_q, hw), f32),
        name="moba_paged_mix",
        compiler_params=_cparams(("arbitrary", "arbitrary")),
    )(page_table, probs, p_own, v_new, *([cache_v] * PAGES_PER_STEP))


def paged_bias_layout(bias_rows_arr, n_past):
    h, t_q, _ = bias_rows_arr.shape
    step_keys = PAGES_PER_STEP * LANES
    n_steps = n_past // step_keys
    past = bias_rows_arr[:, :, :n_past].reshape(h // 2, 2 * t_q, n_steps, step_keys)
    own = bias_rows_arr[:, :, n_past:].reshape(h // 2, 2 * t_q, LANES)
    return past.transpose(0, 2, 1, 3), own


def _rwkv_prep_kernel(p_ref, prev_ref, shift_ref, mu_ref, w0_ref, a0_ref, kk_ref, ka_ref, rk_ref,
                      ww_ref, wa_ref, wg_ref, *o_refs, hw):
    bt, tt, n = p_ref.shape
    j = pl.program_id(1)
    p = p_ref[...].reshape(bt * tt, n)
    first = jnp.where(j == 0, shift_ref[...], prev_ref[:, 7:8, :])
    first = jnp.broadcast_to(first, (bt, tt, n)).reshape(bt * tt, n)
    row = lax.broadcasted_iota(jnp.int32, (bt * tt, 1), 0)
    prev = jnp.where(row % tt == 0, first, pltpu.roll(p, 1, 0))
    xs = p + (prev - p) * mu_ref[...]
    r = xs[:, 0:hw]
    k = xs[:, hw:2 * hw]
    v = xs[:, 2 * hw:3 * hw]
    xwa = xs[:, 3 * hw:3 * hw + LANES]
    xg = xs[:, 3 * hw + LANES:3 * hw + 2 * LANES]
    z = -(w0_ref[...] + _dot_f32(jnp.tanh(xwa), ww_ref[...]))
    softplus = jnp.maximum(z, 0.0) + jnp.log(1.0 + jnp.exp(-jnp.abs(z)))
    log_decay = -jnp.exp(-softplus - 0.5)
    a = _sigmoid(a0_ref[...] + _dot_f32(xwa, wa_ref[...]))
    g = _dot_f32(_sigmoid(xg), wg_ref[...])
    mseg = _seg_sum_matrix(hw, HEAD_DIM)
    kk = k * kk_ref[...]
    kk = kk / jnp.maximum(jnp.sqrt(_seg_reduce(kk * kk, mseg)), 1e-12)
    k2 = k * (1.0 + (a - 1.0) * ka_ref[...])
    bonus = _seg_reduce(r * k2 * rk_ref[...], mseg) * v
    outs = (r, log_decay, k2, v, -kk, kk * a, g, bonus)
    for o_ref, val in zip(o_refs, outs):
        o_ref[...] = val.reshape(bt, tt, hw)


def rwkv_prep(prw, shift0, prm, bt, tt):
    b, t, n = prw.shape
    hw = prm['w0'].shape[-1]
    assert n == 3 * hw + 2 * LANES and tt % 8 == 0
    row = lambda x: x.reshape(1, -1)
    vec = lambda: pl.BlockSpec((1, hw), lambda i, j: (0, 0))
    mat = lambda: pl.BlockSpec((LANES, hw), lambda i, j: (0, 0))
    prev_blk = tt // 8
    return pl.pallas_call(
        functools.partial(_rwkv_prep_kernel, hw=hw),
        grid=(b // bt, t // tt),
        in_specs=[pl.BlockSpec((bt, tt, n), lambda i, j: (i, j, 0)),
                  pl.BlockSpec((bt, 8, n), lambda i, j: (i, jnp.maximum(j * prev_blk - 1, 0), 0)),
                  pl.BlockSpec((bt, 1, n), lambda i, j: (i, 0, 0)),
                  pl.BlockSpec((1, n), lambda i, j: (0, 0)),
                  vec(), vec(), vec(), vec(), vec(), mat(), mat(), mat()],
        out_specs=[pl.BlockSpec((bt, tt, hw), lambda i, j: (i, j, 0))] * 8,
        out_shape=[jax.ShapeDtypeStruct((b, t, hw), f32)] * 8,
        name="rwkv_prep",
        compiler_params=_cparams(("arbitrary", "arbitrary")),
    )(prw, prw, shift0.reshape(b, 1, n), row(prm['mu']), row(prm['w0']), row(prm['a0']),
      row(prm['k_k']), row(prm['k_a']), row(prm['r_k']), prm['w_w2p'], prm['w_a2p'], prm['w_g2'])


def rwkv_prep_params(mu, w0, w_w2, a0, w_a2, w_g2, k_k, k_a, r_k):
    hw = w0.shape[-1]
    zeros = jnp.zeros((LANES - DECAY_LORA, hw), w_w2.dtype)
    return dict(mu=mu, w0=w0, a0=a0, k_k=k_k, k_a=k_a, r_k=r_k.reshape(-1),
                w_w2p=jnp.concatenate([w_w2, zeros], axis=0),
                w_a2p=jnp.concatenate([zeros, w_a2], axis=0), w_g2=w_g2)


def _split3(x):
    hi = x.astype(bf16)
    return hi, (x - hi.astype(f32)).astype(bf16)


def _dot3(a, b):
    ah, al = _split3(a)
    bh, bl = _split3(b)
    return _dot(ah, bh) + (_dot(ah, bl) + _dot(al, bh))


def _dot3_nt(a, b):
    ah, al = _split3(a)
    bh, bl = _split3(b)
    return _dot_nt(ah, bh) + (_dot_nt(ah, bl) + _dot_nt(al, bh))


def _dot3_tn(a, b):
    return _dot3(a.T, b)


def _rwkv_chunk_maps_kernel(r_ref, lw_ref, k_ref, v_ref, x_ref, b_ref, m_ref, n_ref, p_ref, q_ref,
                            *, n_pairs, c_len):
    two_c = 2 * c_len
    rr = lax.broadcasted_iota(jnp.int32, (two_c, two_c), 0)
    cc = lax.broadcasted_iota(jnp.int32, (two_c, two_c), 1)
    same_head = (rr // c_len) == (cc // c_len)
    strict = jnp.logical_and(same_head, (cc % c_len) < (rr % c_len))
    incl = jnp.logical_and(same_head, (cc % c_len) <= (rr % c_len))
    eye = (rr == cc).astype(f32)
    row_head = lax.broadcasted_iota(jnp.int32, (two_c, LANES), 0) // c_len
    lane_head = lax.broadcasted_iota(jnp.int32, (two_c, LANES), 1) // HEAD_DIM
    bd = row_head == lane_head
    eye_k = (lax.broadcasted_iota(jnp.int32, (LANES, LANES), 0)
             == lax.broadcasted_iota(jnp.int32, (LANES, LANES), 1)).astype(f32)
    trow = lax.broadcasted_iota(jnp.int32, (c_len, LANES), 0)
    n_doubling = max(1, (c_len - 1).bit_length())

    def stack(a):
        return jnp.where(bd, jnp.concatenate([a, a], axis=0), 0.0)

    jobs = [(ci, p) for ci in range(r_ref.shape[1] // c_len) for p in range(n_pairs)]
    ops = []
    for ci, p in jobs:
        rows = slice(ci * c_len, (ci + 1) * c_len)
        lanes = slice(p * LANES, (p + 1) * LANES)
        lw = lw_ref[0, rows, lanes]
        cum = lw
        sh = 1
        while sh < c_len:
            cum = cum + jnp.where(trow >= sh, pltpu.roll(cum, sh, 0), 0.0)
            sh *= 2
        gam = jnp.exp(cum)
        inv_gam = jnp.exp(-cum)
        ops.append(dict(
            a=stack(x_ref[0, rows, lanes] * jnp.exp(cum - lw)), r=stack(r_ref[0, rows, lanes] * gam),
            b=stack(b_ref[0, rows, lanes] * inv_gam), k=stack(k_ref[0, rows, lanes] * inv_gam),
            v=stack(v_ref[0, rows, lanes]), g_end=gam[c_len - 1:c_len, :]))
    for o in ops:
        bk = jnp.concatenate([o['b'], o['k']], axis=0)
        gram_a = _dot3_nt(o['a'], bk)
        gram_r = _dot_nt_f32(o['r'], bk)
        o['l_ab'] = jnp.where(strict, gram_a[:, 0:two_c], 0.0)
        o['l_ak'] = jnp.where(strict, gram_a[:, two_c:2 * two_c], 0.0)
        o['m_rb'] = jnp.where(incl, gram_r[:, 0:two_c], 0.0)
        o['m_rk'] = jnp.where(incl, gram_r[:, two_c:2 * two_c], 0.0)
        o['pw'] = o['l_ab']
        o['u'] = o['l_ab']
    for it in range(n_doubling - 1):
        mm = _dot3 if it == 0 else _dot_f32
        for o in ops:
            o['pw'] = mm(o['pw'], o['pw'])
        for o in ops:
            o['u'] = o['u'] + o['pw'] + mm(o['u'], o['pw'])
    for o in ops:
        o['lkv'] = _dot3(o['l_ak'], o['v'])
    for o in ops:
        o['pq1'] = _dot3(eye + o['u'], jnp.concatenate([o['a'], o['lkv']], axis=1))
    for o in ops:
        o['pq2'] = _dot_f32(o['m_rb'], o['pq1'])
        o['mkv'] = _dot_f32(o['m_rk'], o['v'])
    for o in ops:
        o['mn'] = _dot3_tn(o['pq1'], o['b'] * o['g_end'])
        o['vk'] = _dot3_tn(o['v'], o['k'] * o['g_end'])
    for (ci, p), o in zip(jobs, ops):
        p_ref[0, ci, p] = o['r'] + o['pq2'][:, 0:LANES]
        q_ref[0, ci, p] = o['pq2'][:, LANES:2 * LANES] + o['mkv']
        m_ref[0, ci, p] = eye_k * o['g_end'] + o['mn'][0:LANES]
        n_ref[0, ci, p] = o['mn'][LANES:2 * LANES] + o['vk']


def _rwkv_state_kernel(m_ref, n_ref, p_ref, q_ref, g_ref, bonus_ref, s0_ref, lng_ref, lnb_ref,
                       y_ref, sout_ref, state_ref, *, n_pairs, c_len):
    jc = pl.program_id(1)
    n_steps = pl.num_programs(1)

    @pl.when(jc == 0)
    def _():
        state_ref[...] = s0_ref[0]

    mseg = _seg_mean_matrix(LANES, HEAD_DIM)
    states = [state_ref[p] for p in range(n_pairs)]
    for ci in range(m_ref.shape[1]):
        rows = slice(ci * c_len, (ci + 1) * c_len)
        outs = [_dot3_nt(p_ref[0, ci, p], states[p]) + q_ref[0, ci, p] for p in range(n_pairs)]
        states = [_dot3(states[p], m_ref[0, ci, p]) + n_ref[0, ci, p] for p in range(n_pairs)]
        for p in range(n_pairs):
            lanes = slice(p * LANES, (p + 1) * LANES)
            o = outs[p][0:c_len] + outs[p][c_len:2 * c_len]
            mu = _seg_reduce(o, mseg)
            cen = o - mu
            var = _seg_reduce(cen * cen, mseg)
            y = cen * lax.rsqrt(var + GN_EPS) * lng_ref[:, lanes] + lnb_ref[:, lanes]
            y_ref[0, rows, lanes] = (y + bonus_ref[0, rows, lanes]) * g_ref[0, rows, lanes]
    for p in range(n_pairs):
        state_ref[p] = states[p]

    @pl.when(jc == n_steps - 1)
    def _():
        sout_ref[0] = state_ref[...]


def rwkv_chunked(seq, s0_bd, lnx_g, lnx_b, c_len, maps_chunks, scan_chunks):
    r, lw, k2, v, x, kka, g, bonus = seq
    b, t, hw = r.shape
    n_pairs = hw // LANES
    n_chunks = t // c_len
    assert t % c_len == 0 and n_chunks % maps_chunks == 0 and n_chunks % scan_chunks == 0
    two_c = 2 * c_len

    def map_spec(rows, per_step):
        return pl.BlockSpec((1, per_step, n_pairs, rows, LANES), lambda i, j: (i, j, 0, 0, 0))

    def map_shape(rows):
        return jax.ShapeDtypeStruct((b, n_chunks, n_pairs, rows, LANES), f32)

    seq_spec = lambda per_step: pl.BlockSpec((1, per_step * c_len, hw), lambda i, j: (i, j, 0))
    maps = pl.pallas_call(
        functools.partial(_rwkv_chunk_maps_kernel, n_pairs=n_pairs, c_len=c_len),
        grid=(b, n_chunks // maps_chunks),
        in_specs=[seq_spec(maps_chunks)] * 6,
        out_specs=[map_spec(LANES, maps_chunks), map_spec(LANES, maps_chunks),
                   map_spec(two_c, maps_chunks), map_spec(two_c, maps_chunks)],
        out_shape=[map_shape(LANES), map_shape(LANES), map_shape(two_c), map_shape(two_c)],
        name="rwkv_chunk_maps",
        compiler_params=_cparams(("arbitrary", "arbitrary")),
    )(r, lw, k2, v, x, kka)
    st_spec = pl.BlockSpec((1, n_pairs, LANES, LANES), lambda i, j: (i, 0, 0, 0))
    vec = pl.BlockSpec((1, hw), lambda i, j: (0, 0))
    return pl.pallas_call(
        functools.partial(_rwkv_state_kernel, n_pairs=n_pairs, c_len=c_len),
        grid=(b, n_chunks // scan_chunks),
        in_specs=[map_spec(LANES, scan_chunks), map_spec(LANES, scan_chunks),
                  map_spec(two_c, scan_chunks), map_spec(two_c, scan_chunks),
                  seq_spec(scan_chunks), seq_spec(scan_chunks), st_spec, vec, vec],
        out_specs=[seq_spec(scan_chunks), st_spec],
        out_shape=[jax.ShapeDtypeStruct((b, t, hw), f32),
                   jax.ShapeDtypeStruct((b, n_pairs, LANES, LANES), f32)],
        scratch_shapes=[pltpu.VMEM((n_pairs, LANES, LANES), f32)],
        name="rwkv_state_scan",
        compiler_params=_cparams(("arbitrary", "arbitrary")),
    )(*maps, g, bonus, s0_bd, lnx_g.reshape(1, hw), lnx_b.reshape(1, hw))


def state_to_blockdiag(s):
    b, h, n, _ = s.shape
    sp = s.reshape(b, h // 2, 2, n, n)
    z = jnp.zeros_like(sp[:, :, 0])
    top = jnp.concatenate([sp[:, :, 0], z], axis=-1)
    bot = jnp.concatenate([z, sp[:, :, 1]], axis=-1)
    return jnp.concatenate([top, bot], axis=-2)


def state_from_blockdiag(sbd):
    b, hp, _, _ = sbd.shape
    n = HEAD_DIM
    return jnp.stack([sbd[:, :, :n, :n], sbd[:, :, n:, n:]], axis=2).reshape(b, hp * 2, n, n)


def _layer_norm(z, g, b):
    mu = jnp.mean(z, axis=-1, keepdims=True)
    c = z - mu
    var = jnp.mean(c * c, axis=-1, keepdims=True)
    return c * lax.rsqrt(var + LN_EPS) * g + b


def _route(sel, aff, n_groups, per_group, top_k):
    rows = sel.shape[1]
    srow = [sel[e:e + 1, :] for e in range(n_groups * per_group)]
    arow = [aff[e:e + 1, :] for e in range(n_groups * per_group)]
    scores = []
    for g in range(n_groups):
        s = srow[g * per_group:(g + 1) * per_group]
        best = None
        for i in range(per_group):
            for j in range(i + 1, per_group):
                pair = s[i] + s[j]
                best = pair if best is None else jnp.maximum(best, pair)
        scores.append(best)
    top = scores[0]
    for g in range(1, n_groups):
        top = jnp.maximum(top, scores[g])
    taken = jnp.zeros((1, rows), jnp.bool_)
    eidx = lax.broadcasted_iota(jnp.int32, (n_groups * per_group, rows), 0)
    gates = jnp.zeros((n_groups * per_group, rows), f32)
    for g in range(n_groups):
        is_grp = jnp.logical_and(scores[g] == top, jnp.logical_not(taken))
        taken = jnp.logical_or(taken, is_grp)
        s = srow[g * per_group:(g + 1) * per_group]
        a = arow[g * per_group:(g + 1) * per_group]
        chosen = []
        for i in range(per_group):
            cnt = jnp.zeros((1, rows), f32)
            for j in range(per_group):
                if j != i:
                    ahead = (s[j] > s[i]) if j > i else (s[j] >= s[i])
                    cnt = cnt + ahead.astype(f32)
            chosen.append(jnp.logical_and(cnt < top_k, is_grp))
        denom = jnp.zeros((1, rows), f32)
        for i in range(per_group):
            denom = denom + jnp.where(chosen[i], a[i], 0.0)
        for i in range(per_group):
            gate = jnp.where(chosen[i], a[i] / denom, 0.0)
            gates = jnp.where(eidx == g * per_group + i, gate, gates)
    return gates


def _outproj_kernel(x_ref, ya_ref, yb_ref, yc_ref, mod_ref, w_ref, g_ref, b_ref, wr_ref, rb_ref,
                    x1_ref, h2_ref, gates_ref, *, d, alpha):
    bt, tt, _ = x_ref.shape
    rows = bt * tt
    off = 0
    mix = jnp.zeros((rows, d), f32)
    for y_ref in (ya_ref, yb_ref, yc_ref):
        wd = y_ref.shape[2]
        mix = mix + _dot(y_ref[...].reshape(rows, wd).astype(bf16), w_ref[off:off + wd, :])
        off += wd
    g1 = mod_ref[:, :, 2 * d:3 * d]
    z = alpha * x_ref[...] + (1.0 + g1) * mix.reshape(bt, tt, d)
    x1 = _layer_norm(z, g_ref[...], b_ref[...])
    x1_ref[...] = x1
    sh2 = mod_ref[:, :, 3 * d:4 * d]
    sc2 = mod_ref[:, :, 4 * d:5 * d]
    h2 = (x1 * (1.0 + sc2) + sh2).reshape(rows, d)
    h2_ref[...] = h2.astype(bf16).reshape(bt, tt, d)
    aff = _sigmoid(_dot_nt_f32(wr_ref[...], h2))
    gates_ref[...] = _route(aff + rb_ref[...], aff, N_EXPERT_GROUPS,
                            wr_ref.shape[0] // N_EXPERT_GROUPS, TOP_K_EXPERTS)


def out_projection(x, ya, yb, yc, mod, w_out_bf16, ln_g, ln_b, w_router_t, router_bias, alpha, bt, tt):
    b, t, d = x.shape
    n_exp = w_router_t.shape[0]
    rows = bt * tt
    blk = lambda wd: pl.BlockSpec((bt, tt, wd), lambda i, j: (i, j, 0))
    full = lambda a: pl.BlockSpec(a.shape, lambda i, j: (0,) * a.ndim)
    n_tblk = t // tt
    ln_g = ln_g.reshape(1, d)
    ln_b = ln_b.reshape(1, d)
    rb = router_bias.reshape(n_exp, 1)
    return pl.pallas_call(
        functools.partial(_outproj_kernel, d=d, alpha=alpha),
        grid=(b // bt, n_tblk),
        in_specs=[blk(d), blk(ya.shape[2]), blk(yb.shape[2]), blk(yc.shape[2]),
                  pl.BlockSpec((bt, 1, mod.shape[2]), lambda i, j: (i, 0, 0)),
                  full(w_out_bf16), full(ln_g), full(ln_b), full(w_router_t), full(rb)],
        out_specs=[blk(d), blk(d), pl.BlockSpec((n_exp, rows), lambda i, j: (0, i * n_tblk + j))],
        out_shape=[jax.ShapeDtypeStruct((b, t, d), f32), jax.ShapeDtypeStruct((b, t, d), bf16),
                   jax.ShapeDtypeStruct((n_exp, b * t), f32)],
        name="out_projection",
        compiler_params=_cparams(("arbitrary", "arbitrary")),
    )(x, ya, yb, yc, mod, w_out_bf16, ln_g, ln_b, w_router_t, rb)


def _moe_kernel(h_ref, gates_ref, x1_ref, mod_ref, wg_ref, wu_ref, wd_ref, g_ref, b_ref,
                o_ref, acc_ref, *, d, alpha):
    bt, tt, _ = h_ref.shape
    rows = bt * tt
    per_step, f, _ = wd_ref.shape
    eg = pl.program_id(2)
    n_steps = pl.num_programs(2)

    @pl.when(eg == 0)
    def _():
        acc_ref[...] = jnp.zeros_like(acc_ref)

    h = h_ref[...].reshape(rows, d)
    gates = gates_ref[...].reshape(rows, gates_ref.shape[2])
    lane = lax.broadcasted_iota(jnp.int32, gates.shape, 1)
    acts = []
    for i in range(per_step):
        gate = jnp.sum(jnp.where(lane == eg * per_step + i, gates, 0.0), axis=1, keepdims=True)
        acts.append((_silu(_dot(h, wg_ref[i])) * _dot(h, wu_ref[i]) * gate).astype(bf16))
    acc_ref[...] += _dot(jnp.concatenate(acts, axis=1), wd_ref[...].reshape(per_step * f, d))

    @pl.when(eg == n_steps - 1)
    def _():
        g2 = mod_ref[:, :, 5 * d:6 * d]
        z = alpha * x1_ref[...] + (1.0 + g2) * acc_ref[...].reshape(bt, tt, d)
        o_ref[...] = _layer_norm(z, g_ref[...], b_ref[...])


def moe_ffn(h2, gates, x1, mod, wg_bf16, wu_bf16, wd_bf16, ln_g, ln_b, alpha, bt, tt):
    b, t, d = x1.shape
    n_exp = wg_bf16.shape[0]
    f = wg_bf16.shape[2]
    blk = lambda wd: pl.BlockSpec((bt, tt, wd), lambda i, j, e: (i, j, 0))
    per_step = n_exp // N_EXPERT_GROUPS
    return pl.pallas_call(
        functools.partial(_moe_kernel, d=d, alpha=alpha),
        grid=(b // bt, t // tt, n_exp // per_step),
        in_specs=[blk(d), blk(n_exp), blk(d),
                  pl.BlockSpec((bt, 1, mod.shape[2]), lambda i, j, e: (i, 0, 0)),
                  pl.BlockSpec((per_step, d, f), lambda i, j, e: (e, 0, 0)),
                  pl.BlockSpec((per_step, d, f), lambda i, j, e: (e, 0, 0)),
                  pl.BlockSpec((per_step, f, d), lambda i, j, e: (e, 0, 0)),
                  pl.BlockSpec((1, d), lambda i, j, e: (0, 0)),
                  pl.BlockSpec((1, d), lambda i, j, e: (0, 0))],
        out_specs=blk(d),
        out_shape=jax.ShapeDtypeStruct((b, t, d), f32),
        scratch_shapes=[pltpu.VMEM((bt * tt, d), f32)],
        name="moe_ffn",
        compiler_params=_cparams(("arbitrary", "arbitrary", "arbitrary")),
    )(h2, gates, x1, mod, wg_bf16, wu_bf16, wd_bf16, ln_g.reshape(1, d), ln_b.reshape(1, d))


def sgu_full_params(w_s, b_s):
    return w_s, jnp.repeat(b_s.T, HEAD_DIM, axis=1)


def sgu_short_params(w_s, b_s, t_len):
    reps = SGU_CHUNK // t_len
    eye = jnp.eye(reps, dtype=w_s.dtype)
    w_small = w_s[:, :t_len, :t_len]
    w_big = jnp.einsum('ab,gts->gatbs', eye, w_small).reshape(w_s.shape[0], SGU_CHUNK, SGU_CHUNK)
    bias = jnp.tile(jnp.repeat(b_s[:, :t_len].T, HEAD_DIM, axis=1), (reps, 1))
    return w_big, bias


PROMPT_ROWS = 512
RWKV_CHUNK = (64, 2, 4)


def _hybrid_layer(x, mod, tile, wts, attend, sgu_prm, shift0, s0_bd, rwkv_chunk, emit_sgu_v, alpha):
    bt, tt = tile
    b, t, d = x.shape
    hw_sgu = sgu_prm[0].shape[0] * HEAD_DIM
    hw_attn = wts['attn_width']
    widths = (2 * hw_sgu, hw_attn, hw_attn, hw_attn, wts['w_in'].shape[1] - 2 * hw_sgu - 3 * hw_attn)
    puv, q, k, v, prw = in_projection(x, mod, wts['w_in'], widths, bt, tt)
    n_rows = b * t
    sgu_rows = min(PROMPT_ROWS, n_rows) if t < SGU_CHUNK else tt
    sgu_in = puv.reshape(1, n_rows, 2 * hw_sgu) if t < SGU_CHUNK else puv
    sgu_out = sgu(sgu_in, sgu_prm[0], sgu_prm[1], wts['sgu_ln_g'], wts['sgu_ln_b'], sgu_rows, emit_sgu_v)
    ya = sgu_out[0].reshape(b, t, hw_sgu)
    sgu_v = sgu_out[1].reshape(b, t, hw_sgu) if emit_sgu_v else None
    yb = attend(q, k, v)
    seq = rwkv_prep(prw, shift0, wts['rwkv'], bt, tt)
    yc, s_bd = rwkv_chunked(seq, s0_bd, wts['lnx_g'], wts['lnx_b'], *rwkv_chunk)
    x1, h2, gates_t = out_projection(x, ya, yb, yc, mod, wts['w_out'], wts['ln1_g'], wts['ln1_b'],
                                     wts['w_router_t'], wts['router_bias'], alpha, bt, tt)
    gates = gates_t.T.reshape(b, t, gates_t.shape[0])
    x2 = moe_ffn(h2, gates, x1, mod, wts['moe_wg'], wts['moe_wu'], wts['moe_wd'],
                 wts['ln2_g'], wts['ln2_b'], alpha, bt, tt)
    return x2, k, v, state_from_blockdiag(s_bd), prw[:, -1], sgu_v


def kernel(x_prompt, x_sample, cache_k, cache_v, state_rwkv, state_shift, page_table, c_prompt, c_sample, w_ada, b_ada, w_in, w_out, sgu_w, sgu_b, sgu_ln_g, sgu_ln_b, rel_bias, rwkv_mu, rwkv_w0, rwkv_w_w2, rwkv_a0, rwkv_w_a2, rwkv_w_g2, rwkv_k_k, rwkv_k_a, rwkv_r_k, rwkv_lnx_g, rwkv_lnx_b, ln1_g, ln1_b, ln2_g, ln2_b, w_router, router_bias, moe_w_gate, moe_w_up, moe_w_down):
    depth = w_ada.shape[0]
    bp, t_p, d = x_prompt.shape
    bs, t_s, _ = x_sample.shape
    n_heads = cache_k.shape[3]
    hw_attn = n_heads * HEAD_DIM
    n_pool, page = cache_k.shape[1], cache_k.shape[2]
    past_len = page_table.shape[1] * page
    rwkv_in = state_shift.shape[2]
    alpha = (2 * depth) ** 0.25

    mods = ada_modulation(jnp.concatenate([c_prompt, c_sample], axis=0), w_ada, b_ada)
    bias_p = bias_tiles(rel_bias, n_heads, t_p // MOBA_BLOCK).reshape(n_heads, t_p, MOBA_BLOCK)
    bias_past, bias_own = paged_bias_layout(
        bias_rows(rel_bias, n_heads, past_len, t_s, past_len + LANES), past_len)

    def attend_paged(layer):
        def attend(q, k, v):
            return moba_paged(q, k, v, cache_kt, cache_vt, layer, page_table, bias_past, bias_own)
        return attend

    cache_kt = cache_k.transpose(0, 1, 3, 4, 2)
    cache_vt = cache_v.transpose(0, 1, 3, 4, 2)
    w_in16, w_out16 = w_in.astype(bf16), w_out.astype(bf16)
    wg16, wu16, wd16 = moe_w_gate.astype(bf16), moe_w_up.astype(bf16), moe_w_down.astype(bf16)
    w_router_t = w_router.T
    zero_shift = jnp.zeros((bp, rwkv_in), f32)
    zero_state = jnp.zeros((bp, n_heads // 2, LANES, LANES), f32)

    xp, xs = x_prompt, x_sample
    outs = [[] for _ in range(9)]
    for l in range(depth):
        wts = dict(
            attn_width=hw_attn, w_in=w_in16[l], w_out=w_out16[l], sgu_ln_g=sgu_ln_g[l], sgu_ln_b=sgu_ln_b[l],
            rwkv=rwkv_prep_params(rwkv_mu[l], rwkv_w0[l], rwkv_w_w2[l], rwkv_a0[l], rwkv_w_a2[l],
                                  rwkv_w_g2[l], rwkv_k_k[l], rwkv_k_a[l], rwkv_r_k[l]),
            lnx_g=rwkv_lnx_g[l], lnx_b=rwkv_lnx_b[l], ln1_g=ln1_g[l], ln1_b=ln1_b[l],
            ln2_g=ln2_g[l], ln2_b=ln2_b[l], w_router_t=w_router_t, router_bias=router_bias,
            moe_wg=wg16[l], moe_wu=wu16[l], moe_wd=wd16[l])
        mod_p = mods[l, :bp][:, None, :]
        mod_s = mods[l, bp:][:, None, :]
        xp, kp, vp, sp, shp, _ = _hybrid_layer(
            xp, mod_p, (1, PROMPT_ROWS), wts, lambda q, k, v: moba_full(q, k, v, bias_p),
            sgu_full_params(sgu_w[l], sgu_b[l]), zero_shift, zero_state, RWKV_CHUNK, False, alpha)
        xs, ks_, vs_, ss, shs, sgu_v = _hybrid_layer(
            xs, mod_s, (bs, t_s), wts, attend_paged(l),
            sgu_short_params(sgu_w[l], sgu_b[l], t_s), state_shift[l],
            state_to_blockdiag(state_rwkv[l]), (t_s, 1, 1), True, alpha)
        heads = lambda a: a.reshape(a.shape[0], a.shape[1], n_heads, HEAD_DIM)
        for lst, val in zip(outs, (heads(kp), heads(vp), heads(ks_), heads(vs_), sp, ss, shp, shs, sgu_v)):
            lst.append(val)
    return (xp, xs) + tuple(jnp.stack(lst) for lst in outs)
```

```python
import functools
import math

import jax
import jax.numpy as jnp
import numpy as np
from jax import lax
from jax.experimental import pallas as pl
from jax.experimental.pallas import tpu as pltpu

HEAD_DIM = 64
LANES = 128
SGU_CHUNK = 128
MOBA_BLOCK = 256
MOBA_TOPK = 3
REL_BUCKETS = 32
REL_MAX_DIST = 1024
DECAY_LORA = 64
AAA_LORA = 64
GATE_LORA = 128
GN_EPS = 64e-5
LN_EPS = 1e-5
N_EXPERT_GROUPS = 4
TOP_K_EXPERTS = 2
VMEM_LIMIT = 56 * 1024 * 1024
NEG_BIG = -1e30

f32 = jnp.float32
bf16 = jnp.bfloat16


def _cparams(sem):
    return pltpu.CompilerParams(dimension_semantics=sem, vmem_limit_bytes=VMEM_LIMIT)


def _dot(a, b):
    return jnp.dot(a, b, preferred_element_type=f32)


def _dot_nt(a, b):
    return lax.dot_general(a, b, (((1,), (1,)), ((), ())), preferred_element_type=f32)


def _dot_f32(a, b):
    return _dot(a.astype(bf16), b.astype(bf16))


def _dot_nt_f32(a, b):
    return _dot_nt(a.astype(bf16), b.astype(bf16))


def _gelu(x):
    c = math.sqrt(2.0 / math.pi)
    return 0.5 * x * (1.0 + jnp.tanh(c * (x + 0.044715 * (x * x * x))))


def _sigmoid(x):
    return 1.0 / (1.0 + jnp.exp(-x))


def _silu(x):
    return x * _sigmoid(x)


def _t5_bias(dist, rel_ref, head):
    max_exact = REL_BUCKETS // 2
    d = jnp.maximum(dist, 0)
    df = jnp.maximum(d, 1).astype(f32)
    large = max_exact + (jnp.log(df / max_exact) / math.log(REL_MAX_DIST / max_exact)
                         * (REL_BUCKETS - max_exact)).astype(jnp.int32)
    large = jnp.minimum(large, REL_BUCKETS - 1)
    bucket = jnp.where(d < max_exact, d, large)
    out = jnp.zeros(dist.shape, f32)
    for r in range(REL_BUCKETS):
        out = jnp.where(bucket == r, rel_ref[r, head], out)
    return out


def _bias_tiles_kernel(rel_ref, o_ref):
    h = pl.program_id(0)
    delta = pl.num_programs(1) - 1 - pl.program_id(1)
    i = lax.broadcasted_iota(jnp.int32, (MOBA_BLOCK, MOBA_BLOCK), 0)
    j = lax.broadcasted_iota(jnp.int32, (MOBA_BLOCK, MOBA_BLOCK), 1)
    dist = delta * MOBA_BLOCK + j - i
    o_ref[0, 0] = _t5_bias(dist, rel_ref, h)


def bias_tiles(rel_bias, n_heads, n_blocks):
    return pl.pallas_call(
        _bias_tiles_kernel,
        grid=(n_heads, n_blocks),
        in_specs=[pl.BlockSpec(memory_space=pltpu.SMEM)],
        out_specs=pl.BlockSpec((1, 1, MOBA_BLOCK, MOBA_BLOCK), lambda h, d: (h, d, 0, 0)),
        out_shape=jax.ShapeDtypeStruct((n_heads, n_blocks, MOBA_BLOCK, MOBA_BLOCK), f32),
        name="bias_tiles",
        compiler_params=_cparams(("arbitrary", "arbitrary")),
    )(rel_bias)


def _bias_rows_kernel(rel_ref, o_ref, *, q_start, t_q):
    h = pl.program_id(0)
    n_keys = o_ref.shape[2]
    i = lax.broadcasted_iota(jnp.int32, (t_q, n_keys), 0)
    j = lax.broadcasted_iota(jnp.int32, (t_q, n_keys), 1)
    o_ref[0] = _t5_bias(q_start + i - j, rel_ref, h)


def bias_rows(rel_bias, n_heads, q_start, t_q, n_keys):
    return pl.pallas_call(
        functools.partial(_bias_rows_kernel, q_start=q_start, t_q=t_q),
        grid=(n_heads,),
        in_specs=[pl.BlockSpec(memory_space=pltpu.SMEM)],
        out_specs=pl.BlockSpec((1, t_q, n_keys), lambda h: (h, 0, 0)),
        out_shape=jax.ShapeDtypeStruct((n_heads, t_q, n_keys), f32),
        name="bias_rows",
        compiler_params=_cparams(("arbitrary",)),
    )(rel_bias)


def _ada_kernel(c_ref, w_ref, b_ref, o_ref):
    c = _silu(c_ref[...]).astype(bf16)
    o_ref[0] = _dot(c, w_ref[0].astype(bf16)) + b_ref[0]


def ada_modulation(c_all, w_ada, b_ada, tn=1536):
    depth, d, n = w_ada.shape
    rows = c_all.shape[0]
    return pl.pallas_call(
        _ada_kernel,
        grid=(depth, n // tn),
        in_specs=[pl.BlockSpec((rows, d), lambda l, j: (0, 0)),
                  pl.BlockSpec((1, d, tn), lambda l, j: (l, 0, j)),
                  pl.BlockSpec((1, 1, tn), lambda l, j: (l, 0, j))],
        out_specs=pl.BlockSpec((1, rows, tn), lambda l, j: (l, 0, j)),
        out_shape=jax.ShapeDtypeStruct((depth, rows, n), f32),
        name="ada_modulation",
        compiler_params=_cparams(("arbitrary", "arbitrary")),
    )(c_all, w_ada, b_ada.reshape(depth, 1, n))


def _inproj_kernel(x_ref, mod_ref, w_ref, *o_refs, d, widths):
    bt, tt, _ = x_ref.shape
    sh = mod_ref[:, :, 0:d]
    sc = mod_ref[:, :, d:2 * d]
    h = (x_ref[...] * (1.0 + sc) + sh).reshape(bt * tt, d).astype(bf16)
    off = 0
    for o_ref, wd in zip(o_refs, widths):
        o_ref[...] = _dot(h, w_ref[:, off:off + wd]).reshape(bt, tt, wd)
        off += wd


def in_projection(x, mod, w_in_bf16, widths, bt, tt):
    b, t, d = x.shape
    n = w_in_bf16.shape[1]
    assert sum(widths) == n and b % bt == 0 and t % tt == 0
    return pl.pallas_call(
        functools.partial(_inproj_kernel, d=d, widths=widths),
        grid=(b // bt, t // tt),
        in_specs=[pl.BlockSpec((bt, tt, d), lambda i, j: (i, j, 0)),
                  pl.BlockSpec((bt, 1, mod.shape[2]), lambda i, j: (i, 0, 0)),
                  pl.BlockSpec((d, n), lambda i, j: (0, 0))],
        out_specs=[pl.BlockSpec((bt, tt, wd), lambda i, j: (i, j, 0)) for wd in widths],
        out_shape=[jax.ShapeDtypeStruct((b, t, wd), f32) for wd in widths],
        name="in_projection",
        compiler_params=_cparams(("arbitrary", "arbitrary")),
    )(x, mod, w_in_bf16)


def _seg_mean_matrix(width, seg):
    r = lax.broadcasted_iota(jnp.int32, (width, width), 0) // seg
    c = lax.broadcasted_iota(jnp.int32, (width, width), 1) // seg
    return jnp.where(r == c, 1.0 / seg, 0.0).astype(bf16)


def _seg_sum_matrix(width, seg):
    r = lax.broadcasted_iota(jnp.int32, (width, width), 0) // seg
    c = lax.broadcasted_iota(jnp.int32, (width, width), 1) // seg
    return jnp.where(r == c, 1.0, 0.0).astype(bf16)


def _seg_reduce(x, m):
    hi = x.astype(bf16)
    lo = (x - hi.astype(f32)).astype(bf16)
    return _dot(hi, m) + _dot(lo, m)


def _sgu_kernel(x_ref, w_ref, bias_ref, g_ref, b_ref, *o_refs, n_chunks, emit_v):
    wd = x_ref.shape[2] // 2
    n_groups = wd // HEAD_DIM
    x = x_ref[0]
    u = _gelu(x[:, :wd])
    v = _gelu(x[:, wd:])
    m = _seg_mean_matrix(wd, HEAD_DIM)
    mu = _seg_reduce(v, m)
    c = v - mu
    var = _seg_reduce(c * c, m)
    vn = c * lax.rsqrt(var + LN_EPS) * g_ref[...] + b_ref[...]
    if emit_v:
        o_refs[1][0] = vn
    row = lax.broadcasted_iota(jnp.int32, (SGU_CHUNK, SGU_CHUNK), 0)
    col = lax.broadcasted_iota(jnp.int32, (SGU_CHUNK, SGU_CHUNK), 1)
    lane_grp = lax.broadcasted_iota(jnp.int32, (SGU_CHUNK, wd), 1) // HEAD_DIM
    w_tril = [jnp.where(col <= row, w_ref[g], 0.0).astype(bf16) for g in range(n_groups)]
    for ci in range(n_chunks):
        rows = slice(ci * SGU_CHUNK, (ci + 1) * SGU_CHUNK)
        vc = vn[rows]
        s = bias_ref[...]
        for g in range(n_groups):
            s = s + _dot(w_tril[g], jnp.where(lane_grp == g, vc, 0.0).astype(bf16))
        o_refs[0][0, rows, :] = u[rows] * s


def sgu(puv, w_s, bias_tile, ln_g, ln_b, rows_per_step, emit_v):
    g_, r_, w2 = puv.shape
    wd = w2 // 2
    assert r_ % rows_per_step == 0 and rows_per_step % SGU_CHUNK == 0
    n_out = 2 if emit_v else 1
    outs = pl.pallas_call(
        functools.partial(_sgu_kernel, n_chunks=rows_per_step // SGU_CHUNK, emit_v=emit_v),
        grid=(g_, r_ // rows_per_step),
        in_specs=[pl.BlockSpec((1, rows_per_step, w2), lambda i, j: (i, j, 0)),
                  pl.BlockSpec(w_s.shape, lambda i, j: (0, 0, 0)),
                  pl.BlockSpec(bias_tile.shape, lambda i, j: (0, 0)),
                  pl.BlockSpec((1, wd), lambda i, j: (0, 0)),
                  pl.BlockSpec((1, wd), lambda i, j: (0, 0))],
        out_specs=[pl.BlockSpec((1, rows_per_step, wd), lambda i, j: (i, j, 0))] * n_out,
        out_shape=[jax.ShapeDtypeStruct((g_, r_, wd), f32)] * n_out,
        name="sgu",
        compiler_params=_cparams(("arbitrary", "arbitrary")),
    )(puv, w_s, bias_tile, ln_g.reshape(1, wd), ln_b.reshape(1, wd))
    return outs


def _topk_rows(scores, n_valid):
    row = lax.broadcasted_iota(jnp.int32, scores.shape, 0)
    valid = row < n_valid
    sel = jnp.zeros(scores.shape, f32)
    for j in range(n_valid):
        sj = scores[j:j + 1, :]
        beats = jnp.logical_and(valid, jnp.logical_or(scores > sj, jnp.logical_and(scores == sj, row < j)))
        cnt = jnp.sum(beats.astype(f32), axis=0, keepdims=True)
        sel = jnp.where(jnp.logical_and(row == j, cnt < MOBA_TOPK), 1.0, sel)
    return sel


def _moba_full_kernel(q_ref, k_ref, v_ref, bias_ref, o_ref, k16_ref, vt16_ref):
    t_len = q_ref.shape[1]
    nb = t_len // MOBA_BLOCK
    blk = MOBA_BLOCK
    scale = HEAD_DIM ** -0.5
    k2 = k_ref[0]
    k16_ref[...] = k2.astype(bf16)
    vt16_ref[...] = v_ref[0].T.astype(bf16)
    kmean16 = jnp.mean(k2.reshape(nb, blk, LANES), axis=1).astype(bf16)
    lane = lax.broadcasted_iota(jnp.int32, (1, LANES), 1)
    hmask = [(lane // HEAD_DIM == hh).astype(f32) for hh in range(2)]
    krow = lax.broadcasted_iota(jnp.int32, (blk, blk), 0)
    qcol = lax.broadcasted_iota(jnp.int32, (blk, blk), 1)
    causal = krow <= qcol
    drow = lax.broadcasted_iota(jnp.int32, (LANES, blk), 0)

    for qb in range(nb):
        n_keys = (qb + 1) * blk
        q2 = q_ref[0, qb * blk:(qb + 1) * blk, :]
        k_all = k16_ref[0:n_keys, :]
        vt_all = vt16_ref[:, 0:n_keys]
        qm = [q2 * hmask[hh] for hh in range(2)]
        raw = [_dot_nt(k_all, (qm[hh] * scale).astype(bf16)) for hh in range(2)]
        ok = [causal, causal]
        if qb > 0:
            for hh in range(2):
                sel_t = _topk_rows(_dot_nt(kmean16, qm[hh].astype(bf16)), qb)
                past_ok = jnp.broadcast_to(sel_t[0:qb][:, None, :] > 0.5, (qb, blk, blk))
                ok[hh] = jnp.concatenate([past_ok.reshape(qb * blk, blk), causal], axis=0)
        probs, denom = [], []
        for hh in range(2):
            bias = bias_ref[hh, (nb - 1 - qb) * blk:nb * blk, :]
            logits = jnp.where(ok[hh], raw[hh] + bias, NEG_BIG)
            p = jnp.exp(logits - jnp.max(logits, axis=0, keepdims=True))
            denom.append(jnp.sum(p, axis=0, keepdims=True))
            probs.append(p.astype(bf16))
        outs = [_dot(vt_all, probs[hh]) for hh in range(2)]
        o_ref[0, qb * blk:(qb + 1) * blk, :] = jnp.where(drow < HEAD_DIM, outs[0] / denom[0],
                                                         outs[1] / denom[1]).T


def moba_full(q, k, v, bias_t):
    b, t, hw = q.shape
    n_pairs = hw // LANES
    qkv_spec = pl.BlockSpec((1, t, LANES), lambda p, i: (i, 0, p))
    return pl.pallas_call(
        _moba_full_kernel,
        grid=(n_pairs, b),
        in_specs=[qkv_spec, qkv_spec, qkv_spec,
                  pl.BlockSpec((2, t, MOBA_BLOCK), lambda p, i: (p, 0, 0))],
        out_specs=pl.BlockSpec((1, t, LANES), lambda p, i: (i, 0, p)),
        out_shape=jax.ShapeDtypeStruct((b, t, hw), f32),
        scratch_shapes=[pltpu.VMEM((t, LANES), bf16), pltpu.VMEM((LANES, t), bf16)],
        name="moba_full",
        compiler_params=_cparams(("arbitrary", "arbitrary")),
    )(q, k, v, bias_t)


PAGES_PER_STEP = 16


def _pair_queries(q2):
    lane = lax.broadcasted_iota(jnp.int32, (1, LANES), 1)
    return jnp.concatenate([q2 * (lane // HEAD_DIM == hh).astype(f32) for hh in range(2)], axis=0)


def _page_pair(page_ref, p):
    hd, page = page_ref.shape[3:]
    return page_ref[0, 0, 2 * p:2 * p + 2].reshape(2 * hd, page)


def _moba_paged_probs_kernel(pt_ref, q_ref, kn_ref, bias_ref, bown_ref, *refs, n_pairs):
    kp_refs = refs[:PAGES_PER_STEP]
    p_ref, pown_ref, logit_ref, bsum_ref = refs[PAGES_PER_STEP:]
    c = pl.program_id(1)
    n_steps = pl.num_programs(1)
    t_q = q_ref.shape[1]
    page = kp_refs[0].shape[4]
    pages_per_block = MOBA_BLOCK // page
    n_blocks = bsum_ref.shape[0]
    blocks_per_step = PAGES_PER_STEP // pages_per_block
    step_keys = PAGES_PER_STEP * page
    scale = HEAD_DIM ** -0.5

    qf = [_pair_queries(q_ref[0, :, p * LANES:(p + 1) * LANES]) for p in range(n_pairs)]
    q16 = [(x * scale).astype(bf16) for x in qf]
    for p in range(n_pairs):
        kts = [_page_pair(kp_refs[j], p) for j in range(PAGES_PER_STEP)]
        logit_ref[p, c] = _dot(q16[p], jnp.concatenate(kts, axis=1).astype(bf16))
        for jb in range(blocks_per_step):
            ksum = kts[jb * pages_per_block]
            for j in range(jb * pages_per_block + 1, (jb + 1) * pages_per_block):
                ksum = ksum + kts[j]
            bsum_ref[c * blocks_per_step + jb, p] = ksum

    @pl.when(c == n_steps - 1)
    def _():
        n_keys = n_steps * step_keys
        blk_of_key = lax.broadcasted_iota(jnp.int32, (n_blocks, n_keys), 1) // MOBA_BLOCK
        expand = (blk_of_key == lax.broadcasted_iota(jnp.int32, (n_blocks, n_keys), 0)).astype(bf16)
        rq = lax.broadcasted_iota(jnp.int32, (2 * t_q, LANES), 0) % t_q
        ck = lax.broadcasted_iota(jnp.int32, (2 * t_q, LANES), 1)
        own_ok = ck <= rq
        zpad = jnp.zeros((LANES - t_q, LANES), f32)
        blk_lane = lax.broadcasted_iota(jnp.int32, (LANES, LANES), 1)
        for p in range(n_pairs):
            lanes = slice(p * LANES, (p + 1) * LANES)
            kmean_t = jnp.zeros((LANES, LANES), f32)
            for blk in range(n_blocks):
                col = jnp.sum(bsum_ref[blk, p], axis=1, keepdims=True) * (1.0 / MOBA_BLOCK)
                kmean_t = jnp.where(blk_lane == blk, col, kmean_t)
            scores = _dot(qf[p].astype(bf16), kmean_t.astype(bf16))
            sel_t = _topk_rows(scores.T[0:n_blocks], n_blocks)
            chosen = _dot(sel_t.T.astype(bf16), expand)
            lg = [jnp.where(chosen[:, s * step_keys:(s + 1) * step_keys] > 0.5,
                            logit_ref[p, s] + bias_ref[p, s], NEG_BIG) for s in range(n_steps)]
            kn = jnp.concatenate([kn_ref[0, :, lanes], zpad], axis=0).astype(bf16)
            own = jnp.where(own_ok, _dot_nt(q16[p], kn) + bown_ref[p], NEG_BIG)
            m = jnp.max(own, axis=1, keepdims=True)
            for x in lg:
                m = jnp.maximum(m, jnp.max(x, axis=1, keepdims=True))
            e_own = jnp.exp(own - m)
            e = [jnp.exp(x - m) for x in lg]
            denom = jnp.sum(e_own, axis=1, keepdims=True)
            for x in e:
                denom = denom + jnp.sum(x, axis=1, keepdims=True)
            inv = 1.0 / denom
            for s in range(n_steps):
                p_ref[0, p, s] = e[s] * inv
            pown_ref[0, p] = e_own * inv


def _moba_paged_mix_kernel(pt_ref, p_ref, pown_ref, vn_ref, *refs, n_pairs):
    vp_refs = refs[:PAGES_PER_STEP]
    o_ref, acc_ref = refs[PAGES_PER_STEP:]
    c = pl.program_id(1)
    n_steps = pl.num_programs(1)
    t_q = vn_ref.shape[1]
    page = vp_refs[0].shape[4]

    @pl.when(c == 0)
    def _():
        acc_ref[...] = jnp.zeros_like(acc_ref)

    for p in range(n_pairs):
        vt = jnp.concatenate([_page_pair(vp_refs[j], p) for j in range(PAGES_PER_STEP)], axis=1)
        acc_ref[p] += _dot_nt(p_ref[0, p, 0].astype(bf16), vt.astype(bf16))

    @pl.when(c == n_steps - 1)
    def _():
        lane = lax.broadcasted_iota(jnp.int32, (1, LANES), 1)
        zpad = jnp.zeros((LANES - t_q, LANES), f32)
        for p in range(n_pairs):
            lanes = slice(p * LANES, (p + 1) * LANES)
            vn = jnp.concatenate([vn_ref[0, :, lanes], zpad], axis=0).astype(bf16)
            acc = acc_ref[p] + _dot(pown_ref[0, p].astype(bf16), vn)
            o_ref[0, :, lanes] = jnp.where(lane < HEAD_DIM, acc[0:t_q], acc[t_q:2 * t_q])


def moba_paged(q, k_new, v_new, cache_k, cache_v, layer, page_table, bias_past, bias_own):
    b, t_q, hw = q.shape
    n_pairs = hw // LANES
    n_pages = page_table.shape[1]
    n_heads, hd, page = cache_k.shape[2:]
    assert page == LANES and hd == HEAD_DIM and n_heads * hd == hw
    assert n_pages % PAGES_PER_STEP == 0 and PAGES_PER_STEP % (MOBA_BLOCK // page) == 0
    n_steps = n_pages // PAGES_PER_STEP
    step_keys = PAGES_PER_STEP * page
    rows = 2 * t_q

    def page_spec(j):
        return pl.BlockSpec((1, 1, n_heads, hd, page),
                            lambda i, c, pt: (layer, pt[i, c * PAGES_PER_STEP + j], 0, 0, 0))

    new_spec = pl.BlockSpec((1, t_q, hw), lambda i, c, pt: (i, 0, 0))
    own_spec = pl.BlockSpec((1, n_pairs, rows, LANES), lambda i, c, pt: (i, 0, 0, 0))
    probs, p_own = pl.pallas_call(
        functools.partial(_moba_paged_probs_kernel, n_pairs=n_pairs),
        grid_spec=pltpu.PrefetchScalarGridSpec(
            num_scalar_prefetch=1,
            grid=(b, n_steps),
            in_specs=[new_spec, new_spec,
                      pl.BlockSpec(bias_past.shape, lambda i, c, pt: (0, 0, 0, 0)),
                      pl.BlockSpec(bias_own.shape, lambda i, c, pt: (0, 0, 0))]
                     + [page_spec(j) for j in range(PAGES_PER_STEP)],
            out_specs=[pl.BlockSpec((1, n_pairs, n_steps, rows, step_keys), lambda i, c, pt: (i, 0, 0, 0, 0)),
                       own_spec],
            scratch_shapes=[pltpu.VMEM((n_pairs, n_steps, rows, step_keys), f32),
                            pltpu.VMEM((n_pages * page // MOBA_BLOCK, n_pairs, LANES, page), f32)]),
        out_shape=[jax.ShapeDtypeStruct((b, n_pairs, n_steps, rows, step_keys), f32),
                   jax.ShapeDtypeStruct((b, n_pairs, rows, LANES), f32)],
        name="moba_paged_probs",
        compiler_params=_cparams(("arbitrary", "arbitrary")),
    )(page_table, q, k_new, bias_past, bias_own, *([cache_k] * PAGES_PER_STEP))
    return pl.pallas_call(
        functools.partial(_moba_paged_mix_kernel, n_pairs=n_pairs),
        grid_spec=pltpu.PrefetchScalarGridSpec(
            num_scalar_prefetch=1,
            grid=(b, n_steps),
            in_specs=[pl.BlockSpec((1, n_pairs, 1, rows, step_keys), lambda i, c, pt: (i, 0, c, 0, 0)),
                      own_spec, new_spec]
                     + [page_spec(j) for j in range(PAGES_PER_STEP)],
            out_specs=new_spec,
            scratch_shapes=[pltpu.VMEM((n_pairs, rows, LANES), f32)]),
        out_shape=jax.ShapeDtypeStruct((b, t_q, hw), f32),
        name="moba_paged_mix",
        compiler_params=_cparams(("arbitrary", "arbitrary")),
    )(page_table, probs, p_own, v_new, *([cache_v] * PAGES_PER_STEP))


def paged_bias_layout(bias_rows_arr, n_past):
    h, t_q, _ = bias_rows_arr.shape
    step_keys = PAGES_PER_STEP * LANES
    n_steps = n_past // step_keys
    past = bias_rows_arr[:, :, :n_past].reshape(h // 2, 2 * t_q, n_steps, step_keys)
    own = bias_rows_arr[:, :, n_past:].reshape(h // 2, 2 * t_q, LANES)
    return past.transpose(0, 2, 1, 3), own


def _rwkv_prep_kernel(p_ref, prev_ref, shift_ref, mu_ref, w0_ref, a0_ref, kk_ref, ka_ref, rk_ref,
                      ww_ref, wa_ref, wg_ref, *o_refs, hw):
    bt, tt, n = p_ref.shape
    j = pl.program_id(1)
    p = p_ref[...].reshape(bt * tt, n)
    first = jnp.where(j == 0, shift_ref[...], prev_ref[:, 7:8, :])
    first = jnp.broadcast_to(first, (bt, tt, n)).reshape(bt * tt, n)
    row = lax.broadcasted_iota(jnp.int32, (bt * tt, 1), 0)
    prev = jnp.where(row % tt == 0, first, pltpu.roll(p, 1, 0))
    xs = p + (prev - p) * mu_ref[...]
    r = xs[:, 0:hw]
    k = xs[:, hw:2 * hw]
    v = xs[:, 2 * hw:3 * hw]
    xwa = xs[:, 3 * hw:3 * hw + LANES]
    xg = xs[:, 3 * hw + LANES:3 * hw + 2 * LANES]
    z = -(w0_ref[...] + _dot_f32(jnp.tanh(xwa), ww_ref[...]))
    softplus = jnp.maximum(z, 0.0) + jnp.log(1.0 + jnp.exp(-jnp.abs(z)))
    log_decay = -jnp.exp(-softplus - 0.5)
    a = _sigmoid(a0_ref[...] + _dot_f32(xwa, wa_ref[...]))
    g = _dot_f32(_sigmoid(xg), wg_ref[...])
    mseg = _seg_sum_matrix(hw, HEAD_DIM)
    kk = k * kk_ref[...]
    kk = kk / jnp.maximum(jnp.sqrt(_seg_reduce(kk * kk, mseg)), 1e-12)
    k2 = k * (1.0 + (a - 1.0) * ka_ref[...])
    bonus = _seg_reduce(r * k2 * rk_ref[...], mseg) * v
    outs = (r, log_decay, k2, v, -kk, kk * a, g, bonus)
    for o_ref, val in zip(o_refs, outs):
        o_ref[...] = val.reshape(bt, tt, hw)


def rwkv_prep(prw, shift0, prm, bt, tt):
    b, t, n = prw.shape
    hw = prm['w0'].shape[-1]
    assert n == 3 * hw + 2 * LANES and tt % 8 == 0
    row = lambda x: x.reshape(1, -1)
    vec = lambda: pl.BlockSpec((1, hw), lambda i, j: (0, 0))
    mat = lambda: pl.BlockSpec((LANES, hw), lambda i, j: (0, 0))
    prev_blk = tt // 8
    return pl.pallas_call(
        functools.partial(_rwkv_prep_kernel, hw=hw),
        grid=(b // bt, t // tt),
        in_specs=[pl.BlockSpec((bt, tt, n), lambda i, j: (i, j, 0)),
                  pl.BlockSpec((bt, 8, n), lambda i, j: (i, jnp.maximum(j * prev_blk - 1, 0), 0)),
                  pl.BlockSpec((bt, 1, n), lambda i, j: (i, 0, 0)),
                  pl.BlockSpec((1, n), lambda i, j: (0, 0)),
                  vec(), vec(), vec(), vec(), vec(), mat(), mat(), mat()],
        out_specs=[pl.BlockSpec((bt, tt, hw), lambda i, j: (i, j, 0))] * 8,
        out_shape=[jax.ShapeDtypeStruct((b, t, hw), f32)] * 8,
        name="rwkv_prep",
        compiler_params=_cparams(("arbitrary", "arbitrary")),
    )(prw, prw, shift0.reshape(b, 1, n), row(prm['mu']), row(prm['w0']), row(prm['a0']),
      row(prm['k_k']), row(prm['k_a']), row(prm['r_k']), prm['w_w2p'], prm['w_a2p'], prm['w_g2'])


def rwkv_prep_params(mu, w0, w_w2, a0, w_a2, w_g2, k_k, k_a, r_k):
    hw = w0.shape[-1]
    zeros = jnp.zeros((LANES - DECAY_LORA, hw), w_w2.dtype)
    return dict(mu=mu, w0=w0, a0=a0, k_k=k_k, k_a=k_a, r_k=r_k.reshape(-1),
                w_w2p=jnp.concatenate([w_w2, zeros], axis=0),
                w_a2p=jnp.concatenate([zeros, w_a2], axis=0), w_g2=w_g2)


def _split3(x):
    hi = x.astype(bf16)
    return hi, (x - hi.astype(f32)).astype(bf16)


def _dot3(a, b):
    ah, al = _split3(a)
    bh, bl = _split3(b)
    return _dot(ah, bh) + (_dot(ah, bl) + _dot(al, bh))


def _dot3_nt(a, b):
    ah, al = _split3(a)
    bh, bl = _split3(b)
    return _dot_nt(ah, bh) + (_dot_nt(ah, bl) + _dot_nt(al, bh))


def _dot3_tn(a, b):
    return _dot3(a.T, b)


def _rwkv_chunk_maps_kernel(r_ref, lw_ref, k_ref, v_ref, x_ref, b_ref, m_ref, n_ref, p_ref, q_ref,
                            *, n_pairs, c_len):
    two_c = 2 * c_len
    rr = lax.broadcasted_iota(jnp.int32, (two_c, two_c), 0)
    cc = lax.broadcasted_iota(jnp.int32, (two_c, two_c), 1)
    same_head = (rr // c_len) == (cc // c_len)
    strict = jnp.logical_and(same_head, (cc % c_len) < (rr % c_len))
    incl = jnp.logical_and(same_head, (cc % c_len) <= (rr % c_len))
    eye = (rr == cc).astype(f32)
    row_head = lax.broadcasted_iota(jnp.int32, (two_c, LANES), 0) // c_len
    lane_head = lax.broadcasted_iota(jnp.int32, (two_c, LANES), 1) // HEAD_DIM
    bd = row_head == lane_head
    eye_k = (lax.broadcasted_iota(jnp.int32, (LANES, LANES), 0)
             == lax.broadcasted_iota(jnp.int32, (LANES, LANES), 1)).astype(f32)
    trow = lax.broadcasted_iota(jnp.int32, (c_len, LANES), 0)
    n_doubling = max(1, (c_len - 1).bit_length())

    def stack(a):
        return jnp.where(bd, jnp.concatenate([a, a], axis=0), 0.0)

    jobs = [(ci, p) for ci in range(r_ref.shape[1] // c_len) for p in range(n_pairs)]
    ops = []
    for ci, p in jobs:
        rows = slice(ci * c_len, (ci + 1) * c_len)
        lanes = slice(p * LANES, (p + 1) * LANES)
        lw = lw_ref[0, rows, lanes]
        cum = lw
        sh = 1
        while sh < c_len:
            cum = cum + jnp.where(trow >= sh, pltpu.roll(cum, sh, 0), 0.0)
            sh *= 2
        gam = jnp.exp(cum)
        inv_gam = jnp.exp(-cum)
        ops.append(dict(
            a=stack(x_ref[0, rows, lanes] * jnp.exp(cum - lw)), r=stack(r_ref[0, rows, lanes] * gam),
            b=stack(b_ref[0, rows, lanes] * inv_gam), k=stack(k_ref[0, rows, lanes] * inv_gam),
            v=stack(v_ref[0, rows, lanes]), g_end=gam[c_len - 1:c_len, :]))
    for o in ops:
        bk = jnp.concatenate([o['b'], o['k']], axis=0)
        gram_a = _dot3_nt(o['a'], bk)
        gram_r = _dot_nt_f32(o['r'], bk)
        o['l_ab'] = jnp.where(strict, gram_a[:, 0:two_c], 0.0)
        o['l_ak'] = jnp.where(strict, gram_a[:, two_c:2 * two_c], 0.0)
        o['m_rb'] = jnp.where(incl, gram_r[:, 0:two_c], 0.0)
        o['m_rk'] = jnp.where(incl, gram_r[:, two_c:2 * two_c], 0.0)
        o['pw'] = o['l_ab']
        o['u'] = o['l_ab']
    for it in range(n_doubling - 1):
        mm = _dot3 if it == 0 else _dot_f32
        for o in ops:
            o['pw'] = mm(o['pw'], o['pw'])
        for o in ops:
            o['u'] = o['u'] + o['pw'] + mm(o['u'], o['pw'])
    for o in ops:
        o['lkv'] = _dot3(o['l_ak'], o['v'])
    for o in ops:
        o['pq1'] = _dot3(eye + o['u'], jnp.concatenate([o['a'], o['lkv']], axis=1))
    for o in ops:
        o['pq2'] = _dot_f32(o['m_rb'], o['pq1'])
        o['mkv'] = _dot_f32(o['m_rk'], o['v'])
    for o in ops:
        o['mn'] = _dot3_tn(o['pq1'], o['b'] * o['g_end'])
        o['vk'] = _dot3_tn(o['v'], o['k'] * o['g_end'])
    for (ci, p), o in zip(jobs, ops):
        p_ref[0, ci, p] = o['r'] + o['pq2'][:, 0:LANES]
        q_ref[0, ci, p] = o['pq2'][:, LANES:2 * LANES] + o['mkv']
        m_ref[0, ci, p] = eye_k * o['g_end'] + o['mn'][0:LANES]
        n_ref[0, ci, p] = o['mn'][LANES:2 * LANES] + o['vk']


def _rwkv_state_kernel(m_ref, n_ref, p_ref, q_ref, g_ref, bonus_ref, s0_ref, lng_ref, lnb_ref,
                       y_ref, sout_ref, state_ref, *, n_pairs, c_len):
    jc = pl.program_id(1)
    n_steps = pl.num_programs(1)

    @pl.when(jc == 0)
    def _():
        state_ref[...] = s0_ref[0]

    mseg = _seg_mean_matrix(LANES, HEAD_DIM)
    states = [state_ref[p] for p in range(n_pairs)]
    for ci in range(m_ref.shape[1]):
        rows = slice(ci * c_len, (ci + 1) * c_len)
        outs = [_dot3_nt(p_ref[0, ci, p], states[p]) + q_ref[0, ci, p] for p in range(n_pairs)]
        states = [_dot3(states[p], m_ref[0, ci, p]) + n_ref[0, ci, p] for p in range(n_pairs)]
        for p in range(n_pairs):
            lanes = slice(p * LANES, (p + 1) * LANES)
            o = outs[p][0:c_len] + outs[p][c_len:2 * c_len]
            mu = _seg_reduce(o, mseg)
            cen = o - mu
            var = _seg_reduce(cen * cen, mseg)
            y = cen * lax.rsqrt(var + GN_EPS) * lng_ref[:, lanes] + lnb_ref[:, lanes]
            y_ref[0, rows, lanes] = (y + bonus_ref[0, rows, lanes]) * g_ref[0, rows, lanes]
    for p in range(n_pairs):
        state_ref[p] = states[p]

    @pl.when(jc == n_steps - 1)
    def _():
        sout_ref[0] = state_ref[...]


def rwkv_chunked(seq, s0_bd, lnx_g, lnx_b, c_len, maps_chunks, scan_chunks):
    r, lw, k2, v, x, kka, g, bonus = seq
    b, t, hw = r.shape
    n_pairs = hw // LANES
    n_chunks = t // c_len
    assert t % c_len == 0 and n_chunks % maps_chunks == 0 and n_chunks % scan_chunks == 0
    two_c = 2 * c_len

    def map_spec(rows, per_step):
        return pl.BlockSpec((1, per_step, n_pairs, rows, LANES), lambda i, j: (i, j, 0, 0, 0))

    def map_shape(rows):
        return jax.ShapeDtypeStruct((b, n_chunks, n_pairs, rows, LANES), f32)

    seq_spec = lambda per_step: pl.BlockSpec((1, per_step * c_len, hw), lambda i, j: (i, j, 0))
    maps = pl.pallas_call(
        functools.partial(_rwkv_chunk_maps_kernel, n_pairs=n_pairs, c_len=c_len),
        grid=(b, n_chunks // maps_chunks),
        in_specs=[seq_spec(maps_chunks)] * 6,
        out_specs=[map_spec(LANES, maps_chunks), map_spec(LANES, maps_chunks),
                   map_spec(two_c, maps_chunks), map_spec(two_c, maps_chunks)],
        out_shape=[map_shape(LANES), map_shape(LANES), map_shape(two_c), map_shape(two_c)],
        name="rwkv_chunk_maps",
        compiler_params=_cparams(("arbitrary", "arbitrary")),
    )(r, lw, k2, v, x, kka)
    st_spec = pl.BlockSpec((1, n_pairs, LANES, LANES), lambda i, j: (i, 0, 0, 0))
    vec = pl.BlockSpec((1, hw), lambda i, j: (0, 0))
    return pl.pallas_call(
        functools.partial(_rwkv_state_kernel, n_pairs=n_pairs, c_len=c_len),
        grid=(b, n_chunks // scan_chunks),
        in_specs=[map_spec(LANES, scan_chunks), map_spec(LANES, scan_chunks),
                  map_spec(two_c, scan_chunks), map_spec(two_c, scan_chunks),
                  seq_spec(scan_chunks), seq_spec(scan_chunks), st_spec, vec, vec],
        out_specs=[seq_spec(scan_chunks), st_spec],
        out_shape=[jax.ShapeDtypeStruct((b, t, hw), f32),
                   jax.ShapeDtypeStruct((b, n_pairs, LANES, LANES), f32)],
        scratch_shapes=[pltpu.VMEM((n_pairs, LANES, LANES), f32)],
        name="rwkv_state_scan",
        compiler_params=_cparams(("arbitrary", "arbitrary")),
    )(*maps, g, bonus, s0_bd, lnx_g.reshape(1, hw), lnx_b.reshape(1, hw))


def state_to_blockdiag(s):
    b, h, n, _ = s.shape
    sp = s.reshape(b, h // 2, 2, n, n)
    z = jnp.zeros_like(sp[:, :, 0])
    top = jnp.concatenate([sp[:, :, 0], z], axis=-1)
    bot = jnp.concatenate([z, sp[:, :, 1]], axis=-1)
    return jnp.concatenate([top, bot], axis=-2)


def state_from_blockdiag(sbd):
    b, hp, _, _ = sbd.shape
    n = HEAD_DIM
    return jnp.stack([sbd[:, :, :n, :n], sbd[:, :, n:, n:]], axis=2).reshape(b, hp * 2, n, n)


def _layer_norm(z, g, b):
    mu = jnp.mean(z, axis=-1, keepdims=True)
    c = z - mu
    var = jnp.mean(c * c, axis=-1, keepdims=True)
    return c * lax.rsqrt(var + LN_EPS) * g + b


def _route(sel, aff, n_groups, per_group, top_k):
    rows = sel.shape[1]
    srow = [sel[e:e + 1, :] for e in range(n_groups * per_group)]
    arow = [aff[e:e + 1, :] for e in range(n_groups * per_group)]
    scores = []
    for g in range(n_groups):
        s = srow[g * per_group:(g + 1) * per_group]
        best = None
        for i in range(per_group):
            for j in range(i + 1, per_group):
                pair = s[i] + s[j]
                best = pair if best is None else jnp.maximum(best, pair)
        scores.append(best)
    top = scores[0]
    for g in range(1, n_groups):
        top = jnp.maximum(top, scores[g])
    taken = jnp.zeros((1, rows), jnp.bool_)
    eidx = lax.broadcasted_iota(jnp.int32, (n_groups * per_group, rows), 0)
    gates = jnp.zeros((n_groups * per_group, rows), f32)
    for g in range(n_groups):
        is_grp = jnp.logical_and(scores[g] == top, jnp.logical_not(taken))
        taken = jnp.logical_or(taken, is_grp)
        s = srow[g * per_group:(g + 1) * per_group]
        a = arow[g * per_group:(g + 1) * per_group]
        chosen = []
        for i in range(per_group):
            cnt = jnp.zeros((1, rows), f32)
            for j in range(per_group):
                if j != i:
                    ahead = (s[j] > s[i]) if j > i else (s[j] >= s[i])
                    cnt = cnt + ahead.astype(f32)
            chosen.append(jnp.logical_and(cnt < top_k, is_grp))
        denom = jnp.zeros((1, rows), f32)
        for i in range(per_group):
            denom = denom + jnp.where(chosen[i], a[i], 0.0)
        for i in range(per_group):
            gate = jnp.where(chosen[i], a[i] / denom, 0.0)
            gates = jnp.where(eidx == g * per_group + i, gate, gates)
    return gates


def _outproj_kernel(x_ref, ya_ref, yb_ref, yc_ref, mod_ref, w_ref, g_ref, b_ref, wr_ref, rb_ref,
                    x1_ref, h2_ref, gates_ref, *, d, alpha):
    bt, tt, _ = x_ref.shape
    rows = bt * tt
    off = 0
    mix = jnp.zeros((rows, d), f32)
    for y_ref in (ya_ref, yb_ref, yc_ref):
        wd = y_ref.shape[2]
        mix = mix + _dot(y_ref[...].reshape(rows, wd).astype(bf16), w_ref[off:off + wd, :])
        off += wd
    g1 = mod_ref[:, :, 2 * d:3 * d]
    z = alpha * x_ref[...] + (1.0 + g1) * mix.reshape(bt, tt, d)
    x1 = _layer_norm(z, g_ref[...], b_ref[...])
    x1_ref[...] = x1
    sh2 = mod_ref[:, :, 3 * d:4 * d]
    sc2 = mod_ref[:, :, 4 * d:5 * d]
    h2 = (x1 * (1.0 + sc2) + sh2).reshape(rows, d)
    h2_ref[...] = h2.astype(bf16).reshape(bt, tt, d)
    aff = _sigmoid(_dot_nt_f32(wr_ref[...], h2))
    gates_ref[...] = _route(aff + rb_ref[...], aff, N_EXPERT_GROUPS,
                            wr_ref.shape[0] // N_EXPERT_GROUPS, TOP_K_EXPERTS)


def out_projection(x, ya, yb, yc, mod, w_out_bf16, ln_g, ln_b, w_router_t, router_bias, alpha, bt, tt):
    b, t, d = x.shape
    n_exp = w_router_t.shape[0]
    rows = bt * tt
    blk = lambda wd: pl.BlockSpec((bt, tt, wd), lambda i, j: (i, j, 0))
    full = lambda a: pl.BlockSpec(a.shape, lambda i, j: (0,) * a.ndim)
    n_tblk = t // tt
    ln_g = ln_g.reshape(1, d)
    ln_b = ln_b.reshape(1, d)
    rb = router_bias.reshape(n_exp, 1)
    return pl.pallas_call(
        functools.partial(_outproj_kernel, d=d, alpha=alpha),
        grid=(b // bt, n_tblk),
        in_specs=[blk(d), blk(ya.shape[2]), blk(yb.shape[2]), blk(yc.shape[2]),
                  pl.BlockSpec((bt, 1, mod.shape[2]), lambda i, j: (i, 0, 0)),
                  full(w_out_bf16), full(ln_g), full(ln_b), full(w_router_t), full(rb)],
        out_specs=[blk(d), blk(d), pl.BlockSpec((n_exp, rows), lambda i, j: (0, i * n_tblk + j))],
        out_shape=[jax.ShapeDtypeStruct((b, t, d), f32), jax.ShapeDtypeStruct((b, t, d), bf16),
                   jax.ShapeDtypeStruct((n_exp, b * t), f32)],
        name="out_projection",
        compiler_params=_cparams(("arbitrary", "arbitrary")),
    )(x, ya, yb, yc, mod, w_out_bf16, ln_g, ln_b, w_router_t, rb)


def _moe_kernel(h_ref, gates_ref, x1_ref, mod_ref, wg_ref, wu_ref, wd_ref, g_ref, b_ref,
                o_ref, *, d, alpha):
    bt, tt, _ = h_ref.shape
    rows = bt * tt
    n_exp, f, _ = wd_ref.shape
    h = h_ref[...].reshape(rows, d)
    gates = gates_ref[...].reshape(rows, gates_ref.shape[2])
    lane = lax.broadcasted_iota(jnp.int32, gates.shape, 1)
    acts = []
    for e in range(n_exp):
        gate = jnp.sum(jnp.where(lane == e, gates, 0.0), axis=1, keepdims=True)
        acts.append((_silu(_dot(h, wg_ref[e])) * _dot(h, wu_ref[e]) * gate).astype(bf16))
    ffn = _dot(jnp.concatenate(acts, axis=1), wd_ref[...].reshape(n_exp * f, d))
    g2 = mod_ref[:, :, 5 * d:6 * d]
    z = alpha * x1_ref[...] + (1.0 + g2) * ffn.reshape(bt, tt, d)
    o_ref[...] = _layer_norm(z, g_ref[...], b_ref[...])


def moe_ffn(h2, gates, x1, mod, wg_bf16, wu_bf16, wd_bf16, ln_g, ln_b, alpha, bt, tt):
    b, t, d = x1.shape
    n_exp = wg_bf16.shape[0]
    f = wg_bf16.shape[2]
    blk = lambda wd: pl.BlockSpec((bt, tt, wd), lambda i, j: (i, j, 0))
    resident = lambda a: pl.BlockSpec(a.shape, lambda i, j: (0, 0, 0), pipeline_mode=pl.Buffered(1))
    return pl.pallas_call(
        functools.partial(_moe_kernel, d=d, alpha=alpha),
        grid=(b // bt, t // tt),
        in_specs=[blk(d), blk(n_exp), blk(d),
                  pl.BlockSpec((bt, 1, mod.shape[2]), lambda i, j: (i, 0, 0)),
                  resident(wg_bf16), resident(wu_bf16), resident(wd_bf16),
                  pl.BlockSpec((1, d), lambda i, j: (0, 0)),
                  pl.BlockSpec((1, d), lambda i, j: (0, 0))],
        out_specs=blk(d),
        out_shape=jax.ShapeDtypeStruct((b, t, d), f32),
        name="moe_ffn",
        compiler_params=_cparams(("arbitrary", "arbitrary")),
    )(h2, gates, x1, mod, wg_bf16, wu_bf16, wd_bf16, ln_g.reshape(1, d), ln_b.reshape(1, d))


def sgu_full_params(w_s, b_s):
    return w_s, jnp.repeat(b_s.T, HEAD_DIM, axis=1)


def sgu_short_params(w_s, b_s, t_len):
    reps = SGU_CHUNK // t_len
    eye = jnp.eye(reps, dtype=w_s.dtype)
    w_small = w_s[:, :t_len, :t_len]
    w_big = jnp.einsum('ab,gts->gatbs', eye, w_small).reshape(w_s.shape[0], SGU_CHUNK, SGU_CHUNK)
    bias = jnp.tile(jnp.repeat(b_s[:, :t_len].T, HEAD_DIM, axis=1), (reps, 1))
    return w_big, bias


PROMPT_ROWS = 512
RWKV_CHUNK = (64, 2, 4)


def _hybrid_layer(x, mod, tile, wts, attend, sgu_prm, shift0, s0_bd, rwkv_chunk, emit_sgu_v, alpha):
    bt, tt = tile
    b, t, d = x.shape
    hw_sgu = sgu_prm[0].shape[0] * HEAD_DIM
    hw_attn = wts['attn_width']
    widths = (2 * hw_sgu, hw_attn, hw_attn, hw_attn, wts['w_in'].shape[1] - 2 * hw_sgu - 3 * hw_attn)
    puv, q, k, v, prw = in_projection(x, mod, wts['w_in'], widths, bt, tt)
    n_rows = b * t
    sgu_rows = min(PROMPT_ROWS, n_rows) if t < SGU_CHUNK else tt
    sgu_in = puv.reshape(1, n_rows, 2 * hw_sgu) if t < SGU_CHUNK else puv
    sgu_out = sgu(sgu_in, sgu_prm[0], sgu_prm[1], wts['sgu_ln_g'], wts['sgu_ln_b'], sgu_rows, emit_sgu_v)
    ya = sgu_out[0].reshape(b, t, hw_sgu)
    sgu_v = sgu_out[1].reshape(b, t, hw_sgu) if emit_sgu_v else None
    yb = attend(q, k, v)
    seq = rwkv_prep(prw, shift0, wts['rwkv'], bt, tt)
    yc, s_bd = rwkv_chunked(seq, s0_bd, wts['lnx_g'], wts['lnx_b'], *rwkv_chunk)
    x1, h2, gates_t = out_projection(x, ya, yb, yc, mod, wts['w_out'], wts['ln1_g'], wts['ln1_b'],
                                     wts['w_router_t'], wts['router_bias'], alpha, bt, tt)
    gates = gates_t.T.reshape(b, t, gates_t.shape[0])
    x2 = moe_ffn(h2, gates, x1, mod, wts['moe_wg'], wts['moe_wu'], wts['moe_wd'],
                 wts['ln2_g'], wts['ln2_b'], alpha, bt, tt)
    return x2, k, v, state_from_blockdiag(s_bd), prw[:, -1], sgu_v


def kernel(x_prompt, x_sample, cache_k, cache_v, state_rwkv, state_shift, page_table, c_prompt, c_sample, w_ada, b_ada, w_in, w_out, sgu_w, sgu_b, sgu_ln_g, sgu_ln_b, rel_bias, rwkv_mu, rwkv_w0, rwkv_w_w2, rwkv_a0, rwkv_w_a2, rwkv_w_g2, rwkv_k_k, rwkv_k_a, rwkv_r_k, rwkv_lnx_g, rwkv_lnx_b, ln1_g, ln1_b, ln2_g, ln2_b, w_router, router_bias, moe_w_gate, moe_w_up, moe_w_down):
    depth = w_ada.shape[0]
    bp, t_p, d = x_prompt.shape
    bs, t_s, _ = x_sample.shape
    n_heads = cache_k.shape[3]
    hw_attn = n_heads * HEAD_DIM
    n_pool, page = cache_k.shape[1], cache_k.shape[2]
    past_len = page_table.shape[1] * page
    rwkv_in = state_shift.shape[2]
    alpha = (2 * depth) ** 0.25

    mods = ada_modulation(jnp.concatenate([c_prompt, c_sample], axis=0), w_ada, b_ada)
    bias_p = bias_tiles(rel_bias, n_heads, t_p // MOBA_BLOCK).reshape(n_heads, t_p, MOBA_BLOCK)
    bias_past, bias_own = paged_bias_layout(
        bias_rows(rel_bias, n_heads, past_len, t_s, past_len + LANES), past_len)

    def attend_paged(layer):
        def attend(q, k, v):
            return moba_paged(q, k, v, cache_kt, cache_vt, layer, page_table, bias_past, bias_own)
        return attend

    cache_kt = cache_k.transpose(0, 1, 3, 4, 2)
    cache_vt = cache_v.transpose(0, 1, 3, 4, 2)
    w_in16, w_out16 = w_in.astype(bf16), w_out.astype(bf16)
    wg16, wu16, wd16 = moe_w_gate.astype(bf16), moe_w_up.astype(bf16), moe_w_down.astype(bf16)
    w_router_t = w_router.T
    zero_shift = jnp.zeros((bp, rwkv_in), f32)
    zero_state = jnp.zeros((bp, n_heads // 2, LANES, LANES), f32)

    xp, xs = x_prompt, x_sample
    outs = [[] for _ in range(9)]
    for l in range(depth):
        wts = dict(
            attn_width=hw_attn, w_in=w_in16[l], w_out=w_out16[l], sgu_ln_g=sgu_ln_g[l], sgu_ln_b=sgu_ln_b[l],
            rwkv=rwkv_prep_params(rwkv_mu[l], rwkv_w0[l], rwkv_w_w2[l], rwkv_a0[l], rwkv_w_a2[l],
                                  rwkv_w_g2[l], rwkv_k_k[l], rwkv_k_a[l], rwkv_r_k[l]),
            lnx_g=rwkv_lnx_g[l], lnx_b=rwkv_lnx_b[l], ln1_g=ln1_g[l], ln1_b=ln1_b[l],
            ln2_g=ln2_g[l], ln2_b=ln2_b[l], w_router_t=w_router_t, router_bias=router_bias,
            moe_wg=wg16[l], moe_wu=wu16[l], moe_wd=wd16[l])
        mod_p = mods[l, :bp][:, None, :]
        mod_s = mods[l, bp:][:, None, :]
        xp, kp, vp, sp, shp, _ = _hybrid_layer(
            xp, mod_p, (1, PROMPT_ROWS), wts, lambda q, k, v: moba_full(q, k, v, bias_p),
            sgu_full_params(sgu_w[l], sgu_b[l]), zero_shift, zero_state, RWKV_CHUNK, False, alpha)
        xs, ks_, vs_, ss, shs, sgu_v = _hybrid_layer(
            xs, mod_s, (bs, t_s), wts, attend_paged(l),
            sgu_short_params(sgu_w[l], sgu_b[l], t_s), state_shift[l],
            state_to_blockdiag(state_rwkv[l]), (t_s, 1, 1), True, alpha)
        heads = lambda a: a.reshape(a.shape[0], a.shape[1], n_heads, HEAD_DIM)
        for lst, val in zip(outs, (heads(kp), heads(vp), heads(ks_), heads(vs_), sp, ss, shp, shs, sgu_v)):
            lst.append(val)
    return (xp, xs) + tuple(jnp.stack(lst) for lst in outs)
```

```python
import functools
import math

import jax
import jax.numpy as jnp
import numpy as np
from jax import lax
from jax.experimental import pallas as pl
from jax.experimental.pallas import tpu as pltpu

HEAD_DIM = 64
LANES = 128
SGU_CHUNK = 128
MOBA_BLOCK = 256
MOBA_TOPK = 3
REL_BUCKETS = 32
REL_MAX_DIST = 1024
DECAY_LORA = 64
AAA_LORA = 64
GATE_LORA = 128
GN_EPS = 64e-5
LN_EPS = 1e-5
N_EXPERT_GROUPS = 4
TOP_K_EXPERTS = 2
VMEM_LIMIT = 56 * 1024 * 1024
NEG_BIG = -1e30

f32 = jnp.float32
bf16 = jnp.bfloat16


def _cparams(sem):
    return pltpu.CompilerParams(dimension_semantics=sem, vmem_limit_bytes=VMEM_LIMIT)


def _dot(a, b):
    return jnp.dot(a, b, preferred_element_type=f32)


def _dot_nt(a, b):
    return lax.dot_general(a, b, (((1,), (1,)), ((), ())), preferred_element_type=f32)


def _dot_f32(a, b):
    return _dot(a.astype(bf16), b.astype(bf16))


def _dot_nt_f32(a, b):
    return _dot_nt(a.astype(bf16), b.astype(bf16))


def _gelu(x):
    c = math.sqrt(2.0 / math.pi)
    return 0.5 * x * (1.0 + jnp.tanh(c * (x + 0.044715 * (x * x * x))))


def _sigmoid(x):
    return 1.0 / (1.0 + jnp.exp(-x))


def _silu(x):
    return x * _sigmoid(x)


def _t5_bias(dist, rel_ref, head):
    max_exact = REL_BUCKETS // 2
    d = jnp.maximum(dist, 0)
    df = jnp.maximum(d, 1).astype(f32)
    large = max_exact + (jnp.log(df / max_exact) / math.log(REL_MAX_DIST / max_exact)
                         * (REL_BUCKETS - max_exact)).astype(jnp.int32)
    large = jnp.minimum(large, REL_BUCKETS - 1)
    bucket = jnp.where(d < max_exact, d, large)
    out = jnp.zeros(dist.shape, f32)
    for r in range(REL_BUCKETS):
        out = jnp.where(bucket == r, rel_ref[r, head], out)
    return out


def _bias_tiles_kernel(rel_ref, o_ref):
    h = pl.program_id(0)
    delta = pl.num_programs(1) - 1 - pl.program_id(1)
    i = lax.broadcasted_iota(jnp.int32, (MOBA_BLOCK, MOBA_BLOCK), 0)
    j = lax.broadcasted_iota(jnp.int32, (MOBA_BLOCK, MOBA_BLOCK), 1)
    dist = delta * MOBA_BLOCK + j - i
    o_ref[0, 0] = _t5_bias(dist, rel_ref, h)


def bias_tiles(rel_bias, n_heads, n_blocks):
    return pl.pallas_call(
        _bias_tiles_kernel,
        grid=(n_heads, n_blocks),
        in_specs=[pl.BlockSpec(memory_space=pltpu.SMEM)],
        out_specs=pl.BlockSpec((1, 1, MOBA_BLOCK, MOBA_BLOCK), lambda h, d: (h, d, 0, 0)),
        out_shape=jax.ShapeDtypeStruct((n_heads, n_blocks, MOBA_BLOCK, MOBA_BLOCK), f32),
        name="bias_tiles",
        compiler_params=_cparams(("arbitrary", "arbitrary")),
    )(rel_bias)


def _bias_rows_kernel(rel_ref, o_ref, *, q_start, t_q):
    h = pl.program_id(0)
    n_keys = o_ref.shape[2]
    i = lax.broadcasted_iota(jnp.int32, (t_q, n_keys), 0)
    j = lax.broadcasted_iota(jnp.int32, (t_q, n_keys), 1)
    o_ref[0] = _t5_bias(q_start + i - j, rel_ref, h)


def bias_rows(rel_bias, n_heads, q_start, t_q, n_keys):
    return pl.pallas_call(
        functools.partial(_bias_rows_kernel, q_start=q_start, t_q=t_q),
        grid=(n_heads,),
        in_specs=[pl.BlockSpec(memory_space=pltpu.SMEM)],
        out_specs=pl.BlockSpec((1, t_q, n_keys), lambda h: (h, 0, 0)),
        out_shape=jax.ShapeDtypeStruct((n_heads, t_q, n_keys), f32),
        name="bias_rows",
        compiler_params=_cparams(("arbitrary",)),
    )(rel_bias)


def _ada_kernel(c_ref, w_ref, b_ref, o_ref):
    c = _silu(c_ref[...]).astype(bf16)
    o_ref[0] = _dot(c, w_ref[0].astype(bf16)) + b_ref[0]


def ada_modulation(c_all, w_ada, b_ada, tn=1536):
    depth, d, n = w_ada.shape
    rows = c_all.shape[0]
    return pl.pallas_call(
        _ada_kernel,
        grid=(depth, n // tn),
        in_specs=[pl.BlockSpec((rows, d), lambda l, j: (0, 0)),
                  pl.BlockSpec((1, d, tn), lambda l, j: (l, 0, j)),
                  pl.BlockSpec((1, 1, tn), lambda l, j: (l, 0, j))],
        out_specs=pl.BlockSpec((1, rows, tn), lambda l, j: (l, 0, j)),
        out_shape=jax.ShapeDtypeStruct((depth, rows, n), f32),
        name="ada_modulation",
        compiler_params=_cparams(("arbitrary", "arbitrary")),
    )(c_all, w_ada, b_ada.reshape(depth, 1, n))


def _inproj_kernel(x_ref, mod_ref, w_ref, *o_refs, d, widths):
    bt, tt, _ = x_ref.shape
    sh = mod_ref[:, :, 0:d]
    sc = mod_ref[:, :, d:2 * d]
    h = (x_ref[...] * (1.0 + sc) + sh).reshape(bt * tt, d).astype(bf16)
    off = 0
    for o_ref, wd in zip(o_refs, widths):
        o_ref[...] = _dot(h, w_ref[:, off:off + wd]).reshape(bt, tt, wd)
        off += wd


def in_projection(x, mod, w_in_bf16, widths, bt, tt):
    b, t, d = x.shape
    n = w_in_bf16.shape[1]
    assert sum(widths) == n and b % bt == 0 and t % tt == 0
    return pl.pallas_call(
        functools.partial(_inproj_kernel, d=d, widths=widths),
        grid=(b // bt, t // tt),
        in_specs=[pl.BlockSpec((bt, tt, d), lambda i, j: (i, j, 0)),
                  pl.BlockSpec((bt, 1, mod.shape[2]), lambda i, j: (i, 0, 0)),
                  pl.BlockSpec((d, n), lambda i, j: (0, 0))],
        out_specs=[pl.BlockSpec((bt, tt, wd), lambda i, j: (i, j, 0)) for wd in widths],
        out_shape=[jax.ShapeDtypeStruct((b, t, wd), f32) for wd in widths],
        name="in_projection",
        compiler_params=_cparams(("arbitrary", "arbitrary")),
    )(x, mod, w_in_bf16)


def _seg_mean_matrix(width, seg):
    r = lax.broadcasted_iota(jnp.int32, (width, width), 0) // seg
    c = lax.broadcasted_iota(jnp.int32, (width, width), 1) // seg
    return jnp.where(r == c, 1.0 / seg, 0.0).astype(bf16)


def _seg_sum_matrix(width, seg):
    r = lax.broadcasted_iota(jnp.int32, (width, width), 0) // seg
    c = lax.broadcasted_iota(jnp.int32, (width, width), 1) // seg
    return jnp.where(r == c, 1.0, 0.0).astype(bf16)


def _seg_reduce(x, m):
    hi = x.astype(bf16)
    lo = (x - hi.astype(f32)).astype(bf16)
    return _dot(hi, m) + _dot(lo, m)


def _sgu_kernel(x_ref, w_ref, bias_ref, g_ref, b_ref, *o_refs, n_chunks, emit_v):
    wd = x_ref.shape[2] // 2
    n_groups = wd // HEAD_DIM
    x = x_ref[0]
    u = _gelu(x[:, :wd])
    v = _gelu(x[:, wd:])
    m = _seg_mean_matrix(wd, HEAD_DIM)
    mu = _seg_reduce(v, m)
    c = v - mu
    var = _seg_reduce(c * c, m)
    vn = c * lax.rsqrt(var + LN_EPS) * g_ref[...] + b_ref[...]
    if emit_v:
        o_refs[1][0] = vn
    row = lax.broadcasted_iota(jnp.int32, (SGU_CHUNK, SGU_CHUNK), 0)
    col = lax.broadcasted_iota(jnp.int32, (SGU_CHUNK, SGU_CHUNK), 1)
    lane_grp = lax.broadcasted_iota(jnp.int32, (SGU_CHUNK, wd), 1) // HEAD_DIM
    w_tril = [jnp.where(col <= row, w_ref[g], 0.0).astype(bf16) for g in range(n_groups)]
    for ci in range(n_chunks):
        rows = slice(ci * SGU_CHUNK, (ci + 1) * SGU_CHUNK)
        vc = vn[rows]
        s = bias_ref[...]
        for g in range(n_groups):
            s = s + _dot(w_tril[g], jnp.where(lane_grp == g, vc, 0.0).astype(bf16))
        o_refs[0][0, rows, :] = u[rows] * s


def sgu(puv, w_s, bias_tile, ln_g, ln_b, rows_per_step, emit_v):
    g_, r_, w2 = puv.shape
    wd = w2 // 2
    assert r_ % rows_per_step == 0 and rows_per_step % SGU_CHUNK == 0
    n_out = 2 if emit_v else 1
    outs = pl.pallas_call(
        functools.partial(_sgu_kernel, n_chunks=rows_per_step // SGU_CHUNK, emit_v=emit_v),
        grid=(g_, r_ // rows_per_step),
        in_specs=[pl.BlockSpec((1, rows_per_step, w2), lambda i, j: (i, j, 0)),
                  pl.BlockSpec(w_s.shape, lambda i, j: (0, 0, 0)),
                  pl.BlockSpec(bias_tile.shape, lambda i, j: (0, 0)),
                  pl.BlockSpec((1, wd), lambda i, j: (0, 0)),
                  pl.BlockSpec((1, wd), lambda i, j: (0, 0))],
        out_specs=[pl.BlockSpec((1, rows_per_step, wd), lambda i, j: (i, j, 0))] * n_out,
        out_shape=[jax.ShapeDtypeStruct((g_, r_, wd), f32)] * n_out,
        name="sgu",
        compiler_params=_cparams(("arbitrary", "arbitrary")),
    )(puv, w_s, bias_tile, ln_g.reshape(1, wd), ln_b.reshape(1, wd))
    return outs


def _topk_rows(scores, n_valid):
    row = lax.broadcasted_iota(jnp.int32, scores.shape, 0)
    valid = row < n_valid
    sel = jnp.zeros(scores.shape, f32)
    for j in range(n_valid):
        sj = scores[j:j + 1, :]
        beats = jnp.logical_and(valid, jnp.logical_or(scores > sj, jnp.logical_and(scores == sj, row < j)))
        cnt = jnp.sum(beats.astype(f32), axis=0, keepdims=True)
        sel = jnp.where(jnp.logical_and(row == j, cnt < MOBA_TOPK), 1.0, sel)
    return sel


def _moba_full_kernel(q_ref, k_ref, v_ref, bias_ref, o_ref, k16_ref, vt16_ref):
    t_len = q_ref.shape[1]
    nb = t_len // MOBA_BLOCK
    blk = MOBA_BLOCK
    scale = HEAD_DIM ** -0.5
    k2 = k_ref[0]
    k16_ref[...] = k2.astype(bf16)
    vt16_ref[...] = v_ref[0].T.astype(bf16)
    kmean16 = jnp.mean(k2.reshape(nb, blk, LANES), axis=1).astype(bf16)
    lane = lax.broadcasted_iota(jnp.int32, (1, LANES), 1)
    hmask = [(lane // HEAD_DIM == hh).astype(f32) for hh in range(2)]
    krow = lax.broadcasted_iota(jnp.int32, (blk, blk), 0)
    qcol = lax.broadcasted_iota(jnp.int32, (blk, blk), 1)
    causal = krow <= qcol
    drow = lax.broadcasted_iota(jnp.int32, (LANES, blk), 0)

    for qb in range(nb):
        n_keys = (qb + 1) * blk
        q2 = q_ref[0, qb * blk:(qb + 1) * blk, :]
        k_all = k16_ref[0:n_keys, :]
        vt_all = vt16_ref[:, 0:n_keys]
        qm = [q2 * hmask[hh] for hh in range(2)]
        raw = [_dot_nt(k_all, (qm[hh] * scale).astype(bf16)) for hh in range(2)]
        ok = [causal, causal]
        if qb > 0:
            for hh in range(2):
                sel_t = _topk_rows(_dot_nt(kmean16, qm[hh].astype(bf16)), qb)
                past_ok = jnp.broadcast_to(sel_t[0:qb][:, None, :] > 0.5, (qb, blk, blk))
                ok[hh] = jnp.concatenate([past_ok.reshape(qb * blk, blk), causal], axis=0)
        probs, denom = [], []
        for hh in range(2):
            bias = bias_ref[hh, (nb - 1 - qb) * blk:nb * blk, :]
            logits = jnp.where(ok[hh], raw[hh] + bias, NEG_BIG)
            p = jnp.exp(logits - jnp.max(logits, axis=0, keepdims=True))
            denom.append(jnp.sum(p, axis=0, keepdims=True))
            probs.append(p.astype(bf16))
        outs = [_dot(vt_all, probs[hh]) for hh in range(2)]
        o_ref[0, qb * blk:(qb + 1) * blk, :] = jnp.where(drow < HEAD_DIM, outs[0] / denom[0],
                                                         outs[1] / denom[1]).T


def moba_full(q, k, v, bias_t):
    b, t, hw = q.shape
    n_pairs = hw // LANES
    qkv_spec = pl.BlockSpec((1, t, LANES), lambda p, i: (i, 0, p))
    return pl.pallas_call(
        _moba_full_kernel,
        grid=(n_pairs, b),
        in_specs=[qkv_spec, qkv_spec, qkv_spec,
                  pl.BlockSpec((2, t, MOBA_BLOCK), lambda p, i: (p, 0, 0))],
        out_specs=pl.BlockSpec((1, t, LANES), lambda p, i: (i, 0, p)),
        out_shape=jax.ShapeDtypeStruct((b, t, hw), f32),
        scratch_shapes=[pltpu.VMEM((t, LANES), bf16), pltpu.VMEM((LANES, t), bf16)],
        name="moba_full",
        compiler_params=_cparams(("arbitrary", "arbitrary")),
    )(q, k, v, bias_t)


PAGES_PER_STEP = 16


def _pair_queries(q2):
    lane = lax.broadcasted_iota(jnp.int32, (1, LANES), 1)
    return jnp.concatenate([q2 * (lane // HEAD_DIM == hh).astype(f32) for hh in range(2)], axis=0)


def _page_pair(page_ref, p):
    hd, page = page_ref.shape[3:]
    return page_ref[0, 0, 2 * p:2 * p + 2].reshape(2 * hd, page)


def _moba_paged_probs_kernel(pt_ref, q_ref, kn_ref, bias_ref, bown_ref, *refs, n_pairs):
    kp_refs = refs[:PAGES_PER_STEP]
    p_ref, pown_ref, logit_ref, bsum_ref = refs[PAGES_PER_STEP:]
    c = pl.program_id(1)
    n_steps = pl.num_programs(1)
    t_q = q_ref.shape[1]
    page = kp_refs[0].shape[4]
    pages_per_block = MOBA_BLOCK // page
    n_blocks = bsum_ref.shape[0]
    blocks_per_step = PAGES_PER_STEP // pages_per_block
    step_keys = PAGES_PER_STEP * page
    scale = HEAD_DIM ** -0.5

    qf = [_pair_queries(q_ref[0, :, p * LANES:(p + 1) * LANES]) for p in range(n_pairs)]
    q16 = [(x * scale).astype(bf16) for x in qf]
    for p in range(n_pairs):
        kts = [_page_pair(kp_refs[j], p) for j in range(PAGES_PER_STEP)]
        logit_ref[p, c] = _dot(q16[p], jnp.concatenate(kts, axis=1).astype(bf16))
        for jb in range(blocks_per_step):
            ksum = kts[jb * pages_per_block]
            for j in range(jb * pages_per_block + 1, (jb + 1) * pages_per_block):
                ksum = ksum + kts[j]
            bsum_ref[c * blocks_per_step + jb, p] = ksum

    @pl.when(c == n_steps - 1)
    def _():
        n_keys = n_steps * step_keys
        blk_of_key = lax.broadcasted_iota(jnp.int32, (n_blocks, n_keys), 1) // MOBA_BLOCK
        expand = (blk_of_key == lax.broadcasted_iota(jnp.int32, (n_blocks, n_keys), 0)).astype(bf16)
        rq = lax.broadcasted_iota(jnp.int32, (2 * t_q, LANES), 0) % t_q
        ck = lax.broadcasted_iota(jnp.int32, (2 * t_q, LANES), 1)
        own_ok = ck <= rq
        zpad = jnp.zeros((LANES - t_q, LANES), f32)
        blk_lane = lax.broadcasted_iota(jnp.int32, (LANES, LANES), 1)
        for p in range(n_pairs):
            lanes = slice(p * LANES, (p + 1) * LANES)
            kmean_t = jnp.zeros((LANES, LANES), f32)
            for blk in range(n_blocks):
                col = jnp.sum(bsum_ref[blk, p], axis=1, keepdims=True) * (1.0 / MOBA_BLOCK)
                kmean_t = jnp.where(blk_lane == blk, col, kmean_t)
            scores = _dot(qf[p].astype(bf16), kmean_t.astype(bf16))
            sel_t = _topk_rows(scores.T[0:n_blocks], n_blocks)
            chosen = _dot(sel_t.T.astype(bf16), expand)
            lg = [jnp.where(chosen[:, s * step_keys:(s + 1) * step_keys] > 0.5,
                            logit_ref[p, s] + bias_ref[p, s], NEG_BIG) for s in range(n_steps)]
            kn = jnp.concatenate([kn_ref[0, :, lanes], zpad], axis=0).astype(bf16)
            own = jnp.where(own_ok, _dot_nt(q16[p], kn) + bown_ref[p], NEG_BIG)
            m = jnp.max(own, axis=1, keepdims=True)
            for x in lg:
                m = jnp.maximum(m, jnp.max(x, axis=1, keepdims=True))
            e_own = jnp.exp(own - m)
            e = [jnp.exp(x - m) for x in lg]
            denom = jnp.sum(e_own, axis=1, keepdims=True)
            for x in e:
                denom = denom + jnp.sum(x, axis=1, keepdims=True)
            inv = 1.0 / denom
            for s in range(n_steps):
                p_ref[0, p, s] = e[s] * inv
            pown_ref[0, p] = e_own * inv


def _moba_paged_mix_kernel(pt_ref, p_ref, pown_ref, vn_ref, *refs, n_pairs):
    vp_refs = refs[:PAGES_PER_STEP]
    o_ref, acc_ref = refs[PAGES_PER_STEP:]
    c = pl.program_id(1)
    n_steps = pl.num_programs(1)
    t_q = vn_ref.shape[1]
    page = vp_refs[0].shape[4]

    @pl.when(c == 0)
    def _():
        acc_ref[...] = jnp.zeros_like(acc_ref)

    for p in range(n_pairs):
        vt = jnp.concatenate([_page_pair(vp_refs[j], p) for j in range(PAGES_PER_STEP)], axis=1)
        acc_ref[p] += _dot_nt(p_ref[0, p, 0].astype(bf16), vt.astype(bf16))

    @pl.when(c == n_steps - 1)
    def _():
        lane = lax.broadcasted_iota(jnp.int32, (1, LANES), 1)
        zpad = jnp.zeros((LANES - t_q, LANES), f32)
        for p in range(n_pairs):
            lanes = slice(p * LANES, (p + 1) * LANES)
            vn = jnp.concatenate([vn_ref[0, :, lanes], zpad], axis=0).astype(bf16)
            acc = acc_ref[p] + _dot(pown_ref[0, p].astype(bf16), vn)
            o_ref[0, :, lanes] = jnp.where(lane < HEAD_DIM, acc[0:t_q], acc[t_q:2 * t_q])


def moba_paged(q, k_new, v_new, cache_k, cache_v, layer, page_table, bias_past, bias_own):
    b, t_q, hw = q.shape
    n_pairs = hw // LANES
    n_pages = page_table.shape[1]
    n_heads, hd, page = cache_k.shape[2:]
    assert page == LANES and hd == HEAD_DIM and n_heads * hd == hw
    assert n_pages % PAGES_PER_STEP == 0 and PAGES_PER_STEP % (MOBA_BLOCK // page) == 0
    n_steps = n_pages // PAGES_PER_STEP
    step_keys = PAGES_PER_STEP * page
    rows = 2 * t_q

    def page_spec(j):
        return pl.BlockSpec((1, 1, n_heads, hd, page),
                            lambda i, c, pt: (layer, pt[i, c * PAGES_PER_STEP + j], 0, 0, 0))

    new_spec = pl.BlockSpec((1, t_q, hw), lambda i, c, pt: (i, 0, 0))
    own_spec = pl.BlockSpec((1, n_pairs, rows, LANES), lambda i, c, pt: (i, 0, 0, 0))
    probs, p_own = pl.pallas_call(
        functools.partial(_moba_paged_probs_kernel, n_pairs=n_pairs),
        grid_spec=pltpu.PrefetchScalarGridSpec(
            num_scalar_prefetch=1,
            grid=(b, n_steps),
            in_specs=[new_spec, new_spec,
                      pl.BlockSpec(bias_past.shape, lambda i, c, pt: (0, 0, 0, 0)),
                      pl.BlockSpec(bias_own.shape, lambda i, c, pt: (0, 0, 0))]
                     + [page_spec(j) for j in range(PAGES_PER_STEP)],
            out_specs=[pl.BlockSpec((1, n_pairs, n_steps, rows, step_keys), lambda i, c, pt: (i, 0, 0, 0, 0)),
                       own_spec],
            scratch_shapes=[pltpu.VMEM((n_pairs, n_steps, rows, step_keys), f32),
                            pltpu.VMEM((n_pages * page // MOBA_BLOCK, n_pairs, LANES, page), f32)]),
        out_shape=[jax.ShapeDtypeStruct((b, n_pairs, n_steps, rows, step_keys), f32),
                   jax.ShapeDtypeStruct((b, n_pairs, rows, LANES), f32)],
        name="moba_paged_probs",
        compiler_params=_cparams(("arbitrary", "arbitrary")),
    )(page_table, q, k_new, bias_past, bias_own, *([cache_k] * PAGES_PER_STEP))
    return pl.pallas_call(
        functools.partial(_moba_paged_mix_kernel, n_pairs=n_pairs),
        grid_spec=pltpu.PrefetchScalarGridSpec(
            num_scalar_prefetch=1,
            grid=(b, n_steps),
            in_specs=[pl.BlockSpec((1, n_pairs, 1, rows, step_keys), lambda i, c, pt: (i, 0, c, 0, 0)),
                      own_spec, new_spec]
                     + [page_spec(j) for j in range(PAGES_PER_STEP)],
            out_specs=new_spec,
            scratch_shapes=[pltpu.VMEM((n_pairs, rows, LANES), f32)]),
        out_shape=jax.ShapeDtypeStruct((b, t_q, hw), f32),
        name="moba_paged_mix",
        compiler_params=_cparams(("arbitrary", "arbitrary")),
    )(page_table, probs, p_own, v_new, *([cache_v] * PAGES_PER_STEP))


def paged_bias_layout(bias_rows_arr, n_past):
    h, t_q, _ = bias_rows_arr.shape
    step_keys = PAGES_PER_STEP * LANES
    n_steps = n_past // step_keys
    past = bias_rows_arr[:, :, :n_past].reshape(h // 2, 2 * t_q, n_steps, step_keys)
    own = bias_rows_arr[:, :, n_past:].reshape(h // 2, 2 * t_q, LANES)
    return past.transpose(0, 2, 1, 3), own


def _rwkv_prep_kernel(p_ref, prev_ref, shift_ref, mu_ref, w0_ref, a0_ref, kk_ref, ka_ref, rk_ref,
                      ww_ref, wa_ref, wg_ref, *o_refs, hw):
    bt, tt, n = p_ref.shape
    j = pl.program_id(1)
    p = p_ref[...].reshape(bt * tt, n)
    first = jnp.where(j == 0, shift_ref[...], prev_ref[:, 7:8, :])
    first = jnp.broadcast_to(first, (bt, tt, n)).reshape(bt * tt, n)
    row = lax.broadcasted_iota(jnp.int32, (bt * tt, 1), 0)
    prev = jnp.where(row % tt == 0, first, pltpu.roll(p, 1, 0))
    xs = p + (prev - p) * mu_ref[...]
    r = xs[:, 0:hw]
    k = xs[:, hw:2 * hw]
    v = xs[:, 2 * hw:3 * hw]
    xwa = xs[:, 3 * hw:3 * hw + LANES]
    xg = xs[:, 3 * hw + LANES:3 * hw + 2 * LANES]
    z = -(w0_ref[...] + _dot_f32(jnp.tanh(xwa), ww_ref[...]))
    softplus = jnp.maximum(z, 0.0) + jnp.log(1.0 + jnp.exp(-jnp.abs(z)))
    log_decay = -jnp.exp(-softplus - 0.5)
    a = _sigmoid(a0_ref[...] + _dot_f32(xwa, wa_ref[...]))
    g = _dot_f32(_sigmoid(xg), wg_ref[...])
    mseg = _seg_sum_matrix(hw, HEAD_DIM)
    kk = k * kk_ref[...]
    kk = kk / jnp.maximum(jnp.sqrt(_seg_reduce(kk * kk, mseg)), 1e-12)
    k2 = k * (1.0 + (a - 1.0) * ka_ref[...])
    bonus = _seg_reduce(r * k2 * rk_ref[...], mseg) * v
    outs = (r, log_decay, k2, v, -kk, kk * a, g, bonus)
    for o_ref, val in zip(o_refs, outs):
        o_ref[...] = val.reshape(bt, tt, hw)


def rwkv_prep(prw, shift0, prm, bt, tt):
    b, t, n = prw.shape
    hw = prm['w0'].shape[-1]
    assert n == 3 * hw + 2 * LANES and tt % 8 == 0
    row = lambda x: x.reshape(1, -1)
    vec = lambda: pl.BlockSpec((1, hw), lambda i, j: (0, 0))
    mat = lambda: pl.BlockSpec((LANES, hw), lambda i, j: (0, 0))
    prev_blk = tt // 8
    return pl.pallas_call(
        functools.partial(_rwkv_prep_kernel, hw=hw),
        grid=(b // bt, t // tt),
        in_specs=[pl.BlockSpec((bt, tt, n), lambda i, j: (i, j, 0)),
                  pl.BlockSpec((bt, 8, n), lambda i, j: (i, jnp.maximum(j * prev_blk - 1, 0), 0)),
                  pl.BlockSpec((bt, 1, n), lambda i, j: (i, 0, 0)),
                  pl.BlockSpec((1, n), lambda i, j: (0, 0)),
                  vec(), vec(), vec(), vec(), vec(), mat(), mat(), mat()],
        out_specs=[pl.BlockSpec((bt, tt, hw), lambda i, j: (i, j, 0))] * 8,
        out_shape=[jax.ShapeDtypeStruct((b, t, hw), f32)] * 8,
        name="rwkv_prep",
        compiler_params=_cparams(("arbitrary", "arbitrary")),
    )(prw, prw, shift0.reshape(b, 1, n), row(prm['mu']), row(prm['w0']), row(prm['a0']),
      row(prm['k_k']), row(prm['k_a']), row(prm['r_k']), prm['w_w2p'], prm['w_a2p'], prm['w_g2'])


def rwkv_prep_params(mu, w0, w_w2, a0, w_a2, w_g2, k_k, k_a, r_k):
    hw = w0.shape[-1]
    zeros = jnp.zeros((LANES - DECAY_LORA, hw), w_w2.dtype)
    return dict(mu=mu, w0=w0, a0=a0, k_k=k_k, k_a=k_a, r_k=r_k.reshape(-1),
                w_w2p=jnp.concatenate([w_w2, zeros], axis=0),
                w_a2p=jnp.concatenate([zeros, w_a2], axis=0), w_g2=w_g2)


def _split3(x):
    hi = x.astype(bf16)
    return hi, (x - hi.astype(f32)).astype(bf16)


def _dot3(a, b):
    ah, al = _split3(a)
    bh, bl = _split3(b)
    return _dot(ah, bh) + (_dot(ah, bl) + _dot(al, bh))


def _dot3_nt(a, b):
    ah, al = _split3(a)
    bh, bl = _split3(b)
    return _dot_nt(ah, bh) + (_dot_nt(ah, bl) + _dot_nt(al, bh))


def _dot3_tn(a, b):
    return _dot3(a.T, b)


def _rwkv_chunk_maps_kernel(r_ref, lw_ref, k_ref, v_ref, x_ref, b_ref, m_ref, n_ref, p_ref, q_ref,
                            *, n_pairs, c_len):
    two_c = 2 * c_len
    rr = lax.broadcasted_iota(jnp.int32, (two_c, two_c), 0)
    cc = lax.broadcasted_iota(jnp.int32, (two_c, two_c), 1)
    same_head = (rr // c_len) == (cc // c_len)
    strict = jnp.logical_and(same_head, (cc % c_len) < (rr % c_len))
    incl = jnp.logical_and(same_head, (cc % c_len) <= (rr % c_len))
    eye = (rr == cc).astype(f32)
    row_head = lax.broadcasted_iota(jnp.int32, (two_c, LANES), 0) // c_len
    lane_head = lax.broadcasted_iota(jnp.int32, (two_c, LANES), 1) // HEAD_DIM
    bd = row_head == lane_head
    eye_k = (lax.broadcasted_iota(jnp.int32, (LANES, LANES), 0)
             == lax.broadcasted_iota(jnp.int32, (LANES, LANES), 1)).astype(f32)
    trow = lax.broadcasted_iota(jnp.int32, (c_len, LANES), 0)
    n_doubling = max(1, (c_len - 1).bit_length())

    def stack(a):
        return jnp.where(bd, jnp.concatenate([a, a], axis=0), 0.0)

    jobs = [(bi, ci, p) for bi in range(r_ref.shape[0]) for ci in range(r_ref.shape[1] // c_len)
            for p in range(n_pairs)]
    ops = []
    for bi, ci, p in jobs:
        rows = slice(ci * c_len, (ci + 1) * c_len)
        lanes = slice(p * LANES, (p + 1) * LANES)
        lw = lw_ref[bi, rows, lanes]
        cum = lw
        sh = 1
        while sh < c_len:
            cum = cum + jnp.where(trow >= sh, pltpu.roll(cum, sh, 0), 0.0)
            sh *= 2
        gam = jnp.exp(cum)
        inv_gam = jnp.exp(-cum)
        ops.append(dict(
            a=stack(x_ref[bi, rows, lanes] * jnp.exp(cum - lw)), r=stack(r_ref[bi, rows, lanes] * gam),
            b=stack(b_ref[bi, rows, lanes] * inv_gam), k=stack(k_ref[bi, rows, lanes] * inv_gam),
            v=stack(v_ref[bi, rows, lanes]), g_end=gam[c_len - 1:c_len, :]))
    for o in ops:
        bk = jnp.concatenate([o['b'], o['k']], axis=0)
        gram_a = _dot3_nt(o['a'], bk)
        gram_r = _dot_nt_f32(o['r'], bk)
        o['l_ab'] = jnp.where(strict, gram_a[:, 0:two_c], 0.0)
        o['l_ak'] = jnp.where(strict, gram_a[:, two_c:2 * two_c], 0.0)
        o['m_rb'] = jnp.where(incl, gram_r[:, 0:two_c], 0.0)
        o['m_rk'] = jnp.where(incl, gram_r[:, two_c:2 * two_c], 0.0)
        o['pw'] = o['l_ab']
        o['u'] = o['l_ab']
    for it in range(n_doubling - 1):
        mm = _dot3 if it == 0 else _dot_f32
        for o in ops:
            o['pw'] = mm(o['pw'], o['pw'])
        for o in ops:
            o['u'] = o['u'] + o['pw'] + mm(o['u'], o['pw'])
    for o in ops:
        o['lkv'] = _dot3(o['l_ak'], o['v'])
    for o in ops:
        o['pq1'] = _dot3(eye + o['u'], jnp.concatenate([o['a'], o['lkv']], axis=1))
    for o in ops:
        o['pq2'] = _dot_f32(o['m_rb'], o['pq1'])
        o['mkv'] = _dot_f32(o['m_rk'], o['v'])
    for o in ops:
        o['mn'] = _dot3_tn(o['pq1'], o['b'] * o['g_end'])
        o['vk'] = _dot3_tn(o['v'], o['k'] * o['g_end'])
    for (bi, ci, p), o in zip(jobs, ops):
        p_ref[bi, ci, p] = o['r'] + o['pq2'][:, 0:LANES]
        q_ref[bi, ci, p] = o['pq2'][:, LANES:2 * LANES] + o['mkv']
        m_ref[bi, ci, p] = eye_k * o['g_end'] + o['mn'][0:LANES]
        n_ref[bi, ci, p] = o['mn'][LANES:2 * LANES] + o['vk']


def _rwkv_state_kernel(m_ref, n_ref, p_ref, q_ref, g_ref, bonus_ref, s0_ref, lng_ref, lnb_ref,
                       y_ref, sout_ref, state_ref, *, n_pairs, c_len):
    jc = pl.program_id(1)
    n_steps = pl.num_programs(1)

    @pl.when(jc == 0)
    def _():
        state_ref[...] = s0_ref[0]

    mseg = _seg_mean_matrix(LANES, HEAD_DIM)
    states = [state_ref[p] for p in range(n_pairs)]
    for ci in range(m_ref.shape[1]):
        rows = slice(ci * c_len, (ci + 1) * c_len)
        outs = [_dot3_nt(p_ref[0, ci, p], states[p]) + q_ref[0, ci, p] for p in range(n_pairs)]
        states = [_dot3(states[p], m_ref[0, ci, p]) + n_ref[0, ci, p] for p in range(n_pairs)]
        for p in range(n_pairs):
            lanes = slice(p * LANES, (p + 1) * LANES)
            o = outs[p][0:c_len] + outs[p][c_len:2 * c_len]
            mu = _seg_reduce(o, mseg)
            cen = o - mu
            var = _seg_reduce(cen * cen, mseg)
            y = cen * lax.rsqrt(var + GN_EPS) * lng_ref[:, lanes] + lnb_ref[:, lanes]
            y_ref[0, rows, lanes] = (y + bonus_ref[0, rows, lanes]) * g_ref[0, rows, lanes]
    for p in range(n_pairs):
        state_ref[p] = states[p]

    @pl.when(jc == n_steps - 1)
    def _():
        sout_ref[0] = state_ref[...]


def rwkv_chunked(seq, s0_bd, lnx_g, lnx_b, c_len, maps_chunks, scan_chunks, maps_batch):
    r, lw, k2, v, x, kka, g, bonus = seq
    b, t, hw = r.shape
    n_pairs = hw // LANES
    n_chunks = t // c_len
    assert t % c_len == 0 and n_chunks % maps_chunks == 0 and n_chunks % scan_chunks == 0
    assert b % maps_batch == 0
    two_c = 2 * c_len

    def map_spec(rows, per_step, n_seq=1):
        return pl.BlockSpec((n_seq, per_step, n_pairs, rows, LANES), lambda i, j: (i, j, 0, 0, 0))

    def map_shape(rows):
        return jax.ShapeDtypeStruct((b, n_chunks, n_pairs, rows, LANES), f32)

    seq_spec = lambda per_step, n_seq=1: pl.BlockSpec((n_seq, per_step * c_len, hw), lambda i, j: (i, j, 0))
    maps = pl.pallas_call(
        functools.partial(_rwkv_chunk_maps_kernel, n_pairs=n_pairs, c_len=c_len),
        grid=(b // maps_batch, n_chunks // maps_chunks),
        in_specs=[seq_spec(maps_chunks, maps_batch)] * 6,
        out_specs=[map_spec(LANES, maps_chunks, maps_batch), map_spec(LANES, maps_chunks, maps_batch),
                   map_spec(two_c, maps_chunks, maps_batch), map_spec(two_c, maps_chunks, maps_batch)],
        out_shape=[map_shape(LANES), map_shape(LANES), map_shape(two_c), map_shape(two_c)],
        name="rwkv_chunk_maps",
        compiler_params=_cparams(("arbitrary", "arbitrary")),
    )(r, lw, k2, v, x, kka)
    st_spec = pl.BlockSpec((1, n_pairs, LANES, LANES), lambda i, j: (i, 0, 0, 0))
    vec = pl.BlockSpec((1, hw), lambda i, j: (0, 0))
    return pl.pallas_call(
        functools.partial(_rwkv_state_kernel, n_pairs=n_pairs, c_len=c_len),
        grid=(b, n_chunks // scan_chunks),
        in_specs=[map_spec(LANES, scan_chunks), map_spec(LANES, scan_chunks),
                  map_spec(two_c, scan_chunks), map_spec(two_c, scan_chunks),
                  seq_spec(scan_chunks), seq_spec(scan_chunks), st_spec, vec, vec],
        out_specs=[seq_spec(scan_chunks), st_spec],
        out_shape=[jax.ShapeDtypeStruct((b, t, hw), f32),
                   jax.ShapeDtypeStruct((b, n_pairs, LANES, LANES), f32)],
        scratch_shapes=[pltpu.VMEM((n_pairs, LANES, LANES), f32)],
        name="rwkv_state_scan",
        compiler_params=_cparams(("arbitrary", "arbitrary")),
    )(*maps, g, bonus, s0_bd, lnx_g.reshape(1, hw), lnx_b.reshape(1, hw))


def state_to_blockdiag(s):
    b, h, n, _ = s.shape
    sp = s.reshape(b, h // 2, 2, n, n)
    z = jnp.zeros_like(sp[:, :, 0])
    top = jnp.concatenate([sp[:, :, 0], z], axis=-1)
    bot = jnp.concatenate([z, sp[:, :, 1]], axis=-1)
    return jnp.concatenate([top, bot], axis=-2)


def state_from_blockdiag(sbd):
    b, hp, _, _ = sbd.shape
    n = HEAD_DIM
    return jnp.stack([sbd[:, :, :n, :n], sbd[:, :, n:, n:]], axis=2).reshape(b, hp * 2, n, n)


def _layer_norm(z, g, b):
    mu = jnp.mean(z, axis=-1, keepdims=True)
    c = z - mu
    var = jnp.mean(c * c, axis=-1, keepdims=True)
    return c * lax.rsqrt(var + LN_EPS) * g + b


def _route(sel, aff, n_groups, per_group, top_k):
    rows = sel.shape[1]
    srow = [sel[e:e + 1, :] for e in range(n_groups * per_group)]
    arow = [aff[e:e + 1, :] for e in range(n_groups * per_group)]
    scores = []
    for g in range(n_groups):
        s = srow[g * per_group:(g + 1) * per_group]
        best = None
        for i in range(per_group):
            for j in range(i + 1, per_group):
                pair = s[i] + s[j]
                best = pair if best is None else jnp.maximum(best, pair)
        scores.append(best)
    top = scores[0]
    for g in range(1, n_groups):
        top = jnp.maximum(top, scores[g])
    taken = jnp.zeros((1, rows), jnp.bool_)
    eidx = lax.broadcasted_iota(jnp.int32, (n_groups * per_group, rows), 0)
    gates = jnp.zeros((n_groups * per_group, rows), f32)
    for g in range(n_groups):
        is_grp = jnp.logical_and(scores[g] == top, jnp.logical_not(taken))
        taken = jnp.logical_or(taken, is_grp)
        s = srow[g * per_group:(g + 1) * per_group]
        a = arow[g * per_group:(g + 1) * per_group]
        chosen = []
        for i in range(per_group):
            cnt = jnp.zeros((1, rows), f32)
            for j in range(per_group):
                if j != i:
                    ahead = (s[j] > s[i]) if j > i else (s[j] >= s[i])
                    cnt = cnt + ahead.astype(f32)
            chosen.append(jnp.logical_and(cnt < top_k, is_grp))
        denom = jnp.zeros((1, rows), f32)
        for i in range(per_group):
            denom = denom + jnp.where(chosen[i], a[i], 0.0)
        for i in range(per_group):
            gate = jnp.where(chosen[i], a[i] / denom, 0.0)
            gates = jnp.where(eidx == g * per_group + i, gate, gates)
    return gates


def _outproj_kernel(x_ref, ya_ref, yb_ref, yc_ref, mod_ref, w_ref, g_ref, b_ref, wr_ref, rb_ref,
                    x1_ref, h2_ref, gates_ref, *, d, alpha):
    bt, tt, _ = x_ref.shape
    rows = bt * tt
    off = 0
    mix = jnp.zeros((rows, d), f32)
    for y_ref in (ya_ref, yb_ref, yc_ref):
        wd = y_ref.shape[2]
        mix = mix + _dot(y_ref[...].reshape(rows, wd).astype(bf16), w_ref[off:off + wd, :])
        off += wd
    g1 = mod_ref[:, :, 2 * d:3 * d]
    z = alpha * x_ref[...] + (1.0 + g1) * mix.reshape(bt, tt, d)
    x1 = _layer_norm(z, g_ref[...], b_ref[...])
    x1_ref[...] = x1
    sh2 = mod_ref[:, :, 3 * d:4 * d]
    sc2 = mod_ref[:, :, 4 * d:5 * d]
    h2 = (x1 * (1.0 + sc2) + sh2).reshape(rows, d)
    h2_ref[...] = h2.astype(bf16).reshape(bt, tt, d)
    aff = _sigmoid(_dot_nt_f32(wr_ref[...], h2))
    gates_ref[...] = _route(aff + rb_ref[...], aff, N_EXPERT_GROUPS,
                            wr_ref.shape[0] // N_EXPERT_GROUPS, TOP_K_EXPERTS)


def out_projection(x, ya, yb, yc, mod, w_out_bf16, ln_g, ln_b, w_router_t, router_bias, alpha, bt, tt):
    b, t, d = x.shape
    n_exp = w_router_t.shape[0]
    rows = bt * tt
    blk = lambda wd: pl.BlockSpec((bt, tt, wd), lambda i, j: (i, j, 0))
    full = lambda a: pl.BlockSpec(a.shape, lambda i, j: (0,) * a.ndim)
    n_tblk = t // tt
    ln_g = ln_g.reshape(1, d)
    ln_b = ln_b.reshape(1, d)
    rb = router_bias.reshape(n_exp, 1)
    return pl.pallas_call(
        functools.partial(_outproj_kernel, d=d, alpha=alpha),
        grid=(b // bt, n_tblk),
        in_specs=[blk(d), blk(ya.shape[2]), blk(yb.shape[2]), blk(yc.shape[2]),
                  pl.BlockSpec((bt, 1, mod.shape[2]), lambda i, j: (i, 0, 0)),
                  full(w_out_bf16), full(ln_g), full(ln_b), full(w_router_t), full(rb)],
        out_specs=[blk(d), blk(d), pl.BlockSpec((n_exp, rows), lambda i, j: (0, i * n_tblk + j))],
        out_shape=[jax.ShapeDtypeStruct((b, t, d), f32), jax.ShapeDtypeStruct((b, t, d), bf16),
                   jax.ShapeDtypeStruct((n_exp, b * t), f32)],
        name="out_projection",
        compiler_params=_cparams(("arbitrary", "arbitrary")),
    )(x, ya, yb, yc, mod, w_out_bf16, ln_g, ln_b, w_router_t, rb)


def _moe_kernel(h_ref, gates_ref, x1_ref, mod_ref, wg_ref, wu_ref, wd_ref, g_ref, b_ref,
                o_ref, *, d, alpha):
    bt, tt, _ = h_ref.shape
    rows = bt * tt
    n_exp, f, _ = wd_ref.shape
    h = h_ref[...].reshape(rows, d)
    gates = gates_ref[...].reshape(rows, gates_ref.shape[2])
    lane = lax.broadcasted_iota(jnp.int32, gates.shape, 1)
    acts = []
    for e in range(n_exp):
        gate = jnp.sum(jnp.where(lane == e, gates, 0.0), axis=1, keepdims=True)
        acts.append((_silu(_dot(h, wg_ref[e])) * _dot(h, wu_ref[e]) * gate).astype(bf16))
    ffn = _dot(jnp.concatenate(acts, axis=1), wd_ref[...].reshape(n_exp * f, d))
    g2 = mod_ref[:, :, 5 * d:6 * d]
    z = alpha * x1_ref[...] + (1.0 + g2) * ffn.reshape(bt, tt, d)
    o_ref[...] = _layer_norm(z, g_ref[...], b_ref[...])


def moe_ffn(h2, gates, x1, mod, wg_bf16, wu_bf16, wd_bf16, ln_g, ln_b, alpha, bt, tt):
    b, t, d = x1.shape
    n_exp = wg_bf16.shape[0]
    f = wg_bf16.shape[2]
    blk = lambda wd: pl.BlockSpec((bt, tt, wd), lambda i, j: (i, j, 0))
    resident = lambda a: pl.BlockSpec(a.shape, lambda i, j: (0, 0, 0), pipeline_mode=pl.Buffered(1))
    return pl.pallas_call(
        functools.partial(_moe_kernel, d=d, alpha=alpha),
        grid=(b // bt, t // tt),
        in_specs=[blk(d), blk(n_exp), blk(d),
                  pl.BlockSpec((bt, 1, mod.shape[2]), lambda i, j: (i, 0, 0)),
                  resident(wg_bf16), resident(wu_bf16), resident(wd_bf16),
                  pl.BlockSpec((1, d), lambda i, j: (0, 0)),
                  pl.BlockSpec((1, d), lambda i, j: (0, 0))],
        out_specs=blk(d),
        out_shape=jax.ShapeDtypeStruct((b, t, d), f32),
        name="moe_ffn",
        compiler_params=_cparams(("arbitrary", "arbitrary")),
    )(h2, gates, x1, mod, wg_bf16, wu_bf16, wd_bf16, ln_g.reshape(1, d), ln_b.reshape(1, d))


def sgu_full_params(w_s, b_s):
    return w_s, jnp.repeat(b_s.T, HEAD_DIM, axis=1)


def sgu_short_params(w_s, b_s, t_len):
    reps = SGU_CHUNK // t_len
    eye = jnp.eye(reps, dtype=w_s.dtype)
    w_small = w_s[:, :t_len, :t_len]
    w_big = jnp.einsum('ab,gts->gatbs', eye, w_small).reshape(w_s.shape[0], SGU_CHUNK, SGU_CHUNK)
    bias = jnp.tile(jnp.repeat(b_s[:, :t_len].T, HEAD_DIM, axis=1), (reps, 1))
    return w_big, bias


PROMPT_ROWS = 512
RWKV_CHUNK = (64, 4, 4, 1)


def _hybrid_layer(x, mod, tile, wts, attend, sgu_prm, shift0, s0_bd, rwkv_chunk, emit_sgu_v, alpha):
    bt, tt = tile
    b, t, d = x.shape
    hw_sgu = sgu_prm[0].shape[0] * HEAD_DIM
    hw_attn = wts['attn_width']
    widths = (2 * hw_sgu, hw_attn, hw_attn, hw_attn, wts['w_in'].shape[1] - 2 * hw_sgu - 3 * hw_attn)
    puv, q, k, v, prw = in_projection(x, mod, wts['w_in'], widths, bt, tt)
    n_rows = b * t
    sgu_rows = min(PROMPT_ROWS, n_rows) if t < SGU_CHUNK else tt
    sgu_in = puv.reshape(1, n_rows, 2 * hw_sgu) if t < SGU_CHUNK else puv
    sgu_out = sgu(sgu_in, sgu_prm[0], sgu_prm[1], wts['sgu_ln_g'], wts['sgu_ln_b'], sgu_rows, emit_sgu_v)
    ya = sgu_out[0].reshape(b, t, hw_sgu)
    sgu_v = sgu_out[1].reshape(b, t, hw_sgu) if emit_sgu_v else None
    yb = attend(q, k, v)
    seq = rwkv_prep(prw, shift0, wts['rwkv'], bt, tt)
    yc, s_bd = rwkv_chunked(seq, s0_bd, wts['lnx_g'], wts['lnx_b'], *rwkv_chunk)
    x1, h2, gates_t = out_projection(x, ya, yb, yc, mod, wts['w_out'], wts['ln1_g'], wts['ln1_b'],
                                     wts['w_router_t'], wts['router_bias'], alpha, bt, tt)
    gates = gates_t.T.reshape(b, t, gates_t.shape[0])
    x2 = moe_ffn(h2, gates, x1, mod, wts['moe_wg'], wts['moe_wu'], wts['moe_wd'],
                 wts['ln2_g'], wts['ln2_b'], alpha, bt, tt)
    return x2, k, v, state_from_blockdiag(s_bd), prw[:, -1], sgu_v


def kernel(x_prompt, x_sample, cache_k, cache_v, state_rwkv, state_shift, page_table, c_prompt, c_sample, w_ada, b_ada, w_in, w_out, sgu_w, sgu_b, sgu_ln_g, sgu_ln_b, rel_bias, rwkv_mu, rwkv_w0, rwkv_w_w2, rwkv_a0, rwkv_w_a2, rwkv_w_g2, rwkv_k_k, rwkv_k_a, rwkv_r_k, rwkv_lnx_g, rwkv_lnx_b, ln1_g, ln1_b, ln2_g, ln2_b, w_router, router_bias, moe_w_gate, moe_w_up, moe_w_down):
    depth = w_ada.shape[0]
    bp, t_p, d = x_prompt.shape
    bs, t_s, _ = x_sample.shape
    n_heads = cache_k.shape[3]
    hw_attn = n_heads * HEAD_DIM
    n_pool, page = cache_k.shape[1], cache_k.shape[2]
    past_len = page_table.shape[1] * page
    rwkv_in = state_shift.shape[2]
    alpha = (2 * depth) ** 0.25

    mods = ada_modulation(jnp.concatenate([c_prompt, c_sample], axis=0), w_ada, b_ada)
    bias_p = bias_tiles(rel_bias, n_heads, t_p // MOBA_BLOCK).reshape(n_heads, t_p, MOBA_BLOCK)
    bias_past, bias_own = paged_bias_layout(
        bias_rows(rel_bias, n_heads, past_len, t_s, past_len + LANES), past_len)

    def attend_paged(layer):
        def attend(q, k, v):
            return moba_paged(q, k, v, cache_kt, cache_vt, layer, page_table, bias_past, bias_own)
        return attend

    cache_kt = cache_k.transpose(0, 1, 3, 4, 2)
    cache_vt = cache_v.transpose(0, 1, 3, 4, 2)
    w_in16, w_out16 = w_in.astype(bf16), w_out.astype(bf16)
    wg16, wu16, wd16 = moe_w_gate.astype(bf16), moe_w_up.astype(bf16), moe_w_down.astype(bf16)
    w_router_t = w_router.T
    zero_shift = jnp.zeros((bp, rwkv_in), f32)
    zero_state = jnp.zeros((bp, n_heads // 2, LANES, LANES), f32)

    xp, xs = x_prompt, x_sample
    outs = [[] for _ in range(9)]
    for l in range(depth):
        wts = dict(
            attn_width=hw_attn, w_in=w_in16[l], w_out=w_out16[l], sgu_ln_g=sgu_ln_g[l], sgu_ln_b=sgu_ln_b[l],
            rwkv=rwkv_prep_params(rwkv_mu[l], rwkv_w0[l], rwkv_w_w2[l], rwkv_a0[l], rwkv_w_a2[l],
                                  rwkv_w_g2[l], rwkv_k_k[l], rwkv_k_a[l], rwkv_r_k[l]),
            lnx_g=rwkv_lnx_g[l], lnx_b=rwkv_lnx_b[l], ln1_g=ln1_g[l], ln1_b=ln1_b[l],
            ln2_g=ln2_g[l], ln2_b=ln2_b[l], w_router_t=w_router_t, router_bias=router_bias,
            moe_wg=wg16[l], moe_wu=wu16[l], moe_wd=wd16[l])
        mod_p = mods[l, :bp][:, None, :]
        mod_s = mods[l, bp:][:, None, :]
        xp, kp, vp, sp, shp, _ = _hybrid_layer(
            xp, mod_p, (1, PROMPT_ROWS), wts, lambda q, k, v: moba_full(q, k, v, bias_p),
            sgu_full_params(sgu_w[l], sgu_b[l]), zero_shift, zero_state, RWKV_CHUNK, False, alpha)
        xs, ks_, vs_, ss, shs, sgu_v = _hybrid_layer(
            xs, mod_s, (bs, t_s), wts, attend_paged(l),
            sgu_short_params(sgu_w[l], sgu_b[l], t_s), state_shift[l],
            state_to_blockdiag(state_rwkv[l]), (t_s, 1, 1, min(bs, 8)), True, alpha)
        heads = lambda a: a.reshape(a.shape[0], a.shape[1], n_heads, HEAD_DIM)
        for lst, val in zip(outs, (heads(kp), heads(vp), heads(ks_), heads(vs_), sp, ss, shp, shs, sgu_v)):
            lst.append(val)
    return (xp, xs) + tuple(jnp.stack(lst) for lst in outs)
```

```python
import functools
import math

import jax
import jax.numpy as jnp
import numpy as np
from jax import lax
from jax.experimental import pallas as pl
from jax.experimental.pallas import tpu as pltpu

HEAD_DIM = 64
LANES = 128
SGU_CHUNK = 128
MOBA_BLOCK = 256
MOBA_TOPK = 3
REL_BUCKETS = 32
REL_MAX_DIST = 1024
DECAY_LORA = 64
AAA_LORA = 64
GATE_LORA = 128
GN_EPS = 64e-5
LN_EPS = 1e-5
N_EXPERT_GROUPS = 4
TOP_K_EXPERTS = 2
VMEM_LIMIT = 56 * 1024 * 1024
NEG_BIG = -1e30

f32 = jnp.float32
bf16 = jnp.bfloat16


def _cparams(sem):
    return pltpu.CompilerParams(dimension_semantics=sem, vmem_limit_bytes=VMEM_LIMIT)


def _dot(a, b):
    return jnp.dot(a, b, preferred_element_type=f32)


def _dot_nt(a, b):
    return lax.dot_general(a, b, (((1,), (1,)), ((), ())), preferred_element_type=f32)


def _dot_f32(a, b):
    return _dot(a.astype(bf16), b.astype(bf16))


def _dot_nt_f32(a, b):
    return _dot_nt(a.astype(bf16), b.astype(bf16))


def _gelu(x):
    c = math.sqrt(2.0 / math.pi)
    return 0.5 * x * (1.0 + jnp.tanh(c * (x + 0.044715 * (x * x * x))))


def _sigmoid(x):
    return 1.0 / (1.0 + jnp.exp(-x))


def _silu(x):
    return x * _sigmoid(x)


def _t5_bias(dist, rel_ref, head):
    max_exact = REL_BUCKETS // 2
    d = jnp.maximum(dist, 0)
    df = jnp.maximum(d, 1).astype(f32)
    large = max_exact + (jnp.log(df / max_exact) / math.log(REL_MAX_DIST / max_exact)
                         * (REL_BUCKETS - max_exact)).astype(jnp.int32)
    large = jnp.minimum(large, REL_BUCKETS - 1)
    bucket = jnp.where(d < max_exact, d, large)
    out = jnp.zeros(dist.shape, f32)
    for r in range(REL_BUCKETS):
        out = jnp.where(bucket == r, rel_ref[r, head], out)
    return out


def _bias_tiles_kernel(rel_ref, o_ref):
    h = pl.program_id(0)
    delta = pl.num_programs(1) - 1 - pl.program_id(1)
    i = lax.broadcasted_iota(jnp.int32, (MOBA_BLOCK, MOBA_BLOCK), 0)
    j = lax.broadcasted_iota(jnp.int32, (MOBA_BLOCK, MOBA_BLOCK), 1)
    dist = delta * MOBA_BLOCK + j - i
    o_ref[0, 0] = _t5_bias(dist, rel_ref, h)


def bias_tiles(rel_bias, n_heads, n_blocks):
    return pl.pallas_call(
        _bias_tiles_kernel,
        grid=(n_heads, n_blocks),
        in_specs=[pl.BlockSpec(memory_space=pltpu.SMEM)],
        out_specs=pl.BlockSpec((1, 1, MOBA_BLOCK, MOBA_BLOCK), lambda h, d: (h, d, 0, 0)),
        out_shape=jax.ShapeDtypeStruct((n_heads, n_blocks, MOBA_BLOCK, MOBA_BLOCK), f32),
        name="bias_tiles",
        compiler_params=_cparams(("arbitrary", "arbitrary")),
    )(rel_bias)


def _bias_rows_kernel(rel_ref, o_ref, *, q_start, t_q):
    h = pl.program_id(0)
    n_keys = o_ref.shape[2]
    i = lax.broadcasted_iota(jnp.int32, (t_q, n_keys), 0)
    j = lax.broadcasted_iota(jnp.int32, (t_q, n_keys), 1)
    o_ref[0] = _t5_bias(q_start + i - j, rel_ref, h)


def bias_rows(rel_bias, n_heads, q_start, t_q, n_keys):
    return pl.pallas_call(
        functools.partial(_bias_rows_kernel, q_start=q_start, t_q=t_q),
        grid=(n_heads,),
        in_specs=[pl.BlockSpec(memory_space=pltpu.SMEM)],
        out_specs=pl.BlockSpec((1, t_q, n_keys), lambda h: (h, 0, 0)),
        out_shape=jax.ShapeDtypeStruct((n_heads, t_q, n_keys), f32),
        name="bias_rows",
        compiler_params=_cparams(("arbitrary",)),
    )(rel_bias)


def _ada_kernel(c_ref, w_ref, b_ref, o_ref):
    c = _silu(c_ref[...]).astype(bf16)
    o_ref[0] = _dot(c, w_ref[0].astype(bf16)) + b_ref[0]


def ada_modulation(c_all, w_ada, b_ada, tn=1536):
    depth, d, n = w_ada.shape
    rows = c_all.shape[0]
    return pl.pallas_call(
        _ada_kernel,
        grid=(depth, n // tn),
        in_specs=[pl.BlockSpec((rows, d), lambda l, j: (0, 0)),
                  pl.BlockSpec((1, d, tn), lambda l, j: (l, 0, j)),
                  pl.BlockSpec((1, 1, tn), lambda l, j: (l, 0, j))],
        out_specs=pl.BlockSpec((1, rows, tn), lambda l, j: (l, 0, j)),
        out_shape=jax.ShapeDtypeStruct((depth, rows, n), f32),
        name="ada_modulation",
        compiler_params=_cparams(("arbitrary", "arbitrary")),
    )(c_all, w_ada, b_ada.reshape(depth, 1, n))


def _inproj_kernel(x_ref, mod_ref, w_ref, *o_refs, d, widths):
    bt, tt, _ = x_ref.shape
    sh = mod_ref[:, :, 0:d]
    sc = mod_ref[:, :, d:2 * d]
    h = (x_ref[...] * (1.0 + sc) + sh).reshape(bt * tt, d).astype(bf16)
    off = 0
    for o_ref, wd in zip(o_refs, widths):
        o_ref[...] = _dot(h, w_ref[:, off:off + wd]).reshape(bt, tt, wd)
        off += wd


def in_projection(x, mod, w_in_bf16, widths, bt, tt):
    b, t, d = x.shape
    n = w_in_bf16.shape[1]
    assert sum(widths) == n and b % bt == 0 and t % tt == 0
    return pl.pallas_call(
        functools.partial(_inproj_kernel, d=d, widths=widths),
        grid=(b // bt, t // tt),
        in_specs=[pl.BlockSpec((bt, tt, d), lambda i, j: (i, j, 0)),
                  pl.BlockSpec((bt, 1, mod.shape[2]), lambda i, j: (i, 0, 0)),
                  pl.BlockSpec((d, n), lambda i, j: (0, 0))],
        out_specs=[pl.BlockSpec((bt, tt, wd), lambda i, j: (i, j, 0)) for wd in widths],
        out_shape=[jax.ShapeDtypeStruct((b, t, wd), f32) for wd in widths],
        name="in_projection",
        compiler_params=_cparams(("arbitrary", "arbitrary")),
    )(x, mod, w_in_bf16)


def _seg_mean_matrix(width, seg):
    r = lax.broadcasted_iota(jnp.int32, (width, width), 0) // seg
    c = lax.broadcasted_iota(jnp.int32, (width, width), 1) // seg
    return jnp.where(r == c, 1.0 / seg, 0.0).astype(bf16)


def _seg_sum_matrix(width, seg):
    r = lax.broadcasted_iota(jnp.int32, (width, width), 0) // seg
    c = lax.broadcasted_iota(jnp.int32, (width, width), 1) // seg
    return jnp.where(r == c, 1.0, 0.0).astype(bf16)


def _seg_reduce(x, m):
    hi = x.astype(bf16)
    lo = (x - hi.astype(f32)).astype(bf16)
    return _dot(hi, m) + _dot(lo, m)


def _sgu_kernel(x_ref, w_ref, bias_ref, g_ref, b_ref, *o_refs, n_chunks, emit_v):
    wd = x_ref.shape[2] // 2
    n_groups = wd // HEAD_DIM
    x = x_ref[0]
    u = _gelu(x[:, :wd])
    v = _gelu(x[:, wd:])
    m = _seg_mean_matrix(wd, HEAD_DIM)
    mu = _seg_reduce(v, m)
    c = v - mu
    var = _seg_reduce(c * c, m)
    vn = c * lax.rsqrt(var + LN_EPS) * g_ref[...] + b_ref[...]
    if emit_v:
        o_refs[1][0] = vn
    row = lax.broadcasted_iota(jnp.int32, (SGU_CHUNK, SGU_CHUNK), 0)
    col = lax.broadcasted_iota(jnp.int32, (SGU_CHUNK, SGU_CHUNK), 1)
    lane_grp = lax.broadcasted_iota(jnp.int32, (SGU_CHUNK, wd), 1) // HEAD_DIM
    w_tril = [jnp.where(col <= row, w_ref[g], 0.0).astype(bf16) for g in range(n_groups)]
    for ci in range(n_chunks):
        rows = slice(ci * SGU_CHUNK, (ci + 1) * SGU_CHUNK)
        vc = vn[rows]
        s = bias_ref[...]
        for g in range(n_groups):
            s = s + _dot(w_tril[g], jnp.where(lane_grp == g, vc, 0.0).astype(bf16))
        o_refs[0][0, rows, :] = u[rows] * s


def sgu(puv, w_s, bias_tile, ln_g, ln_b, rows_per_step, emit_v):
    g_, r_, w2 = puv.shape
    wd = w2 // 2
    assert r_ % rows_per_step == 0 and rows_per_step % SGU_CHUNK == 0
    n_out = 2 if emit_v else 1
    outs = pl.pallas_call(
        functools.partial(_sgu_kernel, n_chunks=rows_per_step // SGU_CHUNK, emit_v=emit_v),
        grid=(g_, r_ // rows_per_step),
        in_specs=[pl.BlockSpec((1, rows_per_step, w2), lambda i, j: (i, j, 0)),
                  pl.BlockSpec(w_s.shape, lambda i, j: (0, 0, 0)),
                  pl.BlockSpec(bias_tile.shape, lambda i, j: (0, 0)),
                  pl.BlockSpec((1, wd), lambda i, j: (0, 0)),
                  pl.BlockSpec((1, wd), lambda i, j: (0, 0))],
        out_specs=[pl.BlockSpec((1, rows_per_step, wd), lambda i, j: (i, j, 0))] * n_out,
        out_shape=[jax.ShapeDtypeStruct((g_, r_, wd), f32)] * n_out,
        name="sgu",
        compiler_params=_cparams(("arbitrary", "arbitrary")),
    )(puv, w_s, bias_tile, ln_g.reshape(1, wd), ln_b.reshape(1, wd))
    return outs


def _topk_rows(scores, n_valid):
    row = lax.broadcasted_iota(jnp.int32, scores.shape, 0)
    valid = row < n_valid
    sel = jnp.zeros(scores.shape, f32)
    for j in range(n_valid):
        sj = scores[j:j + 1, :]
        beats = jnp.logical_and(valid, jnp.logical_or(scores > sj, jnp.logical_and(scores == sj, row < j)))
        cnt = jnp.sum(beats.astype(f32), axis=0, keepdims=True)
        sel = jnp.where(jnp.logical_and(row == j, cnt < MOBA_TOPK), 1.0, sel)
    return sel


def _moba_full_kernel(q_ref, k_ref, v_ref, bias_ref, o_ref, k16_ref, vt16_ref):
    t_len = q_ref.shape[1]
    nb = t_len // MOBA_BLOCK
    blk = MOBA_BLOCK
    scale = HEAD_DIM ** -0.5
    k2 = k_ref[0]
    k16_ref[...] = k2.astype(bf16)
    vt16_ref[...] = v_ref[0].T.astype(bf16)
    kmean16 = jnp.mean(k2.reshape(nb, blk, LANES), axis=1).astype(bf16)
    lane = lax.broadcasted_iota(jnp.int32, (1, LANES), 1)
    hmask = [(lane // HEAD_DIM == hh).astype(f32) for hh in range(2)]
    krow = lax.broadcasted_iota(jnp.int32, (blk, blk), 0)
    qcol = lax.broadcasted_iota(jnp.int32, (blk, blk), 1)
    causal = krow <= qcol
    drow = lax.broadcasted_iota(jnp.int32, (LANES, blk), 0)

    for qb in range(nb):
        n_keys = (qb + 1) * blk
        q2 = q_ref[0, qb * blk:(qb + 1) * blk, :]
        k_all = k16_ref[0:n_keys, :]
        vt_all = vt16_ref[:, 0:n_keys]
        qm = [q2 * hmask[hh] for hh in range(2)]
        raw = [_dot_nt(k_all, (qm[hh] * scale).astype(bf16)) for hh in range(2)]
        ok = [causal, causal]
        if qb > 0:
            for hh in range(2):
                sel_t = _topk_rows(_dot_nt(kmean16, qm[hh].astype(bf16)), qb)
                past_ok = jnp.broadcast_to(sel_t[0:qb][:, None, :] > 0.5, (qb, blk, blk))
                ok[hh] = jnp.concatenate([past_ok.reshape(qb * blk, blk), causal], axis=0)
        probs, denom = [], []
        for hh in range(2):
            bias = bias_ref[hh, (nb - 1 - qb) * blk:nb * blk, :]
            logits = jnp.where(ok[hh], raw[hh] + bias, NEG_BIG)
            p = jnp.exp(logits - jnp.max(logits, axis=0, keepdims=True))
            denom.append(jnp.sum(p, axis=0, keepdims=True))
            probs.append(p.astype(bf16))
        outs = [_dot(vt_all, probs[hh]) for hh in range(2)]
        o_ref[0, qb * blk:(qb + 1) * blk, :] = jnp.where(drow < HEAD_DIM, outs[0] / denom[0],
                                                         outs[1] / denom[1]).T


def moba_full(q, k, v, bias_t):
    b, t, hw = q.shape
    n_pairs = hw // LANES
    qkv_spec = pl.BlockSpec((1, t, LANES), lambda p, i: (i, 0, p))
    return pl.pallas_call(
        _moba_full_kernel,
        grid=(n_pairs, b),
        in_specs=[qkv_spec, qkv_spec, qkv_spec,
                  pl.BlockSpec((2, t, MOBA_BLOCK), lambda p, i: (p, 0, 0))],
        out_specs=pl.BlockSpec((1, t, LANES), lambda p, i: (i, 0, p)),
        out_shape=jax.ShapeDtypeStruct((b, t, hw), f32),
        scratch_shapes=[pltpu.VMEM((t, LANES), bf16), pltpu.VMEM((LANES, t), bf16)],
        name="moba_full",
        compiler_params=_cparams(("arbitrary", "arbitrary")),
    )(q, k, v, bias_t)


PAGES_PER_STEP = 32


def _pair_queries(q2):
    lane = lax.broadcasted_iota(jnp.int32, (1, LANES), 1)
    return jnp.concatenate([q2 * (lane // HEAD_DIM == hh).astype(f32) for hh in range(2)], axis=0)


def _page_pair(page_ref, p):
    hd, page = page_ref.shape[3:]
    return page_ref[0, 0, 2 * p:2 * p + 2].reshape(2 * hd, page)


def _moba_paged_probs_kernel(pt_ref, q_ref, kn_ref, bias_ref, bown_ref, *refs, n_pairs):
    kp_refs = refs[:PAGES_PER_STEP]
    p_ref, pown_ref, logit_ref, bsum_ref = refs[PAGES_PER_STEP:]
    c = pl.program_id(1)
    n_steps = pl.num_programs(1)
    t_q = q_ref.shape[1]
    page = kp_refs[0].shape[4]
    pages_per_block = MOBA_BLOCK // page
    n_blocks = bsum_ref.shape[0]
    blocks_per_step = PAGES_PER_STEP // pages_per_block
    step_keys = PAGES_PER_STEP * page
    scale = HEAD_DIM ** -0.5

    qf = [_pair_queries(q_ref[0, :, p * LANES:(p + 1) * LANES]) for p in range(n_pairs)]
    q16 = [(x * scale).astype(bf16) for x in qf]
    for p in range(n_pairs):
        kts = [_page_pair(kp_refs[j], p) for j in range(PAGES_PER_STEP)]
        logit_ref[p, c] = _dot(q16[p], jnp.concatenate(kts, axis=1).astype(bf16))
        for jb in range(blocks_per_step):
            ksum = kts[jb * pages_per_block]
            for j in range(jb * pages_per_block + 1, (jb + 1) * pages_per_block):
                ksum = ksum + kts[j]
            bsum_ref[c * blocks_per_step + jb, p] = ksum

    @pl.when(c == n_steps - 1)
    def _():
        n_keys = n_steps * step_keys
        blk_of_key = lax.broadcasted_iota(jnp.int32, (n_blocks, n_keys), 1) // MOBA_BLOCK
        expand = (blk_of_key == lax.broadcasted_iota(jnp.int32, (n_blocks, n_keys), 0)).astype(bf16)
        rq = lax.broadcasted_iota(jnp.int32, (2 * t_q, LANES), 0) % t_q
        ck = lax.broadcasted_iota(jnp.int32, (2 * t_q, LANES), 1)
        own_ok = ck <= rq
        zpad = jnp.zeros((LANES - t_q, LANES), f32)
        blk_lane = lax.broadcasted_iota(jnp.int32, (LANES, LANES), 1)
        for p in range(n_pairs):
            lanes = slice(p * LANES, (p + 1) * LANES)
            kmean_t = jnp.zeros((LANES, LANES), f32)
            for blk in range(n_blocks):
                col = jnp.sum(bsum_ref[blk, p], axis=1, keepdims=True) * (1.0 / MOBA_BLOCK)
                kmean_t = jnp.where(blk_lane == blk, col, kmean_t)
            scores = _dot(qf[p].astype(bf16), kmean_t.astype(bf16))
            sel_t = _topk_rows(scores.T[0:n_blocks], n_blocks)
            chosen = _dot(sel_t.T.astype(bf16), expand)
            lg = [jnp.where(chosen[:, s * step_keys:(s + 1) * step_keys] > 0.5,
                            logit_ref[p, s] + bias_ref[p, s], NEG_BIG) for s in range(n_steps)]
            kn = jnp.concatenate([kn_ref[0, :, lanes], zpad], axis=0).astype(bf16)
            own = jnp.where(own_ok, _dot_nt(q16[p], kn) + bown_ref[p], NEG_BIG)
            m = jnp.max(own, axis=1, keepdims=True)
            for x in lg:
                m = jnp.maximum(m, jnp.max(x, axis=1, keepdims=True))
            e_own = jnp.exp(own - m)
            e = [jnp.exp(x - m) for x in lg]
            denom = jnp.sum(e_own, axis=1, keepdims=True)
            for x in e:
                denom = denom + jnp.sum(x, axis=1, keepdims=True)
            inv = 1.0 / denom
            for s in range(n_steps):
                p_ref[0, p, s] = e[s] * inv
            pown_ref[0, p] = e_own * inv


def _moba_paged_mix_kernel(pt_ref, p_ref, pown_ref, vn_ref, *refs, n_pairs):
    vp_refs = refs[:PAGES_PER_STEP]
    o_ref, acc_ref = refs[PAGES_PER_STEP:]
    c = pl.program_id(1)
    n_steps = pl.num_programs(1)
    t_q = vn_ref.shape[1]
    page = vp_refs[0].shape[4]

    @pl.when(c == 0)
    def _():
        acc_ref[...] = jnp.zeros_like(acc_ref)

    for p in range(n_pairs):
        vt = jnp.concatenate([_page_pair(vp_refs[j], p) for j in range(PAGES_PER_STEP)], axis=1)
        acc_ref[p] += _dot_nt(p_ref[0, p, 0].astype(bf16), vt.astype(bf16))

    @pl.when(c == n_steps - 1)
    def _():
        lane = lax.broadcasted_iota(jnp.int32, (1, LANES), 1)
        zpad = jnp.zeros((LANES - t_q, LANES), f32)
        for p in range(n_pairs):
            lanes = slice(p * LANES, (p + 1) * LANES)
            vn = jnp.concatenate([vn_ref[0, :, lanes], zpad], axis=0).astype(bf16)
            acc = acc_ref[p] + _dot(pown_ref[0, p].astype(bf16), vn)
            o_ref[0, :, lanes] = jnp.where(lane < HEAD_DIM, acc[0:t_q], acc[t_q:2 * t_q])


def moba_paged(q, k_new, v_new, cache_k, cache_v, layer, page_table, bias_past, bias_own):
    b, t_q, hw = q.shape
    n_pairs = hw // LANES
    n_pages = page_table.shape[1]
    n_heads, hd, page = cache_k.shape[2:]
    assert page == LANES and hd == HEAD_DIM and n_heads * hd == hw
    assert n_pages % PAGES_PER_STEP == 0 and PAGES_PER_STEP % (MOBA_BLOCK // page) == 0
    n_steps = n_pages // PAGES_PER_STEP
    step_keys = PAGES_PER_STEP * page
    rows = 2 * t_q

    def page_spec(j):
        return pl.BlockSpec((1, 1, n_heads, hd, page),
                            lambda i, c, pt: (layer, pt[i, c * PAGES_PER_STEP + j], 0, 0, 0))

    new_spec = pl.BlockSpec((1, t_q, hw), lambda i, c, pt: (i, 0, 0))
    own_spec = pl.BlockSpec((1, n_pairs, rows, LANES), lambda i, c, pt: (i, 0, 0, 0))
    probs, p_own = pl.pallas_call(
        functools.partial(_moba_paged_probs_kernel, n_pairs=n_pairs),
        grid_spec=pltpu.PrefetchScalarGridSpec(
            num_scalar_prefetch=1,
            grid=(b, n_steps),
            in_specs=[new_spec, new_spec,
                      pl.BlockSpec(bias_past.shape, lambda i, c, pt: (0, 0, 0, 0)),
                      pl.BlockSpec(bias_own.shape, lambda i, c, pt: (0, 0, 0))]
                     + [page_spec(j) for j in range(PAGES_PER_STEP)],
            out_specs=[pl.BlockSpec((1, n_pairs, n_steps, rows, step_keys), lambda i, c, pt: (i, 0, 0, 0, 0)),
                       own_spec],
            scratch_shapes=[pltpu.VMEM((n_pairs, n_steps, rows, step_keys), f32),
                            pltpu.VMEM((n_pages * page // MOBA_BLOCK, n_pairs, LANES, page), f32)]),
        out_shape=[jax.ShapeDtypeStruct((b, n_pairs, n_steps, rows, step_keys), f32),
                   jax.ShapeDtypeStruct((b, n_pairs, rows, LANES), f32)],
        name="moba_paged_probs",
        compiler_params=_cparams(("arbitrary", "arbitrary")),
    )(page_table, q, k_new, bias_past, bias_own, *([cache_k] * PAGES_PER_STEP))
    return pl.pallas_call(
        functools.partial(_moba_paged_mix_kernel, n_pairs=n_pairs),
        grid_spec=pltpu.PrefetchScalarGridSpec(
            num_scalar_prefetch=1,
            grid=(b, n_steps),
            in_specs=[pl.BlockSpec((1, n_pairs, 1, rows, step_keys), lambda i, c, pt: (i, 0, c, 0, 0)),
                      own_spec, new_spec]
                     + [page_spec(j) for j in range(PAGES_PER_STEP)],
            out_specs=new_spec,
            scratch_shapes=[pltpu.VMEM((n_pairs, rows, LANES), f32)]),
        out_shape=jax.ShapeDtypeStruct((b, t_q, hw), f32),
        name="moba_paged_mix",
        compiler_params=_cparams(("arbitrary", "arbitrary")),
    )(page_table, probs, p_own, v_new, *([cache_v] * PAGES_PER_STEP))


def paged_bias_layout(bias_rows_arr, n_past):
    h, t_q, _ = bias_rows_arr.shape
    step_keys = PAGES_PER_STEP * LANES
    n_steps = n_past // step_keys
    past = bias_rows_arr[:, :, :n_past].reshape(h // 2, 2 * t_q, n_steps, step_keys)
    own = bias_rows_arr[:, :, n_past:].reshape(h // 2, 2 * t_q, LANES)
    return past.transpose(0, 2, 1, 3), own


def _rwkv_prep_kernel(p_ref, prev_ref, shift_ref, mu_ref, w0_ref, a0_ref, kk_ref, ka_ref, rk_ref,
                      ww_ref, wa_ref, wg_ref, *o_refs, hw):
    bt, tt, n = p_ref.shape
    j = pl.program_id(1)
    p = p_ref[...].reshape(bt * tt, n)
    first = jnp.where(j == 0, shift_ref[...], prev_ref[:, 7:8, :])
    first = jnp.broadcast_to(first, (bt, tt, n)).reshape(bt * tt, n)
    row = lax.broadcasted_iota(jnp.int32, (bt * tt, 1), 0)
    prev = jnp.where(row % tt == 0, first, pltpu.roll(p, 1, 0))
    xs = p + (prev - p) * mu_ref[...]
    r = xs[:, 0:hw]
    k = xs[:, hw:2 * hw]
    v = xs[:, 2 * hw:3 * hw]
    xwa = xs[:, 3 * hw:3 * hw + LANES]
    xg = xs[:, 3 * hw + LANES:3 * hw + 2 * LANES]
    z = -(w0_ref[...] + _dot_f32(jnp.tanh(xwa), ww_ref[...]))
    softplus = jnp.maximum(z, 0.0) + jnp.log(1.0 + jnp.exp(-jnp.abs(z)))
    log_decay = -jnp.exp(-softplus - 0.5)
    a = _sigmoid(a0_ref[...] + _dot_f32(xwa, wa_ref[...]))
    g = _dot_f32(_sigmoid(xg), wg_ref[...])
    mseg = _seg_sum_matrix(hw, HEAD_DIM)
    kk = k * kk_ref[...]
    kk = kk / jnp.maximum(jnp.sqrt(_seg_reduce(kk * kk, mseg)), 1e-12)
    k2 = k * (1.0 + (a - 1.0) * ka_ref[...])
    bonus = _seg_reduce(r * k2 * rk_ref[...], mseg) * v
    outs = (r, log_decay, k2, v, -kk, kk * a, g, bonus)
    for o_ref, val in zip(o_refs, outs):
        o_ref[...] = val.reshape(bt, tt, hw)


def rwkv_prep(prw, shift0, prm, bt, tt):
    b, t, n = prw.shape
    hw = prm['w0'].shape[-1]
    assert n == 3 * hw + 2 * LANES and tt % 8 == 0
    row = lambda x: x.reshape(1, -1)
    vec = lambda: pl.BlockSpec((1, hw), lambda i, j: (0, 0))
    mat = lambda: pl.BlockSpec((LANES, hw), lambda i, j: (0, 0))
    prev_blk = tt // 8
    return pl.pallas_call(
        functools.partial(_rwkv_prep_kernel, hw=hw),
        grid=(b // bt, t // tt),
        in_specs=[pl.BlockSpec((bt, tt, n), lambda i, j: (i, j, 0)),
                  pl.BlockSpec((bt, 8, n), lambda i, j: (i, jnp.maximum(j * prev_blk - 1, 0), 0)),
                  pl.BlockSpec((bt, 1, n), lambda i, j: (i, 0, 0)),
                  pl.BlockSpec((1, n), lambda i, j: (0, 0)),
                  vec(), vec(), vec(), vec(), vec(), mat(), mat(), mat()],
        out_specs=[pl.BlockSpec((bt, tt, hw), lambda i, j: (i, j, 0))] * 8,
        out_shape=[jax.ShapeDtypeStruct((b, t, hw), f32)] * 8,
        name="rwkv_prep",
        compiler_params=_cparams(("arbitrary", "arbitrary")),
    )(prw, prw, shift0.reshape(b, 1, n), row(prm['mu']), row(prm['w0']), row(prm['a0']),
      row(prm['k_k']), row(prm['k_a']), row(prm['r_k']), prm['w_w2p'], prm['w_a2p'], prm['w_g2'])


def rwkv_prep_params(mu, w0, w_w2, a0, w_a2, w_g2, k_k, k_a, r_k):
    hw = w0.shape[-1]
    zeros = jnp.zeros((LANES - DECAY_LORA, hw), w_w2.dtype)
    return dict(mu=mu, w0=w0, a0=a0, k_k=k_k, k_a=k_a, r_k=r_k.reshape(-1),
                w_w2p=jnp.concatenate([w_w2, zeros], axis=0),
                w_a2p=jnp.concatenate([zeros, w_a2], axis=0), w_g2=w_g2)


def _split3(x):
    hi = x.astype(bf16)
    return hi, (x - hi.astype(f32)).astype(bf16)


def _dot3(a, b):
    ah, al = _split3(a)
    bh, bl = _split3(b)
    return _dot(ah, bh) + (_dot(ah, bl) + _dot(al, bh))


def _dot3_nt(a, b):
    ah, al = _split3(a)
    bh, bl = _split3(b)
    return _dot_nt(ah, bh) + (_dot_nt(ah, bl) + _dot_nt(al, bh))


def _dot3_tn(a, b):
    return _dot3(a.T, b)


def _rwkv_chunk_maps_kernel(r_ref, lw_ref, k_ref, v_ref, x_ref, b_ref, m_ref, n_ref, p_ref, q_ref,
                            *, n_pairs, c_len):
    two_c = 2 * c_len
    rr = lax.broadcasted_iota(jnp.int32, (two_c, two_c), 0)
    cc = lax.broadcasted_iota(jnp.int32, (two_c, two_c), 1)
    same_head = (rr // c_len) == (cc // c_len)
    strict = jnp.logical_and(same_head, (cc % c_len) < (rr % c_len))
    incl = jnp.logical_and(same_head, (cc % c_len) <= (rr % c_len))
    eye = (rr == cc).astype(f32)
    row_head = lax.broadcasted_iota(jnp.int32, (two_c, LANES), 0) // c_len
    lane_head = lax.broadcasted_iota(jnp.int32, (two_c, LANES), 1) // HEAD_DIM
    bd = row_head == lane_head
    eye_k = (lax.broadcasted_iota(jnp.int32, (LANES, LANES), 0)
             == lax.broadcasted_iota(jnp.int32, (LANES, LANES), 1)).astype(f32)
    trow = lax.broadcasted_iota(jnp.int32, (c_len, LANES), 0)
    n_doubling = max(1, (c_len - 1).bit_length())

    def stack(a):
        return jnp.where(bd, jnp.concatenate([a, a], axis=0), 0.0)

    jobs = [(bi, ci, p) for bi in range(r_ref.shape[0]) for ci in range(r_ref.shape[1] // c_len)
            for p in range(n_pairs)]
    ops = []
    for bi, ci, p in jobs:
        rows = slice(ci * c_len, (ci + 1) * c_len)
        lanes = slice(p * LANES, (p + 1) * LANES)
        lw = lw_ref[bi, rows, lanes]
        cum = lw
        sh = 1
        while sh < c_len:
            cum = cum + jnp.where(trow >= sh, pltpu.roll(cum, sh, 0), 0.0)
            sh *= 2
        gam = jnp.exp(cum)
        inv_gam = jnp.exp(-cum)
        ops.append(dict(
            a=stack(x_ref[bi, rows, lanes] * jnp.exp(cum - lw)), r=stack(r_ref[bi, rows, lanes] * gam),
            b=stack(b_ref[bi, rows, lanes] * inv_gam), k=stack(k_ref[bi, rows, lanes] * inv_gam),
            v=stack(v_ref[bi, rows, lanes]), g_end=gam[c_len - 1:c_len, :]))
    for o in ops:
        bk = jnp.concatenate([o['b'], o['k']], axis=0)
        gram_a = _dot3_nt(o['a'], bk)
        gram_r = _dot_nt_f32(o['r'], bk)
        o['l_ab'] = jnp.where(strict, gram_a[:, 0:two_c], 0.0)
        o['l_ak'] = jnp.where(strict, gram_a[:, two_c:2 * two_c], 0.0)
        o['m_rb'] = jnp.where(incl, gram_r[:, 0:two_c], 0.0)
        o['m_rk'] = jnp.where(incl, gram_r[:, two_c:2 * two_c], 0.0)
        o['pw'] = o['l_ab']
        o['u'] = o['l_ab']
    for it in range(n_doubling - 1):
        mm = _dot3 if it == 0 else _dot_f32
        for o in ops:
            o['pw'] = mm(o['pw'], o['pw'])
        for o in ops:
            o['u'] = o['u'] + o['pw'] + mm(o['u'], o['pw'])
    for o in ops:
        o['lkv'] = _dot3(o['l_ak'], o['v'])
    for o in ops:
        o['pq1'] = _dot3(eye + o['u'], jnp.concatenate([o['a'], o['lkv']], axis=1))
    for o in ops:
        o['pq2'] = _dot_f32(o['m_rb'], o['pq1'])
        o['mkv'] = _dot_f32(o['m_rk'], o['v'])
    for o in ops:
        o['mn'] = _dot3_tn(o['pq1'], o['b'] * o['g_end'])
        o['vk'] = _dot3_tn(o['v'], o['k'] * o['g_end'])
    for (bi, ci, p), o in zip(jobs, ops):
        p_ref[bi, ci, p] = o['r'] + o['pq2'][:, 0:LANES]
        q_ref[bi, ci, p] = o['pq2'][:, LANES:2 * LANES] + o['mkv']
        m_ref[bi, ci, p] = eye_k * o['g_end'] + o['mn'][0:LANES]
        n_ref[bi, ci, p] = o['mn'][LANES:2 * LANES] + o['vk']


def _rwkv_state_kernel(m_ref, n_ref, p_ref, q_ref, g_ref, bonus_ref, s0_ref, lng_ref, lnb_ref,
                       y_ref, sout_ref, state_ref, *, n_pairs, c_len):
    jc = pl.program_id(1)
    n_steps = pl.num_programs(1)

    @pl.when(jc == 0)
    def _():
        state_ref[...] = s0_ref[0]

    mseg = _seg_mean_matrix(LANES, HEAD_DIM)
    states = [state_ref[p] for p in range(n_pairs)]
    for ci in range(m_ref.shape[1]):
        rows = slice(ci * c_len, (ci + 1) * c_len)
        outs = [_dot3_nt(p_ref[0, ci, p], states[p]) + q_ref[0, ci, p] for p in range(n_pairs)]
        states = [_dot3(states[p], m_ref[0, ci, p]) + n_ref[0, ci, p] for p in range(n_pairs)]
        for p in range(n_pairs):
            lanes = slice(p * LANES, (p + 1) * LANES)
            o = outs[p][0:c_len] + outs[p][c_len:2 * c_len]
            mu = _seg_reduce(o, mseg)
            cen = o - mu
            var = _seg_reduce(cen * cen, mseg)
            y = cen * lax.rsqrt(var + GN_EPS) * lng_ref[:, lanes] + lnb_ref[:, lanes]
            y_ref[0, rows, lanes] = (y + bonus_ref[0, rows, lanes]) * g_ref[0, rows, lanes]
    for p in range(n_pairs):
        state_ref[p] = states[p]

    @pl.when(jc == n_steps - 1)
    def _():
        sout_ref[0] = state_ref[...]


def rwkv_chunked(seq, s0_bd, lnx_g, lnx_b, c_len, maps_chunks, scan_chunks, maps_batch):
    r, lw, k2, v, x, kka, g, bonus = seq
    b, t, hw = r.shape
    n_pairs = hw // LANES
    n_chunks = t // c_len
    assert t % c_len == 0 and n_chunks % maps_chunks == 0 and n_chunks % scan_chunks == 0
    assert b % maps_batch == 0
    two_c = 2 * c_len

    def map_spec(rows, per_step, n_seq=1):
        return pl.BlockSpec((n_seq, per_step, n_pairs, rows, LANES), lambda i, j: (i, j, 0, 0, 0))

    def map_shape(rows):
        return jax.ShapeDtypeStruct((b, n_chunks, n_pairs, rows, LANES), f32)

    seq_spec = lambda per_step, n_seq=1: pl.BlockSpec((n_seq, per_step * c_len, hw), lambda i, j: (i, j, 0))
    maps = pl.pallas_call(
        functools.partial(_rwkv_chunk_maps_kernel, n_pairs=n_pairs, c_len=c_len),
        grid=(b // maps_batch, n_chunks // maps_chunks),
        in_specs=[seq_spec(maps_chunks, maps_batch)] * 6,
        out_specs=[map_spec(LANES, maps_chunks, maps_batch), map_spec(LANES, maps_chunks, maps_batch),
                   map_spec(two_c, maps_chunks, maps_batch), map_spec(two_c, maps_chunks, maps_batch)],
        out_shape=[map_shape(LANES), map_shape(LANES), map_shape(two_c), map_shape(two_c)],
        name="rwkv_chunk_maps",
        compiler_params=_cparams(("arbitrary", "arbitrary")),
    )(r, lw, k2, v, x, kka)
    st_spec = pl.BlockSpec((1, n_pairs, LANES, LANES), lambda i, j: (i, 0, 0, 0))
    vec = pl.BlockSpec((1, hw), lambda i, j: (0, 0))
    return pl.pallas_call(
        functools.partial(_rwkv_state_kernel, n_pairs=n_pairs, c_len=c_len),
        grid=(b, n_chunks // scan_chunks),
        in_specs=[map_spec(LANES, scan_chunks), map_spec(LANES, scan_chunks),
                  map_spec(two_c, scan_chunks), map_spec(two_c, scan_chunks),
                  seq_spec(scan_chunks), seq_spec(scan_chunks), st_spec, vec, vec],
        out_specs=[seq_spec(scan_chunks), st_spec],
        out_shape=[jax.ShapeDtypeStruct((b, t, hw), f32),
                   jax.ShapeDtypeStruct((b, n_pairs, LANES, LANES), f32)],
        scratch_shapes=[pltpu.VMEM((n_pairs, LANES, LANES), f32)],
        name="rwkv_state_scan",
        compiler_params=_cparams(("arbitrary", "arbitrary")),
    )(*maps, g, bonus, s0_bd, lnx_g.reshape(1, hw), lnx_b.reshape(1, hw))


def state_to_blockdiag(s):
    b, h, n, _ = s.shape
    sp = s.reshape(b, h // 2, 2, n, n)
    z = jnp.zeros_like(sp[:, :, 0])
    top = jnp.concatenate([sp[:, :, 0], z], axis=-1)
    bot = jnp.concatenate([z, sp[:, :, 1]], axis=-1)
    return jnp.concatenate([top, bot], axis=-2)


def state_from_blockdiag(sbd):
    b, hp, _, _ = sbd.shape
    n = HEAD_DIM
    return jnp.stack([sbd[:, :, :n, :n], sbd[:, :, n:, n:]], axis=2).reshape(b, hp * 2, n, n)


def _layer_norm(z, g, b):
    mu = jnp.mean(z, axis=-1, keepdims=True)
    c = z - mu
    var = jnp.mean(c * c, axis=-1, keepdims=True)
    return c * lax.rsqrt(var + LN_EPS) * g + b


def _route(sel, aff, n_groups, per_group, top_k):
    rows = sel.shape[1]
    srow = [sel[e:e + 1, :] for e in range(n_groups * per_group)]
    arow = [aff[e:e + 1, :] for e in range(n_groups * per_group)]
    scores = []
    for g in range(n_groups):
        s = srow[g * per_group:(g + 1) * per_group]
        best = None
        for i in range(per_group):
            for j in range(i + 1, per_group):
                pair = s[i] + s[j]
                best = pair if best is None else jnp.maximum(best, pair)
        scores.append(best)
    top = scores[0]
    for g in range(1, n_groups):
        top = jnp.maximum(top, scores[g])
    taken = jnp.zeros((1, rows), jnp.bool_)
    eidx = lax.broadcasted_iota(jnp.int32, (n_groups * per_group, rows), 0)
    gates = jnp.zeros((n_groups * per_group, rows), f32)
    for g in range(n_groups):
        is_grp = jnp.logical_and(scores[g] == top, jnp.logical_not(taken))
        taken = jnp.logical_or(taken, is_grp)
        s = srow[g * per_group:(g + 1) * per_group]
        a = arow[g * per_group:(g + 1) * per_group]
        chosen = []
        for i in range(per_group):
            cnt = jnp.zeros((1, rows), f32)
            for j in range(per_group):
                if j != i:
                    ahead = (s[j] > s[i]) if j > i else (s[j] >= s[i])
                    cnt = cnt + ahead.astype(f32)
            chosen.append(jnp.logical_and(cnt < top_k, is_grp))
        denom = jnp.zeros((1, rows), f32)
        for i in range(per_group):
            denom = denom + jnp.where(chosen[i], a[i], 0.0)
        for i in range(per_group):
            gate = jnp.where(chosen[i], a[i] / denom, 0.0)
            gates = jnp.where(eidx == g * per_group + i, gate, gates)
    return gates


def _outproj_kernel(x_ref, ya_ref, yb_ref, yc_ref, mod_ref, w_ref, g_ref, b_ref, wr_ref, rb_ref,
                    x1_ref, h2_ref, gates_ref, *, d, alpha):
    bt, tt, _ = x_ref.shape
    rows = bt * tt
    off = 0
    mix = jnp.zeros((rows, d), f32)
    for y_ref in (ya_ref, yb_ref, yc_ref):
        wd = y_ref.shape[2]
        mix = mix + _dot(y_ref[...].reshape(rows, wd).astype(bf16), w_ref[off:off + wd, :])
        off += wd
    g1 = mod_ref[:, :, 2 * d:3 * d]
    z = alpha * x_ref[...] + (1.0 + g1) * mix.reshape(bt, tt, d)
    x1 = _layer_norm(z, g_ref[...], b_ref[...])
    x1_ref[...] = x1
    sh2 = mod_ref[:, :, 3 * d:4 * d]
    sc2 = mod_ref[:, :, 4 * d:5 * d]
    h2 = (x1 * (1.0 + sc2) + sh2).reshape(rows, d)
    h2_ref[...] = h2.astype(bf16).reshape(bt, tt, d)
    aff = _sigmoid(_dot_nt_f32(wr_ref[...], h2))
    gates_ref[...] = _route(aff + rb_ref[...], aff, N_EXPERT_GROUPS,
                            wr_ref.shape[0] // N_EXPERT_GROUPS, TOP_K_EXPERTS)


def out_projection(x, ya, yb, yc, mod, w_out_bf16, ln_g, ln_b, w_router_t, router_bias, alpha, bt, tt):
    b, t, d = x.shape
    n_exp = w_router_t.shape[0]
    rows = bt * tt
    blk = lambda wd: pl.BlockSpec((bt, tt, wd), lambda i, j: (i, j, 0))
    full = lambda a: pl.BlockSpec(a.shape, lambda i, j: (0,) * a.ndim)
    n_tblk = t // tt
    ln_g = ln_g.reshape(1, d)
    ln_b = ln_b.reshape(1, d)
    rb = router_bias.reshape(n_exp, 1)
    return pl.pallas_call(
        functools.partial(_outproj_kernel, d=d, alpha=alpha),
        grid=(b // bt, n_tblk),
        in_specs=[blk(d), blk(ya.shape[2]), blk(yb.shape[2]), blk(yc.shape[2]),
                  pl.BlockSpec((bt, 1, mod.shape[2]), lambda i, j: (i, 0, 0)),
                  full(w_out_bf16), full(ln_g), full(ln_b), full(w_router_t), full(rb)],
        out_specs=[blk(d), blk(d), pl.BlockSpec((n_exp, rows), lambda i, j: (0, i * n_tblk + j))],
        out_shape=[jax.ShapeDtypeStruct((b, t, d), f32), jax.ShapeDtypeStruct((b, t, d), bf16),
                   jax.ShapeDtypeStruct((n_exp, b * t), f32)],
        name="out_projection",
        compiler_params=_cparams(("arbitrary", "arbitrary")),
    )(x, ya, yb, yc, mod, w_out_bf16, ln_g, ln_b, w_router_t, rb)


def _moe_kernel(h_ref, gates_ref, x1_ref, mod_ref, wg_ref, wu_ref, wd_ref, g_ref, b_ref,
                o_ref, *, d, alpha):
    bt, tt, _ = h_ref.shape
    rows = bt * tt
    n_exp, f, _ = wd_ref.shape
    h = h_ref[...].reshape(rows, d)
    gates = gates_ref[...].reshape(rows, gates_ref.shape[2])
    lane = lax.broadcasted_iota(jnp.int32, gates.shape, 1)
    acts = []
    for e in range(n_exp):
        gate = jnp.sum(jnp.where(lane == e, gates, 0.0), axis=1, keepdims=True)
        acts.append((_silu(_dot(h, wg_ref[e])) * _dot(h, wu_ref[e]) * gate).astype(bf16))
    ffn = _dot(jnp.concatenate(acts, axis=1), wd_ref[...].reshape(n_exp * f, d))
    g2 = mod_ref[:, :, 5 * d:6 * d]
    z = alpha * x1_ref[...] + (1.0 + g2) * ffn.reshape(bt, tt, d)
    o_ref[...] = _layer_norm(z, g_ref[...], b_ref[...])


def moe_ffn(h2, gates, x1, mod, wg_bf16, wu_bf16, wd_bf16, ln_g, ln_b, alpha, bt, tt):
    b, t, d = x1.shape
    n_exp = wg_bf16.shape[0]
    f = wg_bf16.shape[2]
    blk = lambda wd: pl.BlockSpec((bt, tt, wd), lambda i, j: (i, j, 0))
    resident = lambda a: pl.BlockSpec(a.shape, lambda i, j: (0, 0, 0), pipeline_mode=pl.Buffered(1))
    return pl.pallas_call(
        functools.partial(_moe_kernel, d=d, alpha=alpha),
        grid=(b // bt, t // tt),
        in_specs=[blk(d), blk(n_exp), blk(d),
                  pl.BlockSpec((bt, 1, mod.shape[2]), lambda i, j: (i, 0, 0)),
                  resident(wg_bf16), resident(wu_bf16), resident(wd_bf16),
                  pl.BlockSpec((1, d), lambda i, j: (0, 0)),
                  pl.BlockSpec((1, d), lambda i, j: (0, 0))],
        out_specs=blk(d),
        out_shape=jax.ShapeDtypeStruct((b, t, d), f32),
        name="moe_ffn",
        compiler_params=_cparams(("arbitrary", "arbitrary")),
    )(h2, gates, x1, mod, wg_bf16, wu_bf16, wd_bf16, ln_g.reshape(1, d), ln_b.reshape(1, d))


def sgu_full_params(w_s, b_s):
    return w_s, jnp.repeat(b_s.T, HEAD_DIM, axis=1)


def sgu_short_params(w_s, b_s, t_len):
    reps = SGU_CHUNK // t_len
    eye = jnp.eye(reps, dtype=w_s.dtype)
    w_small = w_s[:, :t_len, :t_len]
    w_big = jnp.einsum('ab,gts->gatbs', eye, w_small).reshape(w_s.shape[0], SGU_CHUNK, SGU_CHUNK)
    bias = jnp.tile(jnp.repeat(b_s[:, :t_len].T, HEAD_DIM, axis=1), (reps, 1))
    return w_big, bias


PROMPT_ROWS = 512
RWKV_CHUNK = (64, 8, 4, 1)


def _hybrid_layer(x, mod, tile, wts, attend, sgu_prm, shift0, s0_bd, rwkv_chunk, emit_sgu_v, alpha):
    bt, tt = tile
    b, t, d = x.shape
    hw_sgu = sgu_prm[0].shape[0] * HEAD_DIM
    hw_attn = wts['attn_width']
    widths = (2 * hw_sgu, hw_attn, hw_attn, hw_attn, wts['w_in'].shape[1] - 2 * hw_sgu - 3 * hw_attn)
    puv, q, k, v, prw = in_projection(x, mod, wts['w_in'], widths, bt, tt)
    n_rows = b * t
    sgu_rows = min(PROMPT_ROWS, n_rows) if t < SGU_CHUNK else tt
    sgu_in = puv.reshape(1, n_rows, 2 * hw_sgu) if t < SGU_CHUNK else puv
    sgu_out = sgu(sgu_in, sgu_prm[0], sgu_prm[1], wts['sgu_ln_g'], wts['sgu_ln_b'], sgu_rows, emit_sgu_v)
    ya = sgu_out[0].reshape(b, t, hw_sgu)
    sgu_v = sgu_out[1].reshape(b, t, hw_sgu) if emit_sgu_v else None
    yb = attend(q, k, v)
    seq = rwkv_prep(prw, shift0, wts['rwkv'], bt, tt)
    yc, s_bd = rwkv_chunked(seq, s0_bd, wts['lnx_g'], wts['lnx_b'], *rwkv_chunk)
    x1, h2, gates_t = out_projection(x, ya, yb, yc, mod, wts['w_out'], wts['ln1_g'], wts['ln1_b'],
                                     wts['w_router_t'], wts['router_bias'], alpha, bt, tt)
    gates = gates_t.T.reshape(b, t, gates_t.shape[0])
    x2 = moe_ffn(h2, gates, x1, mod, wts['moe_wg'], wts['moe_wu'], wts['moe_wd'],
                 wts['ln2_g'], wts['ln2_b'], alpha, bt, tt)
    return x2, k, v, state_from_blockdiag(s_bd), prw[:, -1], sgu_v


def kernel(x_prompt, x_sample, cache_k, cache_v, state_rwkv, state_shift, page_table, c_prompt, c_sample, w_ada, b_ada, w_in, w_out, sgu_w, sgu_b, sgu_ln_g, sgu_ln_b, rel_bias, rwkv_mu, rwkv_w0, rwkv_w_w2, rwkv_a0, rwkv_w_a2, rwkv_w_g2, rwkv_k_k, rwkv_k_a, rwkv_r_k, rwkv_lnx_g, rwkv_lnx_b, ln1_g, ln1_b, ln2_g, ln2_b, w_router, router_bias, moe_w_gate, moe_w_up, moe_w_down):
    depth = w_ada.shape[0]
    bp, t_p, d = x_prompt.shape
    bs, t_s, _ = x_sample.shape
    n_heads = cache_k.shape[3]
    hw_attn = n_heads * HEAD_DIM
    n_pool, page = cache_k.shape[1], cache_k.shape[2]
    past_len = page_table.shape[1] * page
    rwkv_in = state_shift.shape[2]
    alpha = (2 * depth) ** 0.25

    mods = ada_modulation(jnp.concatenate([c_prompt, c_sample], axis=0), w_ada, b_ada)
    bias_p = bias_tiles(rel_bias, n_heads, t_p // MOBA_BLOCK).reshape(n_heads, t_p, MOBA_BLOCK)
    bias_past, bias_own = paged_bias_layout(
        bias_rows(rel_bias, n_heads, past_len, t_s, past_len + LANES), past_len)

    def attend_paged(layer):
        def attend(q, k, v):
            return moba_paged(q, k, v, cache_kt, cache_vt, layer, page_table, bias_past, bias_own)
        return attend

    cache_kt = cache_k.transpose(0, 1, 3, 4, 2)
    cache_vt = cache_v.transpose(0, 1, 3, 4, 2)
    w_in16, w_out16 = w_in.astype(bf16), w_out.astype(bf16)
    wg16, wu16, wd16 = moe_w_gate.astype(bf16), moe_w_up.astype(bf16), moe_w_down.astype(bf16)
    w_router_t = w_router.T
    zero_shift = jnp.zeros((bp, rwkv_in), f32)
    zero_state = jnp.zeros((bp, n_heads // 2, LANES, LANES), f32)

    xp, xs = x_prompt, x_sample
    outs = [[] for _ in range(9)]
    for l in range(depth):
        wts = dict(
            attn_width=hw_attn, w_in=w_in16[l], w_out=w_out16[l], sgu_ln_g=sgu_ln_g[l], sgu_ln_b=sgu_ln_b[l],
            rwkv=rwkv_prep_params(rwkv_mu[l], rwkv_w0[l], rwkv_w_w2[l], rwkv_a0[l], rwkv_w_a2[l],
                                  rwkv_w_g2[l], rwkv_k_k[l], rwkv_k_a[l], rwkv_r_k[l]),
            lnx_g=rwkv_lnx_g[l], lnx_b=rwkv_lnx_b[l], ln1_g=ln1_g[l], ln1_b=ln1_b[l],
            ln2_g=ln2_g[l], ln2_b=ln2_b[l], w_router_t=w_router_t, router_bias=router_bias,
            moe_wg=wg16[l], moe_wu=wu16[l], moe_wd=wd16[l])
        mod_p = mods[l, :bp][:, None, :]
        mod_s = mods[l, bp:][:, None, :]
        xp, kp, vp, sp, shp, _ = _hybrid_layer(
            xp, mod_p, (1, PROMPT_ROWS), wts, lambda q, k, v: moba_full(q, k, v, bias_p),
            sgu_full_params(sgu_w[l], sgu_b[l]), zero_shift, zero_state, RWKV_CHUNK, False, alpha)
        xs, ks_, vs_, ss, shs, sgu_v = _hybrid_layer(
            xs, mod_s, (bs, t_s), wts, attend_paged(l),
            sgu_short_params(sgu_w[l], sgu_b[l], t_s), state_shift[l],
            state_to_blockdiag(state_rwkv[l]), (t_s, 1, 1, min(bs, 8)), True, alpha)
        heads = lambda a: a.reshape(a.shape[0], a.shape[1], n_heads, HEAD_DIM)
        for lst, val in zip(outs, (heads(kp), heads(vp), heads(ks_), heads(vs_), sp, ss, shp, shs, sgu_v)):
            lst.append(val)
    return (xp, xs) + tuple(jnp.stack(lst) for lst in outs)
```
